```python
import math
import jax, jax.numpy as jnp
from jax import lax
import numpy as np

D_MODEL = 1024
BATCH = 16
SEQ = 4096
DEPTH = 4
DEC_BATCH = 4
DEC_SEQ = 8192
PAST_LEN = 128

POOL_WINDOWS = (2, 4, 8, 16)
POOL_GROUPS = 4
POOL_CH = D_MODEL // 8
A_WIDTH = POOL_GROUPS * POOL_CH
B_Q_HEADS = 8
B_KV_HEADS = 2
B_HEAD_DIM = D_MODEL // 16
B_HALF_WINDOW = 128
B_WIDTH = B_Q_HEADS * B_HEAD_DIM
C_CONFIGS = ((128, 1), (512, 4), (2048, 16))
C_N_GROUPS = 3
C_HEADS = 4
C_HEAD_DIM = D_MODEL // 8
C_WIDTH = C_HEADS * C_HEAD_DIM
C_PROJ = C_N_GROUPS * C_HEADS * C_HEAD_DIM
N_BRANCH = 3
BRANCH_WIDTH = 512
IN_COLS = A_WIDTH + B_WIDTH + 2 * B_KV_HEADS * B_HEAD_DIM + 3 * C_PROJ + N_BRANCH * D_MODEL
ROPE_THETA = 500000.0
ROT_DIV = 4
N_EXPERTS = 16
EXPERT_FF = 2 * D_MODEL
CAPACITY_FACTOR = 2
LN_EPS = 1e-5
DEEPNORM_ALPHA = (2 * DEPTH) ** 0.25
DEEPNORM_BETA = (8 * DEPTH) ** -0.25
NEG_INF = -1e30

kernel_name = 'hybrid_pool_window_dilated_ec_encoder'


def layer_norm(x, g, b):
    xf = x.astype(jnp.float32)
    mu = jnp.mean(xf, axis=-1, keepdims=True)
    var = jnp.mean(jnp.square(xf - mu), axis=-1, keepdims=True)
    return ((xf - mu) * lax.rsqrt(var + LN_EPS) * g + b).astype(x.dtype)


def rotary_tables(S, rot_dim):
    inv = 1.0 / (ROPE_THETA ** (jnp.arange(0, rot_dim, 2, dtype=jnp.float32) / rot_dim))
    ang = jnp.arange(S, dtype=jnp.float32)[:, None] * inv[None, :]
    return jnp.cos(ang), jnp.sin(ang)


def apply_partial_rotary(x, cos, sin):
    half = cos.shape[-1]
    rot = 2 * half
    xr = x[..., :rot].astype(jnp.float32)
    x1, x2 = xr[..., :half], xr[..., half:]
    c = cos[None, :, None, :]
    s = sin[None, :, None, :]
    xr = jnp.concatenate([x1 * c - x2 * s, x2 * c + x1 * s], axis=-1).astype(x.dtype)
    return jnp.concatenate([xr, x[..., rot:]], axis=-1)


def banded_attention(q, k, v, half_window, sink_logit=None):
    B, L, Hq, Dh = q.shape
    Hkv = k.shape[2]
    G = Hq // Hkv
    blk = half_window
    nblk = -(-L // blk)
    Lp = nblk * blk
    pad = Lp - L
    qb = jnp.pad(q, ((0, 0), (0, pad), (0, 0), (0, 0))).reshape(B, nblk, blk, Hkv, G, Dh)

    def windows(t):
        tb = jnp.pad(t, ((0, 0), (blk, pad + blk), (0, 0), (0, 0))).reshape(B, nblk + 2, blk, Hkv, Dh)
        return jnp.concatenate([tb[:, :-2], tb[:, 1:-1], tb[:, 2:]], axis=2)

    kw = windows(k)
    vw = windows(v)
    s = jnp.einsum('bnqhgd,bnkhd->bnhgqk', qb, kw, preferred_element_type=jnp.float32) * (Dh ** -0.5)
    qpos = jnp.arange(nblk)[:, None] * blk + jnp.arange(blk)[None, :]
    kpos = jnp.arange(nblk)[:, None] * blk - blk + jnp.arange(3 * blk)[None, :]
    rel = kpos[:, None, :] - qpos[:, :, None]
    valid = (jnp.abs(rel) <= half_window) & (kpos >= 0)[:, None, :] & (kpos < L)[:, None, :]
    s = jnp.where(valid[None, :, None, None], s, NEG_INF)
    m = jnp.max(s, axis=-1)
    if sink_logit is not None:
        sink = sink_logit.astype(jnp.float32).reshape(1, 1, Hkv, G, 1)
        m = jnp.maximum(m, sink)
    p = jnp.exp(s - m[..., None])
    denom = jnp.sum(p, axis=-1)
    if sink_logit is not None:
        denom = denom + jnp.exp(sink - m)
    o = jnp.einsum('bnhgqk,bnkhd->bnqhgd', p, vw.astype(jnp.float32))
    o = o / jnp.moveaxis(denom, -1, 2)[..., None]
    lse = m + jnp.log(denom)
    o = o.reshape(B, Lp, Hq, Dh)[:, :L].astype(q.dtype)
    lse = jnp.moveaxis(lse, -1, 2).reshape(B, Lp, Hq)[:, :L]
    return o, lse


def dilated_attention(q, k, v, half_steps, dilation):
    B, S, H, Dh = q.shape
    M = S // dilation

    def fold(t):
        return t.reshape(B, M, dilation, H, Dh).transpose(0, 2, 1, 3, 4).reshape(B * dilation, M, H, Dh)

    o, lse = banded_attention(fold(q), fold(k), fold(v), half_steps)
    o = o.reshape(B, dilation, M, H, Dh).transpose(0, 2, 1, 3, 4).reshape(B, S, H, Dh)
    lse = lse.reshape(B, dilation, M, H).transpose(0, 2, 1, 3).reshape(B, S, H)
    return o, lse


def pool_mixer(a, pool_mix, pool_scale):
    B, S, _ = a.shape
    af = a.astype(jnp.float32).reshape(B, S, POOL_GROUPS, POOL_CH)
    cs = jnp.pad(jnp.cumsum(af, axis=1), ((0, 0), (1, 0), (0, 0), (0, 0)))
    pos = jnp.arange(S)
    outs = []
    for g, w in enumerate(POOL_WINDOWS):
        lo = jnp.clip(pos - w // 2, 0, S)
        hi = jnp.clip(pos + w // 2, 0, S)
        csg = cs[:, :, g]
        win_sum = jnp.take(csg, hi, axis=1) - jnp.take(csg, lo, axis=1)
        cnt = (hi - lo).astype(jnp.float32)[None, :, None]
        outs.append(win_sum / cnt - af[:, :, g])
    pooled = jnp.stack(outs, axis=2).astype(a.dtype)
    mixed = jnp.einsum('bsgc,gce->bsge', pooled, pool_mix).reshape(B, S, A_WIDTH)
    return mixed * pool_scale


def token_mixer(h, w_in, pool_mix, pool_scale, sink_logit, w_branch, w_out):
    B, S, _ = h.shape
    z = jnp.einsum('bsd,dc->bsc', h, w_in)
    widths = [A_WIDTH, B_WIDTH, B_KV_HEADS * B_HEAD_DIM, B_KV_HEADS * B_HEAD_DIM,
              C_PROJ, C_PROJ, C_PROJ, N_BRANCH * D_MODEL]
    cuts = [int(c) for c in np.cumsum(widths)[:-1]]
    a, qb, kb, vb, qc, kc, vc, zg = jnp.split(z, cuts, axis=-1)
    o_a = pool_mixer(a, pool_mix, pool_scale)
    cos_b, sin_b = rotary_tables(S, B_HEAD_DIM // ROT_DIV)
    qb = apply_partial_rotary(qb.reshape(B, S, B_Q_HEADS, B_HEAD_DIM), cos_b, sin_b)
    kb = apply_partial_rotary(kb.reshape(B, S, B_KV_HEADS, B_HEAD_DIM), cos_b, sin_b)
    vb = vb.reshape(B, S, B_KV_HEADS, B_HEAD_DIM)
    o_b, _ = banded_attention(qb, kb, vb, B_HALF_WINDOW, sink_logit)
    o_b = o_b.reshape(B, S, B_WIDTH)
    nh = C_N_GROUPS * C_HEADS
    cos_c, sin_c = rotary_tables(S, C_HEAD_DIM // ROT_DIV)
    qc = apply_partial_rotary(qc.reshape(B, S, nh, C_HEAD_DIM), cos_c, sin_c)
    kc = apply_partial_rotary(kc.reshape(B, S, nh, C_HEAD_DIM), cos_c, sin_c)
    vc = vc.reshape(B, S, nh, C_HEAD_DIM)
    outs, lses = [], []
    for g, (window, dil) in enumerate(C_CONFIGS):
        hs = slice(g * C_HEADS, (g + 1) * C_HEADS)
        o, lse = dilated_attention(qc[:, :, hs], kc[:, :, hs], vc[:, :, hs], window // (2 * dil), dil)
        outs.append(o)
        lses.append(lse)
    wts = jax.nn.softmax(jnp.stack(lses, axis=0), axis=0)
    o_c = jnp.einsum('gbsh,gbshd->bshd', wts, jnp.stack(outs, axis=0).astype(jnp.float32))
    o_c = o_c.astype(h.dtype).reshape(B, S, C_WIDTH)
    branches = jnp.stack([o_a, o_b, o_c], axis=2)
    gates = jax.nn.sigmoid(zg.reshape(B, S, N_BRANCH, D_MODEL).astype(jnp.float32)).astype(h.dtype)
    merged = jnp.sum(gates * jnp.einsum('bskc,kcd->bskd', branches, w_branch), axis=2)
    return jnp.einsum('bsd,de->bse', merged, w_out)


def expert_choice_ffn(h, w_router, w_gate_e, w_up_e, w_down_e):
    B, S, D = h.shape
    N = B * S
    cap = CAPACITY_FACTOR * N // N_EXPERTS
    xf = h.reshape(N, D)
    logits = jnp.einsum('nd,de->ne', xf, w_router, preferred_element_type=jnp.float32)
    aff = jax.nn.softmax(logits, axis=-1)
    vals, idx = lax.top_k(aff.T, cap)

    def one_expert(args):
        ids, wg, wu, wd = args
        xe = jnp.take(xf, ids, axis=0)
        hid = jax.nn.silu(xe @ wg) * (xe @ wu)
        return hid @ wd

    ye = lax.map(one_expert, (idx, w_gate_e, w_up_e, w_down_e))
    ye = ye * vals[..., None].astype(ye.dtype)
    y = jnp.zeros((N, D), ye.dtype).at[idx.reshape(-1)].add(ye.reshape(-1, D))
    return y.reshape(B, S, D).astype(h.dtype)


def trunk(x, w_in, pool_mix, pool_scale, sink_logit, w_branch, w_out, ln1_g, ln1_b,
          w_router, w_gate_e, w_up_e, w_down_e, ln2_g, ln2_b):
    for l in range(DEPTH):
        mix = token_mixer(x, w_in[l], pool_mix[l], pool_scale[l], sink_logit[l], w_branch[l], w_out[l])
        x = layer_norm(DEEPNORM_ALPHA * x + mix, ln1_g[l], ln1_b[l])
        ffn = expert_choice_ffn(x, w_router[l], w_gate_e[l], w_up_e[l], w_down_e[l])
        x = layer_norm(DEEPNORM_ALPHA * x + ffn, ln2_g[l], ln2_b[l])
    return x


def setup_inputs(seed: int = 0) -> dict:
    key = jax.random.key(seed)
    ks = jax.random.split(key, 17)

    def nrm(k, shape, scale):
        return jax.random.normal(k, shape, jnp.float32) * scale

    return {
        'x_prompt': nrm(ks[0], (BATCH, SEQ, D_MODEL), 1.0),
        'x_sample': nrm(ks[1], (DEC_BATCH, DEC_SEQ, D_MODEL), 1.0),
        'w_in': nrm(ks[2], (DEPTH, D_MODEL, IN_COLS), D_MODEL ** -0.5),
        'pool_mix': nrm(ks[3], (DEPTH, POOL_GROUPS, POOL_CH, POOL_CH), POOL_CH ** -0.5),
        'pool_scale': 1.0 + nrm(ks[4], (DEPTH, A_WIDTH), 0.02),
        'sink_logit': nrm(ks[5], (DEPTH, B_Q_HEADS), 0.5),
        'w_branch': nrm(ks[6], (DEPTH, N_BRANCH, BRANCH_WIDTH, D_MODEL), BRANCH_WIDTH ** -0.5),
        'w_out': nrm(ks[7], (DEPTH, D_MODEL, D_MODEL), D_MODEL ** -0.5 * DEEPNORM_BETA),
        'ln1_g': 1.0 + nrm(ks[8], (DEPTH, D_MODEL), 0.02),
        'ln1_b': nrm(ks[9], (DEPTH, D_MODEL), 0.02),
        'w_router': nrm(ks[10], (DEPTH, D_MODEL, N_EXPERTS), D_MODEL ** -0.5),
        'w_gate_e': nrm(ks[11], (DEPTH, N_EXPERTS, D_MODEL, EXPERT_FF), D_MODEL ** -0.5),
        'w_up_e': nrm(ks[12], (DEPTH, N_EXPERTS, D_MODEL, EXPERT_FF), D_MODEL ** -0.5),
        'w_down_e': nrm(ks[13], (DEPTH, N_EXPERTS, EXPERT_FF, D_MODEL), EXPERT_FF ** -0.5 * DEEPNORM_BETA),
        'ln2_g': 1.0 + nrm(ks[14], (DEPTH, D_MODEL), 0.02),
        'ln2_b': nrm(ks[15], (DEPTH, D_MODEL), 0.02),
    }


def reference(x_prompt, x_sample, w_in, pool_mix, pool_scale, sink_logit, w_branch, w_out,
              ln1_g, ln1_b, w_router, w_gate_e, w_up_e, w_down_e, ln2_g, ln2_b):
    y_prompt = trunk(x_prompt, w_in, pool_mix, pool_scale, sink_logit, w_branch, w_out, ln1_g, ln1_b,
                     w_router, w_gate_e, w_up_e, w_down_e, ln2_g, ln2_b)
    y_sample = trunk(x_sample, w_in, pool_mix, pool_scale, sink_logit, w_branch, w_out, ln1_g, ln1_b,
                     w_router, w_gate_e, w_up_e, w_down_e, ln2_g, ln2_b)
    return (y_prompt, y_sample)
```

```python
import functools

import numpy as np
import jax
import jax.numpy as jnp
from jax import lax
from jax.experimental import pallas as pl
from jax.experimental.pallas import tpu as pltpu

F32 = jnp.float32
BF16 = jnp.bfloat16
I32 = jnp.int32

POOL_WINDOWS = (2, 4, 8, 16)
POOL_CH = 128
A_WIDTH = 512
B_Q_HEADS = 8
B_KV_HEADS = 2
B_HEAD_DIM = 64
B_HALF_WINDOW = 128
C_CONFIGS = ((128, 1), (512, 4), (2048, 16))
C_HEADS = 4
C_HEAD_DIM = 128
C_GROUP_W = C_HEADS * C_HEAD_DIM
N_EXPERTS = 16
CAPACITY_FACTOR = 2
ROPE_THETA = 500000.0
ROT_DIV = 4
LN_EPS = 1e-5
NEG_INF = -1e30

LANES = 128
SUBLANES = 8
VMEM_LIMIT = 56 * 1024 * 1024


def _cparams(sem):
    return pltpu.CompilerParams(dimension_semantics=sem, vmem_limit_bytes=VMEM_LIMIT)


def _proj_kernel(x_ref, w_ref, *rest, rot_half):
    if rot_half:
        cos_ref, s1_ref, s2_ref, o_ref = rest
    else:
        (o_ref,) = rest
    acc = jnp.dot(x_ref[...], w_ref[...], preferred_element_type=F32)
    if rot_half:
        c, s1, s2 = cos_ref[...], s1_ref[...], s2_ref[...]
        for g in range(acc.shape[1] // LANES):
            a = acc[:, g * LANES:(g + 1) * LANES]
            r = a * c + pltpu.roll(a, LANES - rot_half, 1) * s1 + pltpu.roll(a, rot_half, 1) * s2
            o_ref[:, g * LANES:(g + 1) * LANES] = r.astype(o_ref.dtype)
    else:
        o_ref[...] = acc.astype(o_ref.dtype)


def _proj(x, w, S, tm, tn, out_dtype, rot=None):
    N, K = x.shape
    C = w.shape[1]
    in_specs = [pl.BlockSpec((tm, K), lambda i, j: (i, 0)),
                pl.BlockSpec((K, tn), lambda i, j: (0, j))]
    args = [x, w]
    rot_half = 0
    if rot is not None:
        rot_half, tabs = rot
        spb = S // tm
        in_specs += [pl.BlockSpec((tm, LANES), lambda i, j: (i % spb, 0))] * 3
        args += list(tabs)
    return pl.pallas_call(
        functools.partial(_proj_kernel, rot_half=rot_half),
        out_shape=jax.ShapeDtypeStruct((N, C), out_dtype),
        grid=(N // tm, C // tn),
        in_specs=in_specs,
        out_specs=pl.BlockSpec((tm, tn), lambda i, j: (i, j)),
        compiler_params=_cparams(("parallel", "arbitrary")),
        name="proj",
    )(*args)


def _rot_tables(S, head_dim):
    rot = head_dim // ROT_DIV
    half = rot // 2
    inv = 1.0 / (ROPE_THETA ** (jnp.arange(0, rot, 2, dtype=F32) / rot))
    ang = jnp.arange(S, dtype=F32)[:, None] * inv[None, :]
    cos, sin = jnp.cos(ang), jnp.sin(ang)
    d = np.arange(LANES) % head_dim
    first = d < half
    second = (d >= half) & (d < rot)
    src = np.where(first, d, np.where(second, d - half, 0))
    cos_t = jnp.where(first | second, cos[:, src], 1.0)
    s1_t = jnp.where(first, -sin[:, src], 0.0)
    s2_t = jnp.where(second, sin[:, src], 0.0)
    return half, (cos_t, s1_t, s2_t)


def _band_mask(tq, halo, W, q0, L):
    nk = tq + 2 * halo
    row = lax.broadcasted_iota(I32, (tq, nk), 0)
    col = lax.broadcasted_iota(I32, (tq, nk), 1)
    rel = col - halo - row
    kpos = q0 - halo + col
    return (jnp.abs(rel) <= W) & (kpos >= 0) & (kpos < L)


def _attn_b_kernel(sink_ref, q_ref, kp_ref, km_ref, kn_ref, vp_ref, vm_ref, vn_ref, o_ref, *, S, TQ):
    W = B_HALF_WINDOW
    G = B_Q_HEADS // B_KV_HEADS
    q0 = pl.program_id(1) * TQ
    k_all = jnp.concatenate([kp_ref[...], km_ref[...], kn_ref[...]], axis=0)
    v_all = jnp.concatenate([vp_ref[...], vm_ref[...], vn_ref[...]], axis=0).astype(BF16)
    valid = _band_mask(TQ, W, W, q0, S)
    valid = jnp.concatenate([valid] * G, axis=0)
    q = q_ref[...]
    outs = []
    for j in range(B_KV_HEADS):
        kj = k_all[:, j * B_HEAD_DIM:(j + 1) * B_HEAD_DIM]
        vj = v_all[:, j * B_HEAD_DIM:(j + 1) * B_HEAD_DIM]
        heads = [G * j + g for g in range(G)]
        qs = jnp.concatenate([q[:, h * B_HEAD_DIM:(h + 1) * B_HEAD_DIM] for h in heads], axis=0)
        sink = jnp.concatenate([jnp.full((TQ, 1), sink_ref[h], F32) for h in heads], axis=0)
        s = lax.dot_general(qs, kj, (((1,), (1,)), ((), ())), preferred_element_type=F32)
        s = s * (B_HEAD_DIM ** -0.5)
        s = jnp.where(valid, s, NEG_INF)
        m = jnp.maximum(jnp.max(s, axis=1, keepdims=True), sink)
        p = jnp.exp(s - m)
        denom = jnp.sum(p, axis=1, keepdims=True) + jnp.exp(sink - m)
        o = jnp.dot(p.astype(BF16), vj, preferred_element_type=F32) / denom
        outs += [o[g * TQ:(g + 1) * TQ] for g in range(G)]
    o_ref[...] = jnp.concatenate(outs, axis=1).astype(o_ref.dtype)


def _attn_b(qk, av, sink, B, S, TQ):
    N = B * S
    W = B_HALF_WINDOW
    nq = S // TQ
    r = TQ // W
    kcol = (B_Q_HEADS * B_HEAD_DIM) // LANES
    last = N // W - 1

    def prev(b, i):
        return (jnp.maximum((b * nq + i) * r - 1, 0), kcol)

    def main(b, i):
        return (b * nq + i, kcol)

    def nxt(b, i):
        return (jnp.minimum((b * nq + i + 1) * r, last), kcol)

    return pl.pallas_call(
        functools.partial(_attn_b_kernel, S=S, TQ=TQ),
        out_shape=jax.ShapeDtypeStruct((N, B_Q_HEADS * B_HEAD_DIM), BF16),
        grid=(B, nq),
        in_specs=[pl.BlockSpec(memory_space=pltpu.SMEM),
                  pl.BlockSpec((TQ, B_Q_HEADS * B_HEAD_DIM), lambda b, i: (b * nq + i, 0)),
                  pl.BlockSpec((W, LANES), prev), pl.BlockSpec((TQ, LANES), main), pl.BlockSpec((W, LANES), nxt),
                  pl.BlockSpec((W, LANES), prev), pl.BlockSpec((TQ, LANES), main), pl.BlockSpec((W, LANES), nxt)],
        out_specs=pl.BlockSpec((TQ, B_Q_HEADS * B_HEAD_DIM), lambda b, i: (b * nq + i, 0)),
        compiler_params=_cparams(("parallel", "arbitrary")),
        name="attn_window",
    )(sink, qk, qk, qk, qk, av, av, av)


def _attn_c_kernel(q_ref, kp_ref, km_ref, kn_ref, vp_ref, vm_ref, vn_ref, o_ref, lse_ref, *, M, TQ, W):
    q0 = pl.program_id(2) * TQ
    k_all = jnp.concatenate([kp_ref[...], km_ref[...], kn_ref[...]], axis=0)
    v_all = jnp.concatenate([vp_ref[...], vm_ref[...], vn_ref[...]], axis=0)
    valid = _band_mask(TQ, W, W, q0, M)
    q = q_ref[...]
    lane = lax.broadcasted_iota(I32, (TQ, LANES), 1)
    lse_tile = jnp.zeros((TQ, LANES), F32)
    outs = []
    for h in range(C_HEADS):
        sl = slice(h * C_HEAD_DIM, (h + 1) * C_HEAD_DIM)
        s = lax.dot_general(q[:, sl], k_all[:, sl], (((1,), (1,)), ((), ())), preferred_element_type=F32)
        s = s * (C_HEAD_DIM ** -0.5)
        s = jnp.where(valid, s, NEG_INF)
        m = jnp.max(s, axis=1, keepdims=True)
        p = jnp.exp(s - m)
        denom = jnp.sum(p, axis=1, keepdims=True)
        outs.append(jnp.dot(p.astype(BF16), v_all[:, sl], preferred_element_type=F32) / denom)
        lse_tile = jnp.where(lane == h, m + jnp.log(denom), lse_tile)
    o_ref[...] = jnp.concatenate(outs, axis=1).astype(o_ref.dtype)
    lse_ref[...] = lse_tile


def _attn_c(qk, vz, g, B, S, TQ):
    window, d = C_CONFIGS[g]
    W = window // (2 * d)
    N = B * S
    M = S // d
    nq = M // TQ
    r = TQ // W
    GW = C_GROUP_W
    qk_v = qk.reshape(N // d, d * qk.shape[1])
    vz_v = vz.reshape(N // d, d * vz.shape[1])
    qk_cb = qk.shape[1] // GW
    vz_cb = vz.shape[1] // GW
    n_grp = len(C_CONFIGS)
    last = (N // d) // W - 1

    def q_map(b, rr, i):
        return (b * nq + i, rr * qk_cb + g)

    def mk(col_blocks, off):
        def prev(b, rr, i):
            return (jnp.maximum((b * nq + i) * r - 1, 0), rr * col_blocks + off)

        def main(b, rr, i):
            return (b * nq + i, rr * col_blocks + off)

        def nxt(b, rr, i):
            return (jnp.minimum((b * nq + i + 1) * r, last), rr * col_blocks + off)

        return [pl.BlockSpec((W, GW), prev), pl.BlockSpec((TQ, GW), main), pl.BlockSpec((W, GW), nxt)]

    o, lse = pl.pallas_call(
        functools.partial(_attn_c_kernel, M=M, TQ=TQ, W=W),
        out_shape=(jax.ShapeDtypeStruct((N // d, d * GW), BF16),
                   jax.ShapeDtypeStruct((N // d, d * LANES), F32)),
        grid=(B, d, nq),
        in_specs=[pl.BlockSpec((TQ, GW), q_map)] + mk(qk_cb, n_grp + g) + mk(vz_cb, vz_cb - n_grp + g),
        out_specs=(pl.BlockSpec((TQ, GW), lambda b, rr, i: (b * nq + i, rr)),
                   pl.BlockSpec((TQ, LANES), lambda b, rr, i: (b * nq + i, rr))),
        compiler_params=_cparams(("parallel", "arbitrary", "arbitrary")),
        name=f"attn_dilated_{d}",
    )(qk_v, qk_v, qk_v, qk_v, vz_v, vz_v, vz_v)
    return o.reshape(N, GW), lse.reshape(N, LANES)


def _layer_norm(h, g, b):
    mu = jnp.mean(h, axis=-1, keepdims=True)
    var = jnp.mean(jnp.square(h - mu), axis=-1, keepdims=True)
    return (h - mu) * lax.rsqrt(var + LN_EPS) * g + b


def _router_aff(xb, wr):
    logits = jnp.dot(xb, wr, preferred_element_type=F32)
    lane = lax.broadcasted_iota(I32, logits.shape, 1)
    logits = jnp.where(lane < N_EXPERTS, logits, NEG_INF)
    e = jnp.exp(logits - jnp.max(logits, axis=1, keepdims=True))
    return e / jnp.sum(e, axis=1, keepdims=True)


def _merge_kernel(x_ref, a_ref, ap_ref, an_ref, ob_ref, oc0_ref, oc1_ref, oc2_ref, l0_ref, l1_ref, l2_ref,
                  zg_ref, pmix_ref, pscale_ref, wbr_ref, wout_ref, g_ref, b_ref, wr_ref,
                  x1_ref, aff_ref, *, S, tm, alpha):
    HW = max(POOL_WINDOWS) // 2
    L = tm + 2 * HW
    pos0 = (pl.program_id(0) * tm) % S
    xa = jnp.concatenate([ap_ref[...], a_ref[...], an_ref[...]], axis=0)
    xpos = pos0 - HW + lax.broadcasted_iota(I32, (L, 1), 0)
    xa = jnp.where((xpos >= 0) & (xpos < S), xa, 0.0)
    sums = {}
    t, w = xa, 1
    while w < max(POOL_WINDOWS):
        t = t + pltpu.roll(t, L - w, 0)
        w *= 2
        sums[w] = t
    pos = pos0 + lax.broadcasted_iota(I32, (tm, 1), 0)
    mixed = []
    for g, w in enumerate(POOL_WINDOWS):
        cs = slice(g * POOL_CH, (g + 1) * POOL_CH)
        off = HW - w // 2
        sw = sums[w][:, cs]
        if off:
            sw = pltpu.roll(sw, L - off, 0)
        sw = sw[:tm]
        cnt = (jnp.clip(pos + w // 2, 0, S) - jnp.clip(pos - w // 2, 0, S)).astype(F32)
        pooled = sw / cnt - a_ref[:, cs]
        mixed.append(jnp.dot(pooled.astype(BF16), pmix_ref[g], preferred_element_type=F32) * pscale_ref[:, cs])
    o_a = jnp.concatenate(mixed, axis=1)
    ls = [l0_ref[...], l1_ref[...], l2_ref[...]]
    mx = jnp.maximum(jnp.maximum(ls[0], ls[1]), ls[2])
    es = [jnp.exp(l - mx) for l in ls]
    tot = es[0] + es[1] + es[2]
    ocs = [oc0_ref[...], oc1_ref[...], oc2_ref[...]]
    pieces = []
    for h in range(C_HEADS):
        sl = slice(h * C_HEAD_DIM, (h + 1) * C_HEAD_DIM)
        acc = None
        for gi in range(3):
            term = (es[gi] / tot)[:, h:h + 1] * ocs[gi][:, sl].astype(F32)
            acc = term if acc is None else acc + term
        pieces.append(acc)
    o_c = jnp.concatenate(pieces, axis=1)
    branches = [o_a.astype(BF16), ob_ref[...], o_c.astype(BF16)]
    D = x_ref.shape[1]
    merged = None
    for k in range(3):
        gate = jax.nn.sigmoid(zg_ref[:, k * D:(k + 1) * D].astype(F32))
        term = gate * jnp.dot(branches[k], wbr_ref[k], preferred_element_type=F32)
        merged = term if merged is None else merged + term
    mix = jnp.dot(merged.astype(BF16), wout_ref[...], preferred_element_type=F32)
    x1 = _layer_norm(alpha * x_ref[...] + mix, g_ref[...], b_ref[...])
    x1_ref[...] = x1
    aff = _router_aff(x1.astype(BF16), wr_ref[...])
    aff_ref[...] = aff.T[:N_EXPERTS]


def _merge(x, av, o_b, o_cs, lses, vz, pmix, pscale, wbr, wout, g, b, wr, S, tm, alpha):
    N, D = x.shape
    HW = max(POOL_WINDOWS) // 2
    r = tm // HW
    last = N // HW - 1
    row = lambda i: (i, 0)
    full2 = lambda i: (0, 0)
    full3 = lambda i: (0, 0, 0)
    in_specs = [
        pl.BlockSpec((tm, D), row),
        pl.BlockSpec((tm, A_WIDTH), row),
        pl.BlockSpec((HW, A_WIDTH), lambda i: (jnp.maximum(i * r - 1, 0), 0)),
        pl.BlockSpec((HW, A_WIDTH), lambda i: (jnp.minimum((i + 1) * r, last), 0)),
        pl.BlockSpec((tm, B_Q_HEADS * B_HEAD_DIM), row),
        pl.BlockSpec((tm, C_GROUP_W), row), pl.BlockSpec((tm, C_GROUP_W), row), pl.BlockSpec((tm, C_GROUP_W), row),
        pl.BlockSpec((tm, LANES), row), pl.BlockSpec((tm, LANES), row), pl.BlockSpec((tm, LANES), row),
        pl.BlockSpec((tm, 3 * D), row),
        pl.BlockSpec(pmix.shape, full3),
        pl.BlockSpec(pscale.shape, full2),
        pl.BlockSpec(wbr.shape, full3),
        pl.BlockSpec(wout.shape, full2),
        pl.BlockSpec(g.shape, full2), pl.BlockSpec(b.shape, full2),
        pl.BlockSpec(wr.shape, full2),
    ]
    return pl.pallas_call(
        functools.partial(_merge_kernel, S=S, tm=tm, alpha=alpha),
        out_shape=(jax.ShapeDtypeStruct((N, D), F32), jax.ShapeDtypeStruct((N_EXPERTS, N), F32)),
        grid=(N // tm,),
        in_specs=in_specs,
        out_specs=(pl.BlockSpec((tm, D), row), pl.BlockSpec((N_EXPERTS, tm), lambda i: (0, i))),
        compiler_params=_cparams(("parallel",)),
        name="merge_ln1_router",
    )(x, av, av, av, o_b, *o_cs, *lses, vz, pmix, pscale, wbr, wout, g, b, wr)


def _count(mask):
    return jnp.sum(jnp.sum(jnp.where(mask, 1.0, 0.0), axis=1), axis=1, keepdims=True)


def _thr_kernel(aff_ref, t_ref, need_ref, *, cap):
    bits = pltpu.bitcast(aff_ref[...], I32)

    def body(k, t):
        cand = t | jnp.left_shift(jnp.int32(1), 30 - k)
        return jnp.where(_count(bits >= cand[:, :, None]) >= cap, cand, t)

    t = lax.fori_loop(0, 31, body, jnp.zeros((N_EXPERTS, 1), I32))
    need = cap - _count(bits > t[:, :, None])
    t_ref[...] = jnp.broadcast_to(t, t_ref.shape)
    need_ref[...] = jnp.broadcast_to(need.astype(I32), need_ref.shape)


def _pos_kernel(t_ref, need_ref, aff_ref, clt_ref, ext_ref, lo_ref, hi_ref, *, NC):
    e = pl.program_id(0)
    bits = pltpu.bitcast(aff_ref[...], I32)
    t = t_ref[e]
    need = need_ref[e].astype(F32)
    upper = jnp.where(lax.broadcasted_iota(I32, (LANES, LANES), 0) <= lax.broadcasted_iota(I32, (LANES, LANES), 1),
                      1.0, 0.0).astype(BF16)
    ones = jnp.ones((LANES, LANES), BF16)
    before = jnp.where(lax.broadcasted_iota(I32, (NC, NC), 1) < lax.broadcasted_iota(I32, (NC, NC), 0),
                       1.0, 0.0).astype(BF16)

    def cums(xf):
        xb = xf.astype(BF16)
        local = jnp.dot(xb, upper, preferred_element_type=F32)
        tot = jnp.dot(xb, ones, preferred_element_type=F32)
        offs = jnp.dot(before, tot.astype(BF16), preferred_element_type=F32)
        return local, tot, offs

    eq = jnp.where(bits == t, 1.0, 0.0)
    local, _, offs = cums(eq)
    sel = (bits > t) | ((bits == t) & (local + offs - eq < need))
    local, tot, offs = cums(jnp.where(sel, 1.0, 0.0))
    lo_ref[...] = offs
    hi_ref[...] = offs + tot
    clt_ref[...] = local.T
    chunk = lax.broadcasted_iota(I32, (NC, LANES), 0)
    lane = lax.broadcasted_iota(I32, (NC, LANES), 1)
    offs_i = offs.astype(I32)
    ext = jnp.where(lane == 0, offs_i >> 8,
                    jnp.where(lane == 1, offs_i & 255,
                              jnp.where(lane == 2, chunk >> 8, jnp.where(lane == 3, chunk & 255, 0))))
    ext_ref[...] = ext.astype(F32).T[:SUBLANES]


def _idx_kernel(clt_ref, ext_ref, lo_ref, hi_ref, idx_ref, *, SB):
    s = (pl.program_id(1) * SB + lax.broadcasted_iota(I32, (1, SB), 1)).astype(F32)
    onehot = jnp.where((lo_ref[:, 0:1] <= s) & (s < hi_ref[:, 0:1]), 1.0, 0.0).astype(BF16)
    cg = jnp.dot(clt_ref[...].astype(BF16), onehot, preferred_element_type=F32)
    ex = jnp.dot(ext_ref[...].astype(BF16), onehot, preferred_element_type=F32)
    slot_lo = ex[0:1] * 256.0 + ex[1:2]
    chunk = ex[2:3] * 256.0 + ex[3:4]
    within = jnp.sum(jnp.where(cg <= s - slot_lo, 1.0, 0.0), axis=0, keepdims=True)
    idx_ref[...] = (chunk * LANES + within).astype(I32)


def _select(aff_t, cap):
    E, N = aff_t.shape
    NC = N // LANES
    aff3 = aff_t.reshape(E, NC, LANES)
    t, need = pl.pallas_call(
        functools.partial(_thr_kernel, cap=cap),
        out_shape=(jax.ShapeDtypeStruct((E, LANES), I32), jax.ShapeDtypeStruct((E, LANES), I32)),
        compiler_params=pltpu.CompilerParams(vmem_limit_bytes=VMEM_LIMIT),
        name="select_threshold",
    )(aff3)
    per_e = lambda e: (e, 0, 0)
    smem = pl.BlockSpec(memory_space=pltpu.SMEM)
    clt, ext, lo, hi = pl.pallas_call(
        functools.partial(_pos_kernel, NC=NC),
        out_shape=(jax.ShapeDtypeStruct((E, LANES, NC), F32), jax.ShapeDtypeStruct((E, SUBLANES, NC), F32),
                   jax.ShapeDtypeStruct((E, NC, LANES), F32), jax.ShapeDtypeStruct((E, NC, LANES), F32)),
        grid=(E,),
        in_specs=[smem, smem, pl.BlockSpec((None, NC, LANES), per_e)],
        out_specs=(pl.BlockSpec((None, LANES, NC), per_e), pl.BlockSpec((None, SUBLANES, NC), per_e),
                   pl.BlockSpec((None, NC, LANES), per_e), pl.BlockSpec((None, NC, LANES), per_e)),
        compiler_params=_cparams(("parallel",)),
        name="select_prefix",
    )(t[:, 0], need[:, 0], aff3)
    SB = min(cap, 1024)
    per_e2 = lambda e, s: (e, 0, 0)
    idx = pl.pallas_call(
        functools.partial(_idx_kernel, SB=SB),
        out_shape=jax.ShapeDtypeStruct((E, 1, cap), I32),
        grid=(E, cap // SB),
        in_specs=[pl.BlockSpec((None, LANES, NC), per_e2), pl.BlockSpec((None, SUBLANES, NC), per_e2),
                  pl.BlockSpec((None, NC, LANES), per_e2), pl.BlockSpec((None, NC, LANES), per_e2)],
        out_specs=pl.BlockSpec((None, 1, SB), lambda e, s: (e, 0, s)),
        compiler_params=_cparams(("parallel", "arbitrary")),
        name="select_index",
    )(clt, ext, lo, hi)
    return idx.reshape(E, cap), lo[:, :, 0].astype(I32)


def _gather_copy(x_hbm, o_hbm, sem, src, dst, rows):
    return pltpu.make_async_copy(x_hbm.at[pl.ds(src, rows)], o_hbm.at[pl.ds(dst, rows)], sem)


def _gather_kernel(idx_ref, x_hbm, o_hbm, sem, *, GB):
    base = pl.program_id(0) * GB

    def body(r, carry):
        _gather_copy(x_hbm, o_hbm, sem, idx_ref[0, 0, r], base + r, 1).start()
        return carry

    lax.fori_loop(0, GB, body, 0)
    _gather_copy(x_hbm, o_hbm, sem, 0, base, GB).wait()


def _gather(x, idx_flat, GB):
    N, D = x.shape
    R = idx_flat.shape[0]
    idx3 = idx_flat.reshape(R // GB, 1, GB)
    return pl.pallas_call(
        functools.partial(_gather_kernel, GB=GB),
        out_shape=jax.ShapeDtypeStruct((R, D), x.dtype),
        grid=(R // GB,),
        in_specs=[pl.BlockSpec((1, 1, GB), lambda i: (i, 0, 0), memory_space=pltpu.SMEM),
                  pl.BlockSpec(memory_space=pl.ANY)],
        out_specs=pl.BlockSpec(memory_space=pl.ANY),
        scratch_shapes=[pltpu.SemaphoreType.DMA],
        compiler_params=pltpu.CompilerParams(dimension_semantics=("arbitrary",), has_side_effects=True),
        name="gather_rows",
    )(idx3, x)


def _ffn_kernel(x_ref, wg_ref, wu_ref, wd_ref, wr_ref, o_ref):
    e = pl.program_id(0)
    xb = x_ref[...].astype(BF16)
    aff = _router_aff(xb, wr_ref[...])
    lane = lax.broadcasted_iota(I32, aff.shape, 1)
    val = jnp.sum(jnp.where(lane == e, aff, 0.0), axis=1, keepdims=True)
    gate = jnp.dot(xb, wg_ref[...], preferred_element_type=F32)
    up = jnp.dot(xb, wu_ref[...], preferred_element_type=F32)
    hid = gate * jax.nn.sigmoid(gate) * up
    y = jnp.dot(hid.astype(BF16), wd_ref[...], preferred_element_type=F32)
    o_ref[...] = y * val


def _ffn(xe, wg, wu, wd, wr, cap, tc):
    R, D = xe.shape
    E, _, F = wg.shape
    nb = cap // tc
    return pl.pallas_call(
        _ffn_kernel,
        out_shape=jax.ShapeDtypeStruct((R, D), F32),
        grid=(E, nb),
        in_specs=[pl.BlockSpec((tc, D), lambda e, c: (e * nb + c, 0)),
                  pl.BlockSpec((None, D, F), lambda e, c: (e, 0, 0)),
                  pl.BlockSpec((None, D, F), lambda e, c: (e, 0, 0)),
                  pl.BlockSpec((None, F, D), lambda e, c: (e, 0, 0)),
                  pl.BlockSpec(wr.shape, lambda e, c: (0, 0))],
        out_specs=pl.BlockSpec((tc, D), lambda e, c: (e * nb + c, 0)),
        compiler_params=_cparams(("parallel", "arbitrary")),
        name="expert_swiglu",
    )(xe, wg, wu, wd, wr)


def _combine_kernel(blk_s, chunk_s, lo_s, hi_s, first_s, last_s, idx_ref, ye_ref, x1_ref, g_ref, b_ref,
                    x2_ref, xb_ref, acc_ref, *, T, alpha):
    w = pl.program_id(0)

    @pl.when(first_s[w] == 1)
    def _():
        acc_ref[...] = jnp.zeros_like(acc_ref)

    base = chunk_s[w] * T

    def body(j, carry):
        r = idx_ref[0, 0, j] - base
        acc_ref[pl.ds(r, 1), :] = acc_ref[pl.ds(r, 1), :] + ye_ref[pl.ds(j, 1), :]
        return carry

    lax.fori_loop(lo_s[w], hi_s[w], body, 0)

    @pl.when(last_s[w] == 1)
    def _():
        x2 = _layer_norm(alpha * x1_ref[...] + acc_ref[...], g_ref[...], b_ref[...])
        x2_ref[...] = x2
        xb_ref[...] = x2.astype(BF16)


def _work_list(slot_lo, cap, T, n_items):
    E, NC = slot_lo.shape
    step = T // LANES
    nch = NC // step
    s0 = slot_lo[:, ::step]
    s1 = jnp.concatenate([s0[:, 1:], jnp.full((E, 1), cap, I32)], axis=1)
    b0 = s0 // LANES
    nb = jnp.where(s1 > s0, (s1 + LANES - 1) // LANES - b0, 0)
    nb = nb.at[0].set(jnp.maximum(nb[0], 1))
    order = lambda a: a.T.reshape(-1)
    nb_f, b0_f, s0_f, s1_f = order(nb), order(b0), order(s0), order(s1)
    ends = jnp.cumsum(nb_f)
    starts = ends - nb_f
    w = jnp.arange(n_items, dtype=I32)
    pair = jnp.minimum(jnp.searchsorted(ends, w, side="right"), nb_f.shape[0] - 1).astype(I32)
    valid = w < ends[-1]
    blk_local = b0_f[pair] + (w - starts[pair])
    expert = pair % E
    chunk = jnp.where(valid, pair // E, nch)
    lo = jnp.clip(s0_f[pair] - blk_local * LANES, 0, LANES)
    hi = jnp.clip(s1_f[pair] - blk_local * LANES, 0, LANES)
    hi = jnp.where(valid, jnp.maximum(hi, lo), lo)
    blk = jnp.where(valid, expert * (cap // LANES) + jnp.minimum(blk_local, cap // LANES - 1), 0)
    prev_chunk = jnp.concatenate([jnp.full((1,), -1, I32), chunk[:-1]])
    next_chunk = jnp.concatenate([chunk[1:], jnp.full((1,), nch, I32)])
    first = (valid & (chunk != prev_chunk)).astype(I32)
    last = (valid & (chunk != next_chunk)).astype(I32)
    return (blk.astype(I32), jnp.minimum(chunk, nch - 1).astype(I32), lo.astype(I32), hi.astype(I32), first, last)


def _combine(ye, idx_flat, slot_lo, x1, g, b, cap, T, alpha):
    N, D = x1.shape
    E = slot_lo.shape[0]
    nch = N // T
    n_items = E * cap // LANES + 2 * E * nch + nch
    work = _work_list(slot_lo, cap, T, n_items)
    idx3 = idx_flat.reshape(-1, 1, LANES)
    grid_spec = pltpu.PrefetchScalarGridSpec(
        num_scalar_prefetch=6,
        grid=(n_items,),
        in_specs=[pl.BlockSpec((1, 1, LANES), lambda w, blk, *_: (blk[w], 0, 0), memory_space=pltpu.SMEM),
                  pl.BlockSpec((LANES, D), lambda w, blk, *_: (blk[w], 0)),
                  pl.BlockSpec((T, D), lambda w, blk, chunk, *_: (chunk[w], 0)),
                  pl.BlockSpec(g.shape, lambda w, *_: (0, 0)),
                  pl.BlockSpec(b.shape, lambda w, *_: (0, 0))],
        out_specs=(pl.BlockSpec((T, D), lambda w, blk, chunk, *_: (chunk[w], 0)),
                   pl.BlockSpec((T, D), lambda w, blk, chunk, *_: (chunk[w], 0))),
        scratch_shapes=[pltpu.VMEM((T, D), F32)],
    )
    return pl.pallas_call(
        functools.partial(_combine_kernel, T=T, alpha=alpha),
        out_shape=(jax.ShapeDtypeStruct((N, D), F32), jax.ShapeDtypeStruct((N, D), BF16)),
        grid_spec=grid_spec,
        compiler_params=_cparams(("arbitrary",)),
        name="combine_ln2",
    )(*work, idx3, ye, x1, g, b)


def _tiles(B, S):
    N = B * S
    tm = 1024 if S % 1024 == 0 else S
    return dict(tm_proj=tm, tq_b=min(256, S), tq_c=128, tm_merge=256, tc=min(512, CAPACITY_FACTOR * N // N_EXPERTS),
                gb=min(512, CAPACITY_FACTOR * N // N_EXPERTS), T=1024)


def _prep_layer(l, w_in, pool_mix, pool_scale, sink_logit, w_branch, w_out, ln1_g, ln1_b, w_router,
                w_gate_e, w_up_e, w_down_e, ln2_g, ln2_b):
    D = w_in.shape[1]
    a0, qb0 = 0, A_WIDTH
    kb0 = qb0 + B_Q_HEADS * B_HEAD_DIM
    vb0 = kb0 + B_KV_HEADS * B_HEAD_DIM
    qc0 = vb0 + B_KV_HEADS * B_HEAD_DIM
    vc0 = qc0 + 2 * len(C_CONFIGS) * C_GROUP_W
    zg0 = vc0 + len(C_CONFIGS) * C_GROUP_W
    wl = w_in[l]
    wr = jnp.zeros((D, LANES), F32).at[:, :N_EXPERTS].set(w_router[l])
    return dict(
        w_qkb=wl[:, qb0:vb0].astype(BF16),
        w_avb=jnp.concatenate([wl[:, a0:qb0], wl[:, vb0:qc0]], axis=1).astype(BF16),
        w_qkc=wl[:, qc0:vc0].astype(BF16),
        w_zgv=jnp.concatenate([wl[:, zg0:], wl[:, vc0:zg0]], axis=1).astype(BF16),
        pmix=pool_mix[l].astype(BF16), pscale=pool_scale[l][None, :], sink=sink_logit[l],
        wbr=w_branch[l].astype(BF16), wout=w_out[l].astype(BF16),
        g1=ln1_g[l][None, :], b1=ln1_b[l][None, :], wr=wr.astype(BF16),
        wg=w_gate_e[l].astype(BF16), wu=w_up_e[l].astype(BF16), wd=w_down_e[l].astype(BF16),
        g2=ln2_g[l][None, :], b2=ln2_b[l][None, :],
    )


def _layer(x, xb, p, B, S, alpha, rot_b, rot_c):
    N, D = x.shape
    t = _tiles(B, S)
    cap = CAPACITY_FACTOR * N // N_EXPERTS
    qkb = _proj(xb, p["w_qkb"], S, t["tm_proj"], p["w_qkb"].shape[1], BF16, rot_b)
    avb = _proj(xb, p["w_avb"], S, t["tm_proj"], p["w_avb"].shape[1], F32)
    qkc = _proj(xb, p["w_qkc"], S, t["tm_proj"], 1024, BF16, rot_c)
    zgv = _proj(xb, p["w_zgv"], S, t["tm_proj"], 1536, BF16)
    o_b = _attn_b(qkb, avb, p["sink"], B, S, t["tq_b"])
    o_cs, lses = [], []
    for g in range(len(C_CONFIGS)):
        o, lse = _attn_c(qkc, zgv, g, B, S, t["tq_c"])
        o_cs.append(o)
        lses.append(lse)
    x1, aff_t = _merge(x, avb, o_b, o_cs, lses, zgv, p["pmix"], p["pscale"], p["wbr"], p["wout"], p["g1"], p["b1"],
                       p["wr"], S, t["tm_merge"], alpha)
    idx, slot_lo = _select(aff_t, cap)
    idx_flat = idx.reshape(-1)
    xe = _gather(x1, idx_flat, t["gb"])
    ye = _ffn(xe, p["wg"], p["wu"], p["wd"], p["wr"], cap, t["tc"])
    return _combine(ye, idx_flat, slot_lo, x1, p["g2"], p["b2"], cap, t["T"], alpha)


def _trunk(x, layers, alpha):
    B, S, D = x.shape
    rot_b = _rot_tables(S, B_HEAD_DIM)
    rot_c = _rot_tables(S, C_HEAD_DIM)
    xf = x.reshape(B * S, D)
    xb = xf.astype(BF16)
    for p in layers:
        xf, xb = _layer(xf, xb, p, B, S, alpha, rot_b, rot_c)
    return xf.reshape(B, S, D)


def kernel(x_prompt, x_sample, w_in, pool_mix, pool_scale, sink_logit, w_branch, w_out, ln1_g, ln1_b, w_router,
           w_gate_e, w_up_e, w_down_e, ln2_g, ln2_b):
    depth = w_in.shape[0]
    alpha = (2 * depth) ** 0.25
    layers = [_prep_layer(l, w_in, pool_mix, pool_scale, sink_logit, w_branch, w_out, ln1_g, ln1_b, w_router,
                          w_gate_e, w_up_e, w_down_e, ln2_g, ln2_b) for l in range(depth)]
    return (_trunk(x_prompt, layers, alpha), _trunk(x_sample, layers, alpha))
```

```python
import functools

import numpy as np
import jax
import jax.numpy as jnp
from jax import lax
from jax.experimental import pallas as pl
from jax.experimental.pallas import tpu as pltpu
from jax.experimental.pallas import tpu_sc as plsc

F32 = jnp.float32
BF16 = jnp.bfloat16
I32 = jnp.int32

POOL_WINDOWS = (2, 4, 8, 16)
POOL_CH = 128
A_WIDTH = 512
B_Q_HEADS = 8
B_KV_HEADS = 2
B_HEAD_DIM = 64
B_HALF_WINDOW = 128
C_CONFIGS = ((128, 1), (512, 4), (2048, 16))
C_HEADS = 4
C_HEAD_DIM = 128
C_GROUP_W = C_HEADS * C_HEAD_DIM
N_EXPERTS = 16
CAPACITY_FACTOR = 2
ROPE_THETA = 500000.0
ROT_DIV = 4
LN_EPS = 1e-5
NEG_INF = -1e30

LANES = 128
SUBLANES = 8
VMEM_LIMIT = 56 * 1024 * 1024
SC_CORES = 2
SC_SUBCORES = 16
SC_GATHER_ROWS = 64


def _cparams(sem):
    return pltpu.CompilerParams(dimension_semantics=sem, vmem_limit_bytes=VMEM_LIMIT)


def _proj_kernel(x_ref, w_ref, *rest, rot_half):
    if rot_half:
        cos_ref, s1_ref, s2_ref, o_ref = rest
    else:
        (o_ref,) = rest
    acc = jnp.dot(x_ref[...], w_ref[...], preferred_element_type=F32)
    if rot_half:
        c, s1, s2 = cos_ref[...], s1_ref[...], s2_ref[...]
        for g in range(acc.shape[1] // LANES):
            a = acc[:, g * LANES:(g + 1) * LANES]
            r = a * c + pltpu.roll(a, LANES - rot_half, 1) * s1 + pltpu.roll(a, rot_half, 1) * s2
            o_ref[:, g * LANES:(g + 1) * LANES] = r.astype(o_ref.dtype)
    else:
        o_ref[...] = acc.astype(o_ref.dtype)


def _proj(x, w, S, tm, tn, out_dtype, rot=None):
    N, K = x.shape
    C = w.shape[1]
    in_specs = [pl.BlockSpec((tm, K), lambda i, j: (i, 0)),
                pl.BlockSpec((K, tn), lambda i, j: (0, j))]
    args = [x, w]
    rot_half = 0
    if rot is not None:
        rot_half, tabs = rot
        spb = S // tm
        in_specs += [pl.BlockSpec((tm, LANES), lambda i, j: (i % spb, 0))] * 3
        args += list(tabs)
    return pl.pallas_call(
        functools.partial(_proj_kernel, rot_half=rot_half),
        out_shape=jax.ShapeDtypeStruct((N, C), out_dtype),
        grid=(N // tm, C // tn),
        in_specs=in_specs,
        out_specs=pl.BlockSpec((tm, tn), lambda i, j: (i, j)),
        compiler_params=_cparams(("parallel", "arbitrary")),
        name="proj",
    )(*args)


def _proj_fold_kernel(x_ref, w_ref, cos_ref, s1_ref, s2_ref, o_ref, scr_ref, *, rot_half, d, n_rot):
    acc = jnp.dot(x_ref[...], w_ref[...], preferred_element_type=F32)
    tm, tn = acc.shape

    groups = tn // LANES

    @pl.when(pl.program_id(1) < n_rot)
    def _():
        c, s1, s2 = cos_ref[...], s1_ref[...], s2_ref[...]
        for g in range(groups):
            a = acc[:, g * LANES:(g + 1) * LANES]
            scr_ref[g] = a * c + pltpu.roll(a, LANES - rot_half, 1) * s1 + pltpu.roll(a, rot_half, 1) * s2

    @pl.when(pl.program_id(1) >= n_rot)
    def _():
        for g in range(groups):
            scr_ref[g] = acc[:, g * LANES:(g + 1) * LANES]

    for rr in range(d):
        for g in range(groups):
            o_ref[rr, :, g * LANES:(g + 1) * LANES] = scr_ref[g, pl.ds(rr, tm // d, stride=d), :].astype(o_ref.dtype)


def _proj_fold(x, w, B, S, d, tm, tn, n_rot, rot):
    N, K = x.shape
    C = w.shape[1]
    rot_half, tabs = rot
    spb = S // tm
    tab_spec = pl.BlockSpec((tm, LANES), lambda i, j: (i % spb, 0))
    return pl.pallas_call(
        functools.partial(_proj_fold_kernel, rot_half=rot_half, d=d, n_rot=n_rot),
        out_shape=jax.ShapeDtypeStruct((B * d, S // d, C), BF16),
        grid=(N // tm, C // tn),
        in_specs=[pl.BlockSpec((tm, K), lambda i, j: (i, 0)), pl.BlockSpec((K, tn), lambda i, j: (0, j)),
                  tab_spec, tab_spec, tab_spec],
        out_specs=pl.BlockSpec((d, tm // d, tn), lambda i, j: (i // spb, i % spb, j)),
        scratch_shapes=[pltpu.VMEM((tn // LANES, tm, LANES), F32)],
        compiler_params=_cparams(("parallel", "arbitrary")),
        name=f"proj_fold_{d}",
    )(x, w, *tabs)


def _rot_tables(S, head_dim):
    rot = head_dim // ROT_DIV
    half = rot // 2
    inv = 1.0 / (ROPE_THETA ** (jnp.arange(0, rot, 2, dtype=F32) / rot))
    ang = jnp.arange(S, dtype=F32)[:, None] * inv[None, :]
    cos, sin = jnp.cos(ang), jnp.sin(ang)
    d = np.arange(LANES) % head_dim
    first = d < half
    second = (d >= half) & (d < rot)
    src = np.where(first, d, np.where(second, d - half, 0))
    cos_t = jnp.where(first | second, cos[:, src], 1.0)
    s1_t = jnp.where(first, -sin[:, src], 0.0)
    s2_t = jnp.where(second, sin[:, src], 0.0)
    return half, (cos_t, s1_t, s2_t)


def _band_mask(tq, halo, W, q0, L):
    nk = tq + 2 * halo
    row = lax.broadcasted_iota(I32, (tq, nk), 0)
    col = lax.broadcasted_iota(I32, (tq, nk), 1)
    rel = col - halo - row
    kpos = q0 - halo + col
    return (jnp.abs(rel) <= W) & (kpos >= 0) & (kpos < L)


def _attn_b_kernel(sink_ref, q_ref, kp_ref, km_ref, kn_ref, vp_ref, vm_ref, vn_ref, o_ref, *, S, TQ):
    W = B_HALF_WINDOW
    G = B_Q_HEADS // B_KV_HEADS
    q0 = pl.program_id(1) * TQ
    k_all = jnp.concatenate([kp_ref[...], km_ref[...], kn_ref[...]], axis=0)
    v_all = jnp.concatenate([vp_ref[...], vm_ref[...], vn_ref[...]], axis=0).astype(BF16)
    valid = _band_mask(TQ, W, W, q0, S)
    valid = jnp.concatenate([valid] * G, axis=0)
    q = q_ref[...]
    outs = []
    for j in range(B_KV_HEADS):
        kj = k_all[:, j * B_HEAD_DIM:(j + 1) * B_HEAD_DIM]
        vj = v_all[:, j * B_HEAD_DIM:(j + 1) * B_HEAD_DIM]
        heads = [G * j + g for g in range(G)]
        qs = jnp.concatenate([q[:, h * B_HEAD_DIM:(h + 1) * B_HEAD_DIM] for h in heads], axis=0)
        sink = jnp.concatenate([jnp.full((TQ, 1), sink_ref[h], F32) for h in heads], axis=0)
        s = lax.dot_general(qs, kj, (((1,), (1,)), ((), ())), preferred_element_type=F32)
        s = s * (B_HEAD_DIM ** -0.5)
        s = jnp.where(valid, s, NEG_INF)
        m = jnp.maximum(jnp.max(s, axis=1, keepdims=True), sink)
        p = jnp.exp(s - m)
        denom = jnp.sum(p, axis=1, keepdims=True) + jnp.exp(sink - m)
        o = jnp.dot(p.astype(BF16), vj, preferred_element_type=F32) / denom
        outs += [o[g * TQ:(g + 1) * TQ] for g in range(G)]
    o_ref[...] = jnp.concatenate(outs, axis=1).astype(o_ref.dtype)


def _attn_b(qk, av, sink, B, S, TQ):
    N = B * S
    W = B_HALF_WINDOW
    nq = S // TQ
    r = TQ // W
    kcol = (B_Q_HEADS * B_HEAD_DIM) // LANES
    last = N // W - 1

    def prev(b, i):
        return (jnp.maximum((b * nq + i) * r - 1, 0), kcol)

    def main(b, i):
        return (b * nq + i, kcol)

    def nxt(b, i):
        return (jnp.minimum((b * nq + i + 1) * r, last), kcol)

    return pl.pallas_call(
        functools.partial(_attn_b_kernel, S=S, TQ=TQ),
        out_shape=jax.ShapeDtypeStruct((N, B_Q_HEADS * B_HEAD_DIM), BF16),
        grid=(B, nq),
        in_specs=[pl.BlockSpec(memory_space=pltpu.SMEM),
                  pl.BlockSpec((TQ, B_Q_HEADS * B_HEAD_DIM), lambda b, i: (b * nq + i, 0)),
                  pl.BlockSpec((W, LANES), prev), pl.BlockSpec((TQ, LANES), main), pl.BlockSpec((W, LANES), nxt),
                  pl.BlockSpec((W, LANES), prev), pl.BlockSpec((TQ, LANES), main), pl.BlockSpec((W, LANES), nxt)],
        out_specs=pl.BlockSpec((TQ, B_Q_HEADS * B_HEAD_DIM), lambda b, i: (b * nq + i, 0)),
        compiler_params=_cparams(("parallel", "arbitrary")),
        name="attn_window",
    )(sink, qk, qk, qk, qk, av, av, av)


def _attn_c_kernel(q_ref, kp_ref, km_ref, kn_ref, vp_ref, vm_ref, vn_ref, o_ref, lse_ref, *, M, TQ, W):
    q0 = pl.program_id(1) * TQ
    k_all = jnp.concatenate([kp_ref[...], km_ref[...], kn_ref[...]], axis=0)
    v_all = jnp.concatenate([vp_ref[...], vm_ref[...], vn_ref[...]], axis=0)
    valid = _band_mask(TQ, W, W, q0, M)
    q = q_ref[...]
    lane = lax.broadcasted_iota(I32, (TQ, LANES), 1)
    lse_tile = jnp.zeros((TQ, LANES), F32)
    outs = []
    for h in range(C_HEADS):
        sl = slice(h * C_HEAD_DIM, (h + 1) * C_HEAD_DIM)
        s = lax.dot_general(q[:, sl], k_all[:, sl], (((1,), (1,)), ((), ())), preferred_element_type=F32)
        s = s * (C_HEAD_DIM ** -0.5)
        s = jnp.where(valid, s, NEG_INF)
        m = jnp.max(s, axis=1, keepdims=True)
        p = jnp.exp(s - m)
        denom = jnp.sum(p, axis=1, keepdims=True)
        outs.append(jnp.dot(p.astype(BF16), v_all[:, sl], preferred_element_type=F32) / denom)
        lse_tile = jnp.where(lane == h, m + jnp.log(denom), lse_tile)
    o_ref[...] = jnp.concatenate(outs, axis=1).astype(o_ref.dtype)
    lse_ref[...] = lse_tile


def _attn_c(qkv, g, B, S, TQ):
    window, d = C_CONFIGS[g]
    W = window // (2 * d)
    M = S // d
    r = TQ // W
    GW = C_GROUP_W
    last = M // W - 1

    def mk(col):
        return [pl.BlockSpec((None, W, GW), lambda b, i: (b, jnp.maximum(i * r - 1, 0), col)),
                pl.BlockSpec((None, TQ, GW), lambda b, i: (b, i, col)),
                pl.BlockSpec((None, W, GW), lambda b, i: (b, jnp.minimum((i + 1) * r, last), col))]

    return pl.pallas_call(
        functools.partial(_attn_c_kernel, M=M, TQ=TQ, W=W),
        out_shape=(jax.ShapeDtypeStruct((B * d, M, GW), BF16), jax.ShapeDtypeStruct((B * d, M, LANES), F32)),
        grid=(B * d, M // TQ),
        in_specs=[pl.BlockSpec((None, TQ, GW), lambda b, i: (b, i, 0))] + mk(1) + mk(2),
        out_specs=(pl.BlockSpec((None, TQ, GW), lambda b, i: (b, i, 0)),
                   pl.BlockSpec((None, TQ, LANES), lambda b, i: (b, i, 0))),
        compiler_params=_cparams(("parallel", "arbitrary")),
        name=f"attn_dilated_{d}",
    )(qkv, qkv, qkv, qkv, qkv, qkv, qkv)


def _layer_norm(h, g, b):
    mu = jnp.mean(h, axis=-1, keepdims=True)
    var = jnp.mean(jnp.square(h - mu), axis=-1, keepdims=True)
    return (h - mu) * lax.rsqrt(var + LN_EPS) * g + b


def _router_aff(xb, wr):
    logits = jnp.dot(xb, wr, preferred_element_type=F32)
    lane = lax.broadcasted_iota(I32, logits.shape, 1)
    logits = jnp.where(lane < N_EXPERTS, logits, NEG_INF)
    e = jnp.exp(logits - jnp.max(logits, axis=1, keepdims=True))
    return e / jnp.sum(e, axis=1, keepdims=True)


def _unfold(ref, scr_ref):
    d, rows, width = ref.shape
    if d == 1:
        return ref[0].astype(F32)
    groups = width // LANES
    for rr in range(d):
        blk = ref[rr].astype(F32)
        for g in range(groups):
            scr_ref[g, pl.ds(rr, rows, stride=d), :] = blk[:, g * LANES:(g + 1) * LANES]
    return jnp.concatenate([scr_ref[g] for g in range(groups)], axis=1)


def _merge_kernel(x_ref, a_ref, ap_ref, an_ref, ob_ref, oc0_ref, oc1_ref, oc2_ref, l0_ref, l1_ref, l2_ref,
                  zg_ref, pmix_ref, pscale_ref, wbr_ref, wout_ref, g_ref, b_ref, wr_ref,
                  x1_ref, aff_ref, so1_ref, so2_ref, sl1_ref, sl2_ref, *, S, tm, alpha):
    HW = max(POOL_WINDOWS) // 2
    L = tm + 2 * HW
    pos0 = (pl.program_id(0) * tm) % S
    xa = jnp.concatenate([ap_ref[...], a_ref[...], an_ref[...]], axis=0)
    xpos = pos0 - HW + lax.broadcasted_iota(I32, (L, 1), 0)
    xa = jnp.where((xpos >= 0) & (xpos < S), xa, 0.0)
    sums = {}
    t, w = xa, 1
    while w < max(POOL_WINDOWS):
        t = t + pltpu.roll(t, L - w, 0)
        w *= 2
        sums[w] = t
    pos = pos0 + lax.broadcasted_iota(I32, (tm, 1), 0)
    mixed = []
    for g, w in enumerate(POOL_WINDOWS):
        cs = slice(g * POOL_CH, (g + 1) * POOL_CH)
        off = HW - w // 2
        sw = sums[w][:, cs]
        if off:
            sw = pltpu.roll(sw, L - off, 0)
        sw = sw[:tm]
        cnt = (jnp.clip(pos + w // 2, 0, S) - jnp.clip(pos - w // 2, 0, S)).astype(F32)
        pooled = sw / cnt - a_ref[:, cs]
        mixed.append(jnp.dot(pooled.astype(BF16), pmix_ref[g], preferred_element_type=F32) * pscale_ref[:, cs])
    o_a = jnp.concatenate(mixed, axis=1)
    ls = [_unfold(l0_ref, None), _unfold(l1_ref, sl1_ref), _unfold(l2_ref, sl2_ref)]
    mx = jnp.maximum(jnp.maximum(ls[0], ls[1]), ls[2])
    es = [jnp.exp(l - mx) for l in ls]
    tot = es[0] + es[1] + es[2]
    ocs = [_unfold(oc0_ref, None), _unfold(oc1_ref, so1_ref), _unfold(oc2_ref, so2_ref)]
    pieces = []
    for h in range(C_HEADS):
        sl = slice(h * C_HEAD_DIM, (h + 1) * C_HEAD_DIM)
        acc = None
        for gi in range(3):
            term = (es[gi] / tot)[:, h:h + 1] * ocs[gi][:, sl].astype(F32)
            acc = term if acc is None else acc + term
        pieces.append(acc)
    o_c = jnp.concatenate(pieces, axis=1)
    branches = [o_a.astype(BF16), ob_ref[...], o_c.astype(BF16)]
    D = x_ref.shape[1]
    merged = None
    for k in range(3):
        gate = jax.nn.sigmoid(zg_ref[:, k * D:(k + 1) * D].astype(F32))
        term = gate * jnp.dot(branches[k], wbr_ref[k], preferred_element_type=F32)
        merged = term if merged is None else merged + term
    mix = jnp.dot(merged.astype(BF16), wout_ref[...], preferred_element_type=F32)
    x1 = _layer_norm(alpha * x_ref[...] + mix, g_ref[...], b_ref[...])
    x1_ref[...] = x1
    aff = _router_aff(x1.astype(BF16), wr_ref[...])
    aff_ref[...] = aff.T[:N_EXPERTS]


def _merge(x, av, o_b, o_cs, lses, vz, pmix, pscale, wbr, wout, g, b, wr, S, tm, alpha):
    N, D = x.shape
    HW = max(POOL_WINDOWS) // 2
    r = tm // HW
    last = N // HW - 1
    row = lambda i: (i, 0)
    full2 = lambda i: (0, 0)
    full3 = lambda i: (0, 0, 0)
    spb = S // tm
    dils = [d for _, d in C_CONFIGS]

    def folded(width):
        return [pl.BlockSpec((d, tm // d, width), lambda i: (i // spb, i % spb, 0)) for d in dils]

    in_specs = [
        pl.BlockSpec((tm, D), row),
        pl.BlockSpec((tm, A_WIDTH), row),
        pl.BlockSpec((HW, A_WIDTH), lambda i: (jnp.maximum(i * r - 1, 0), 0)),
        pl.BlockSpec((HW, A_WIDTH), lambda i: (jnp.minimum((i + 1) * r, last), 0)),
        pl.BlockSpec((tm, B_Q_HEADS * B_HEAD_DIM), row),
        *folded(C_GROUP_W),
        *folded(LANES),
        pl.BlockSpec((tm, 3 * D), row),
        pl.BlockSpec(pmix.shape, full3),
        pl.BlockSpec(pscale.shape, full2),
        pl.BlockSpec(wbr.shape, full3),
        pl.BlockSpec(wout.shape, full2),
        pl.BlockSpec(g.shape, full2), pl.BlockSpec(b.shape, full2),
        pl.BlockSpec(wr.shape, full2),
    ]
    return pl.pallas_call(
        functools.partial(_merge_kernel, S=S, tm=tm, alpha=alpha),
        out_shape=(jax.ShapeDtypeStruct((N, D), F32), jax.ShapeDtypeStruct((N_EXPERTS, N), F32)),
        grid=(N // tm,),
        in_specs=in_specs,
        out_specs=(pl.BlockSpec((tm, D), row), pl.BlockSpec((N_EXPERTS, tm), lambda i: (0, i))),
        scratch_shapes=[pltpu.VMEM((C_GROUP_W // LANES, tm, LANES), F32)] * 2 + [pltpu.VMEM((1, tm, LANES), F32)] * 2,
        compiler_params=_cparams(("parallel",)),
        name="merge_ln1_router",
    )(x, av, av, av, o_b, *o_cs, *lses, vz, pmix, pscale, wbr, wout, g, b, wr)


def _count(mask):
    return jnp.sum(jnp.sum(jnp.where(mask, 1.0, 0.0), axis=1), axis=1, keepdims=True)


def _thr_kernel(aff_ref, t_ref, need_ref, *, cap):
    bits = pltpu.bitcast(aff_ref[...], I32)

    def body(k, t):
        cand = t | jnp.left_shift(jnp.int32(1), 30 - k)
        return jnp.where(_count(bits >= cand[:, :, None]) >= cap, cand, t)

    t = lax.fori_loop(0, 31, body, jnp.zeros((N_EXPERTS, 1), I32))
    need = cap - _count(bits > t[:, :, None])
    t_ref[...] = jnp.broadcast_to(t, t_ref.shape)
    need_ref[...] = jnp.broadcast_to(need.astype(I32), need_ref.shape)


def _pos_kernel(t_ref, need_ref, aff_ref, clt_ref, ext_ref, lo_ref, hi_ref, *, NC):
    e = pl.program_id(0)
    bits = pltpu.bitcast(aff_ref[...], I32)
    t = t_ref[e]
    need = need_ref[e].astype(F32)
    upper = jnp.where(lax.broadcasted_iota(I32, (LANES, LANES), 0) <= lax.broadcasted_iota(I32, (LANES, LANES), 1),
                      1.0, 0.0).astype(BF16)
    ones = jnp.ones((LANES, LANES), BF16)
    before = jnp.where(lax.broadcasted_iota(I32, (NC, NC), 1) < lax.broadcasted_iota(I32, (NC, NC), 0),
                       1.0, 0.0).astype(BF16)

    def cums(xf):
        xb = xf.astype(BF16)
        local = jnp.dot(xb, upper, preferred_element_type=F32)
        tot = jnp.dot(xb, ones, preferred_element_type=F32)
        offs = jnp.dot(before, tot.astype(BF16), preferred_element_type=F32)
        return local, tot, offs

    eq = jnp.where(bits == t, 1.0, 0.0)
    local, _, offs = cums(eq)
    sel = (bits > t) | ((bits == t) & (local + offs - eq < need))
    local, tot, offs = cums(jnp.where(sel, 1.0, 0.0))
    lo_ref[...] = offs
    hi_ref[...] = offs + tot
    clt_ref[...] = local.T
    chunk = lax.broadcasted_iota(I32, (NC, LANES), 0)
    lane = lax.broadcasted_iota(I32, (NC, LANES), 1)
    offs_i = offs.astype(I32)
    ext = jnp.where(lane == 0, offs_i >> 8,
                    jnp.where(lane == 1, offs_i & 255,
                              jnp.where(lane == 2, chunk >> 8, jnp.where(lane == 3, chunk & 255, 0))))
    ext_ref[...] = ext.astype(F32).T[:SUBLANES]


def _idx_kernel(clt_ref, ext_ref, lo_ref, hi_ref, idx_ref, *, SB):
    s = (pl.program_id(1) * SB + lax.broadcasted_iota(I32, (1, SB), 1)).astype(F32)
    onehot = jnp.where((lo_ref[:, 0:1] <= s) & (s < hi_ref[:, 0:1]), 1.0, 0.0).astype(BF16)
    cg = jnp.dot(clt_ref[...].astype(BF16), onehot, preferred_element_type=F32)
    ex = jnp.dot(ext_ref[...].astype(BF16), onehot, preferred_element_type=F32)
    slot_lo = ex[0:1] * 256.0 + ex[1:2]
    chunk = ex[2:3] * 256.0 + ex[3:4]
    within = jnp.sum(jnp.where(cg <= s - slot_lo, 1.0, 0.0), axis=0, keepdims=True)
    idx_ref[...] = (chunk * LANES + within).astype(I32)


def _select(aff_t, cap):
    E, N = aff_t.shape
    NC = N // LANES
    aff3 = aff_t.reshape(E, NC, LANES)
    t, need = pl.pallas_call(
        functools.partial(_thr_kernel, cap=cap),
        out_shape=(jax.ShapeDtypeStruct((E, LANES), I32), jax.ShapeDtypeStruct((E, LANES), I32)),
        compiler_params=pltpu.CompilerParams(vmem_limit_bytes=VMEM_LIMIT),
        name="select_threshold",
    )(aff3)
    per_e = lambda e: (e, 0, 0)
    smem = pl.BlockSpec(memory_space=pltpu.SMEM)
    clt, ext, lo, hi = pl.pallas_call(
        functools.partial(_pos_kernel, NC=NC),
        out_shape=(jax.ShapeDtypeStruct((E, LANES, NC), F32), jax.ShapeDtypeStruct((E, SUBLANES, NC), F32),
                   jax.ShapeDtypeStruct((E, NC, LANES), F32), jax.ShapeDtypeStruct((E, NC, LANES), F32)),
        grid=(E,),
        in_specs=[smem, smem, pl.BlockSpec((None, NC, LANES), per_e)],
        out_specs=(pl.BlockSpec((None, LANES, NC), per_e), pl.BlockSpec((None, SUBLANES, NC), per_e),
                   pl.BlockSpec((None, NC, LANES), per_e), pl.BlockSpec((None, NC, LANES), per_e)),
        compiler_params=_cparams(("parallel",)),
        name="select_prefix",
    )(t[:, 0], need[:, 0], aff3)
    SB = min(cap, 1024)
    per_e2 = lambda e, s: (e, 0, 0)
    idx = pl.pallas_call(
        functools.partial(_idx_kernel, SB=SB),
        out_shape=jax.ShapeDtypeStruct((E, 1, cap), I32),
        grid=(E, cap // SB),
        in_specs=[pl.BlockSpec((None, LANES, NC), per_e2), pl.BlockSpec((None, SUBLANES, NC), per_e2),
                  pl.BlockSpec((None, NC, LANES), per_e2), pl.BlockSpec((None, NC, LANES), per_e2)],
        out_specs=pl.BlockSpec((None, 1, SB), lambda e, s: (e, 0, s)),
        compiler_params=_cparams(("parallel", "arbitrary")),
        name="select_index",
    )(clt, ext, lo, hi)
    return idx.reshape(E, cap), lo[:, :, 0].astype(I32)


def _gather(x, idx_flat, K):
    N, D = x.shape
    R = idx_flat.shape[0]
    workers = SC_CORES * SC_SUBCORES
    per_w = R // workers
    mesh = plsc.VectorSubcoreMesh(core_axis_name="c", subcore_axis_name="s")

    @functools.partial(
        pl.kernel, mesh=mesh,
        out_type=jax.ShapeDtypeStruct((R, D), x.dtype),
        scratch_types=[pltpu.VMEM((K,), I32), pltpu.VMEM((K, D), x.dtype), pltpu.SemaphoreType.DMA],
        name="gather_rows_sc",
    )
    def run(x_hbm, idx_hbm, o_hbm, idx_v, rows_v, sem):
        base = (lax.axis_index("s") * SC_CORES + lax.axis_index("c")) * per_w

        @pl.loop(0, per_w // K)
        def _(c):
            off = base + c * K
            pltpu.sync_copy(idx_hbm.at[pl.ds(off, K)], idx_v)
            pltpu.async_copy(x_hbm.at[idx_v], rows_v, sem).wait()
            pltpu.sync_copy(rows_v, o_hbm.at[pl.ds(off, K)])

    return run(x, idx_flat)


def _ffn_kernel(x_ref, wg_ref, wu_ref, wd_ref, wr_ref, o_ref):
    e = pl.program_id(0)
    xb = x_ref[...].astype(BF16)
    aff = _router_aff(xb, wr_ref[...])
    lane = lax.broadcasted_iota(I32, aff.shape, 1)
    val = jnp.sum(jnp.where(lane == e, aff, 0.0), axis=1, keepdims=True)
    gate = jnp.dot(xb, wg_ref[...], preferred_element_type=F32)
    up = jnp.dot(xb, wu_ref[...], preferred_element_type=F32)
    hid = gate * jax.nn.sigmoid(gate) * up
    y = jnp.dot(hid.astype(BF16), wd_ref[...], preferred_element_type=F32)
    o_ref[...] = y * val


def _ffn(xe, wg, wu, wd, wr, cap, tc):
    R, D = xe.shape
    E, _, F = wg.shape
    nb = cap // tc
    return pl.pallas_call(
        _ffn_kernel,
        out_shape=jax.ShapeDtypeStruct((R, D), F32),
        grid=(E, nb),
        in_specs=[pl.BlockSpec((tc, D), lambda e, c: (e * nb + c, 0)),
                  pl.BlockSpec((None, D, F), lambda e, c: (e, 0, 0)),
                  pl.BlockSpec((None, D, F), lambda e, c: (e, 0, 0)),
                  pl.BlockSpec((None, F, D), lambda e, c: (e, 0, 0)),
                  pl.BlockSpec(wr.shape, lambda e, c: (0, 0))],
        out_specs=pl.BlockSpec((tc, D), lambda e, c: (e * nb + c, 0)),
        compiler_params=_cparams(("parallel", "arbitrary")),
        name="expert_swiglu",
    )(xe, wg, wu, wd, wr)


def _combine_kernel(blk_s, chunk_s, lo_s, hi_s, first_s, last_s, idx_ref, ye_ref, x1_ref, g_ref, b_ref,
                    x2_ref, xb_ref, acc_ref, *, T, alpha):
    w = pl.program_id(0)

    @pl.when(first_s[w] == 1)
    def _():
        acc_ref[...] = jnp.zeros_like(acc_ref)

    base = chunk_s[w] * T

    def body(j, carry):
        r = idx_ref[0, 0, j] - base
        acc_ref[pl.ds(r, 1), :] = acc_ref[pl.ds(r, 1), :] + ye_ref[pl.ds(j, 1), :]
        return carry

    lax.fori_loop(lo_s[w], hi_s[w], body, 0)

    @pl.when(last_s[w] == 1)
    def _():
        x2 = _layer_norm(alpha * x1_ref[...] + acc_ref[...], g_ref[...], b_ref[...])
        x2_ref[...] = x2
        xb_ref[...] = x2.astype(BF16)


def _work_list(slot_lo, cap, T, n_items):
    E, NC = slot_lo.shape
    step = T // LANES
    nch = NC // step
    s0 = slot_lo[:, ::step]
    s1 = jnp.concatenate([s0[:, 1:], jnp.full((E, 1), cap, I32)], axis=1)
    b0 = s0 // LANES
    nb = jnp.where(s1 > s0, (s1 + LANES - 1) // LANES - b0, 0)
    nb = nb.at[0].set(jnp.maximum(nb[0], 1))
    order = lambda a: a.T.reshape(-1)
    nb_f, b0_f, s0_f, s1_f = order(nb), order(b0), order(s0), order(s1)
    ends = jnp.cumsum(nb_f)
    starts = ends - nb_f
    w = jnp.arange(n_items, dtype=I32)
    pair = jnp.minimum(jnp.searchsorted(ends, w, side="right"), nb_f.shape[0] - 1).astype(I32)
    valid = w < ends[-1]
    blk_local = b0_f[pair] + (w - starts[pair])
    expert = pair % E
    chunk = jnp.where(valid, pair // E, nch)
    lo = jnp.clip(s0_f[pair] - blk_local * LANES, 0, LANES)
    hi = jnp.clip(s1_f[pair] - blk_local * LANES, 0, LANES)
    hi = jnp.where(valid, jnp.maximum(hi, lo), lo)
    blk = jnp.where(valid, expert * (cap // LANES) + jnp.minimum(blk_local, cap // LANES - 1), 0)
    prev_chunk = jnp.concatenate([jnp.full((1,), -1, I32), chunk[:-1]])
    next_chunk = jnp.concatenate([chunk[1:], jnp.full((1,), nch, I32)])
    first = (valid & (chunk != prev_chunk)).astype(I32)
    last = (valid & (chunk != next_chunk)).astype(I32)
    return (blk.astype(I32), jnp.minimum(chunk, nch - 1).astype(I32), lo.astype(I32), hi.astype(I32), first, last)


def _combine(ye, idx_flat, slot_lo, x1, g, b, cap, T, alpha):
    N, D = x1.shape
    E = slot_lo.shape[0]
    nch = N // T
    n_items = E * cap // LANES + 2 * E * nch + nch
    work = _work_list(slot_lo, cap, T, n_items)
    idx3 = idx_flat.reshape(-1, 1, LANES)
    grid_spec = pltpu.PrefetchScalarGridSpec(
        num_scalar_prefetch=6,
        grid=(n_items,),
        in_specs=[pl.BlockSpec((1, 1, LANES), lambda w, blk, *_: (blk[w], 0, 0), memory_space=pltpu.SMEM),
                  pl.BlockSpec((LANES, D), lambda w, blk, *_: (blk[w], 0)),
                  pl.BlockSpec((T, D), lambda w, blk, chunk, *_: (chunk[w], 0)),
                  pl.BlockSpec(g.shape, lambda w, *_: (0, 0)),
                  pl.BlockSpec(b.shape, lambda w, *_: (0, 0))],
        out_specs=(pl.BlockSpec((T, D), lambda w, blk, chunk, *_: (chunk[w], 0)),
                   pl.BlockSpec((T, D), lambda w, blk, chunk, *_: (chunk[w], 0))),
        scratch_shapes=[pltpu.VMEM((T, D), F32)],
    )
    return pl.pallas_call(
        functools.partial(_combine_kernel, T=T, alpha=alpha),
        out_shape=(jax.ShapeDtypeStruct((N, D), F32), jax.ShapeDtypeStruct((N, D), BF16)),
        grid_spec=grid_spec,
        compiler_params=_cparams(("arbitrary",)),
        name="combine_ln2",
    )(*work, idx3, ye, x1, g, b)


def _tiles(B, S):
    N = B * S
    tm = 1024 if S % 1024 == 0 else S
    return dict(tm_proj=tm, tq_b=min(256, S), tq_c=128, tm_merge=256, tc=min(512, CAPACITY_FACTOR * N // N_EXPERTS),
                T=1024)


def _prep_layer(l, w_in, pool_mix, pool_scale, sink_logit, w_branch, w_out, ln1_g, ln1_b, w_router,
                w_gate_e, w_up_e, w_down_e, ln2_g, ln2_b):
    D = w_in.shape[1]
    a0, qb0 = 0, A_WIDTH
    kb0 = qb0 + B_Q_HEADS * B_HEAD_DIM
    vb0 = kb0 + B_KV_HEADS * B_HEAD_DIM
    qc0 = vb0 + B_KV_HEADS * B_HEAD_DIM
    kc0 = qc0 + len(C_CONFIGS) * C_GROUP_W
    vc0 = kc0 + len(C_CONFIGS) * C_GROUP_W
    zg0 = vc0 + len(C_CONFIGS) * C_GROUP_W
    wl = w_in[l]
    wr = jnp.zeros((D, LANES), F32).at[:, :N_EXPERTS].set(w_router[l])
    return dict(
        w_qkb=wl[:, qb0:vb0].astype(BF16),
        w_avb=jnp.concatenate([wl[:, a0:qb0], wl[:, vb0:qc0]], axis=1).astype(BF16),
        w_c=[jnp.concatenate([wl[:, c0 + g * C_GROUP_W:c0 + (g + 1) * C_GROUP_W] for c0 in (qc0, kc0, vc0)],
                             axis=1).astype(BF16) for g in range(len(C_CONFIGS))],
        w_zg=wl[:, zg0:].astype(BF16),
        pmix=pool_mix[l].astype(BF16), pscale=pool_scale[l][None, :], sink=sink_logit[l],
        wbr=w_branch[l].astype(BF16), wout=w_out[l].astype(BF16),
        g1=ln1_g[l][None, :], b1=ln1_b[l][None, :], wr=wr.astype(BF16),
        wg=w_gate_e[l].astype(BF16), wu=w_up_e[l].astype(BF16), wd=w_down_e[l].astype(BF16),
        g2=ln2_g[l][None, :], b2=ln2_b[l][None, :],
    )


def _layer(x, xb, p, B, S, alpha, rot_b, rot_c):
    N, D = x.shape
    t = _tiles(B, S)
    cap = CAPACITY_FACTOR * N // N_EXPERTS
    qkb = _proj(xb, p["w_qkb"], S, t["tm_proj"], p["w_qkb"].shape[1], BF16, rot_b)
    avb = _proj(xb, p["w_avb"], S, t["tm_proj"], p["w_avb"].shape[1], F32)
    zg = _proj(xb, p["w_zg"], S, t["tm_proj"], 1024, BF16)
    o_b = _attn_b(qkb, avb, p["sink"], B, S, t["tq_b"])
    o_cs, lses = [], []
    for g, (_, d) in enumerate(C_CONFIGS):
        qkv = _proj_fold(xb, p["w_c"][g], B, S, d, t["tm_proj"], C_GROUP_W, 2, rot_c)
        o, lse = _attn_c(qkv, g, B, S, t["tq_c"])
        o_cs.append(o)
        lses.append(lse)
    x1, aff_t = _merge(x, avb, o_b, o_cs, lses, zg, p["pmix"], p["pscale"], p["wbr"], p["wout"], p["g1"], p["b1"],
                       p["wr"], S, t["tm_merge"], alpha)
    idx, slot_lo = _select(aff_t, cap)
    idx_flat = idx.reshape(-1)
    xe = _gather(x1, idx_flat, SC_GATHER_ROWS)
    ye = _ffn(xe, p["wg"], p["wu"], p["wd"], p["wr"], cap, t["tc"])
    return _combine(ye, idx_flat, slot_lo, x1, p["g2"], p["b2"], cap, t["T"], alpha)


def _trunk(x, layers, alpha):
    B, S, D = x.shape
    rot_b = _rot_tables(S, B_HEAD_DIM)
    rot_c = _rot_tables(S, C_HEAD_DIM)
    xf = x.reshape(B * S, D)
    xb = xf.astype(BF16)
    for p in layers:
        xf, xb = _layer(xf, xb, p, B, S, alpha, rot_b, rot_c)
    return xf.reshape(B, S, D)


def kernel(x_prompt, x_sample, w_in, pool_mix, pool_scale, sink_logit, w_branch, w_out, ln1_g, ln1_b, w_router,
           w_gate_e, w_up_e, w_down_e, ln2_g, ln2_b):
    depth = w_in.shape[0]
    alpha = (2 * depth) ** 0.25
    layers = [_prep_layer(l, w_in, pool_mix, pool_scale, sink_logit, w_branch, w_out, ln1_g, ln1_b, w_router,
                          w_gate_e, w_up_e, w_down_e, ln2_g, ln2_b) for l in range(depth)]
    return (_trunk(x_prompt, layers, alpha), _trunk(x_sample, layers, alpha))
```

```python
import functools

import numpy as np
import jax
import jax.numpy as jnp
from jax import lax
from jax.experimental import pallas as pl
from jax.experimental.pallas import tpu as pltpu
from jax.experimental.pallas import tpu_sc as plsc

F32 = jnp.float32
BF16 = jnp.bfloat16
I32 = jnp.int32

POOL_WINDOWS = (2, 4, 8, 16)
POOL_CH = 128
A_WIDTH = 512
B_Q_HEADS = 8
B_KV_HEADS = 2
B_HEAD_DIM = 64
B_HALF_WINDOW = 128
C_CONFIGS = ((128, 1), (512, 4), (2048, 16))
C_HEADS = 4
C_HEAD_DIM = 128
C_GROUP_W = C_HEADS * C_HEAD_DIM
N_EXPERTS = 16
CAPACITY_FACTOR = 2
ROPE_THETA = 500000.0
ROT_DIV = 4
LN_EPS = 1e-5
NEG_INF = -1e30

LANES = 128
SUBLANES = 8
VMEM_LIMIT = 56 * 1024 * 1024
SC_CORES = 2
SC_SUBCORES = 16
SC_LANES = 16
SC_GATHER_ROWS = 64
PERM_BLOCK = 256


def _cparams(sem):
    return pltpu.CompilerParams(dimension_semantics=sem, vmem_limit_bytes=VMEM_LIMIT)


def _proj_kernel(x_ref, w_ref, *rest, rot_half):
    if rot_half:
        cos_ref, s1_ref, s2_ref, o_ref = rest
    else:
        (o_ref,) = rest
    acc = jnp.dot(x_ref[...], w_ref[...], preferred_element_type=F32)
    if rot_half:
        c, s1, s2 = cos_ref[...], s1_ref[...], s2_ref[...]
        for g in range(acc.shape[1] // LANES):
            a = acc[:, g * LANES:(g + 1) * LANES]
            r = a * c + pltpu.roll(a, LANES - rot_half, 1) * s1 + pltpu.roll(a, rot_half, 1) * s2
            o_ref[:, g * LANES:(g + 1) * LANES] = r.astype(o_ref.dtype)
    else:
        o_ref[...] = acc.astype(o_ref.dtype)


def _proj(x, w, S, tm, tn, out_dtype, rot=None):
    N, K = x.shape
    C = w.shape[1]
    in_specs = [pl.BlockSpec((tm, K), lambda i, j: (i, 0)),
                pl.BlockSpec((K, tn), lambda i, j: (0, j))]
    args = [x, w]
    rot_half = 0
    if rot is not None:
        rot_half, tabs = rot
        spb = S // tm
        in_specs += [pl.BlockSpec((tm, LANES), lambda i, j: (i % spb, 0))] * 3
        args += list(tabs)
    return pl.pallas_call(
        functools.partial(_proj_kernel, rot_half=rot_half),
        out_shape=jax.ShapeDtypeStruct((N, C), out_dtype),
        grid=(N // tm, C // tn),
        in_specs=in_specs,
        out_specs=pl.BlockSpec((tm, tn), lambda i, j: (i, j)),
        compiler_params=_cparams(("parallel", "arbitrary")),
        name="proj",
    )(*args)


def _fold_perm(d, inverse=False):
    i = lax.broadcasted_iota(I32, (PERM_BLOCK, PERM_BLOCK), 1 if inverse else 0)
    j = lax.broadcasted_iota(I32, (PERM_BLOCK, PERM_BLOCK), 0 if inverse else 1)
    per = PERM_BLOCK // d
    return jnp.where(j == (i % per) * d + i // per, 1.0, 0.0).astype(BF16)


def _proj_fold_kernel(x_ref, w_ref, cos_ref, s1_ref, s2_ref, o_ref, scr_ref, *, rot_half, d, n_rot):
    acc = jnp.dot(x_ref[...], w_ref[...], preferred_element_type=F32)
    tm, tn = acc.shape

    @pl.when(pl.program_id(1) < n_rot)
    def _():
        c, s1, s2 = cos_ref[...], s1_ref[...], s2_ref[...]
        for g in range(tn // LANES):
            a = acc[:, g * LANES:(g + 1) * LANES]
            r = a * c + pltpu.roll(a, LANES - rot_half, 1) * s1 + pltpu.roll(a, rot_half, 1) * s2
            scr_ref[:, g * LANES:(g + 1) * LANES] = r.astype(BF16)

    @pl.when(pl.program_id(1) >= n_rot)
    def _():
        scr_ref[...] = acc.astype(BF16)

    if d == 1:
        o_ref[0] = scr_ref[...]
        return
    perm = _fold_perm(d)
    per = PERM_BLOCK // d
    for blk in range(tm // PERM_BLOCK):
        y = jnp.dot(perm, scr_ref[blk * PERM_BLOCK:(blk + 1) * PERM_BLOCK, :], preferred_element_type=F32)
        y = y.astype(o_ref.dtype)
        for rr in range(d):
            o_ref[rr, blk * per:(blk + 1) * per, :] = y[rr * per:(rr + 1) * per]


def _proj_fold(x, w, B, S, d, tm, tn, n_rot, rot):
    N, K = x.shape
    C = w.shape[1]
    rot_half, tabs = rot
    spb = S // tm
    tab_spec = pl.BlockSpec((tm, LANES), lambda i, j: (i % spb, 0))
    return pl.pallas_call(
        functools.partial(_proj_fold_kernel, rot_half=rot_half, d=d, n_rot=n_rot),
        out_shape=jax.ShapeDtypeStruct((B * d, S // d, C), BF16),
        grid=(N // tm, C // tn),
        in_specs=[pl.BlockSpec((tm, K), lambda i, j: (i, 0)), pl.BlockSpec((K, tn), lambda i, j: (0, j)),
                  tab_spec, tab_spec, tab_spec],
        out_specs=pl.BlockSpec((d, tm // d, tn), lambda i, j: (i // spb, i % spb, j)),
        scratch_shapes=[pltpu.VMEM((tm, tn), BF16)],
        compiler_params=_cparams(("parallel", "arbitrary")),
        name=f"proj_fold_{d}",
    )(x, w, *tabs)


def _rot_tables(S, head_dim):
    rot = head_dim // ROT_DIV
    half = rot // 2
    inv = 1.0 / (ROPE_THETA ** (jnp.arange(0, rot, 2, dtype=F32) / rot))
    ang = jnp.arange(S, dtype=F32)[:, None] * inv[None, :]
    cos, sin = jnp.cos(ang), jnp.sin(ang)
    d = np.arange(LANES) % head_dim
    first = d < half
    second = (d >= half) & (d < rot)
    src = np.where(first, d, np.where(second, d - half, 0))
    cos_t = jnp.where(first | second, cos[:, src], 1.0)
    s1_t = jnp.where(first, -sin[:, src], 0.0)
    s2_t = jnp.where(second, sin[:, src], 0.0)
    return half, (cos_t, s1_t, s2_t)


def _band_mask(tq, halo, W, q0, L):
    nk = tq + 2 * halo
    row = lax.broadcasted_iota(I32, (tq, nk), 0)
    col = lax.broadcasted_iota(I32, (tq, nk), 1)
    rel = col - halo - row
    kpos = q0 - halo + col
    return (jnp.abs(rel) <= W) & (kpos >= 0) & (kpos < L)


def _attn_b_kernel(sink_ref, q_ref, kp_ref, km_ref, kn_ref, vp_ref, vm_ref, vn_ref, o_ref, *, S, TQ):
    W = B_HALF_WINDOW
    G = B_Q_HEADS // B_KV_HEADS
    q0 = pl.program_id(1) * TQ
    k_all = jnp.concatenate([kp_ref[...], km_ref[...], kn_ref[...]], axis=0)
    v_all = jnp.concatenate([vp_ref[...], vm_ref[...], vn_ref[...]], axis=0).astype(BF16)
    valid = _band_mask(TQ, W, W, q0, S)
    valid = jnp.concatenate([valid] * G, axis=0)
    q = q_ref[...]
    outs = []
    for j in range(B_KV_HEADS):
        kj = k_all[:, j * B_HEAD_DIM:(j + 1) * B_HEAD_DIM]
        vj = v_all[:, j * B_HEAD_DIM:(j + 1) * B_HEAD_DIM]
        heads = [G * j + g for g in range(G)]
        qs = jnp.concatenate([q[:, h * B_HEAD_DIM:(h + 1) * B_HEAD_DIM] for h in heads], axis=0)
        sink = jnp.concatenate([jnp.full((TQ, 1), sink_ref[h], F32) for h in heads], axis=0)
        s = lax.dot_general(qs, kj, (((1,), (1,)), ((), ())), preferred_element_type=F32)
        s = s * (B_HEAD_DIM ** -0.5)
        s = jnp.where(valid, s, NEG_INF)
        m = jnp.maximum(jnp.max(s, axis=1, keepdims=True), sink)
        p = jnp.exp(s - m)
        denom = jnp.sum(p, axis=1, keepdims=True) + jnp.exp(sink - m)
        o = jnp.dot(p.astype(BF16), vj, preferred_element_type=F32) / denom
        outs += [o[g * TQ:(g + 1) * TQ] for g in range(G)]
    o_ref[...] = jnp.concatenate(outs, axis=1).astype(o_ref.dtype)


def _attn_b(qk, av, sink, B, S, TQ):
    N = B * S
    W = B_HALF_WINDOW
    nq = S // TQ
    r = TQ // W
    kcol = (B_Q_HEADS * B_HEAD_DIM) // LANES
    last = N // W - 1

    def prev(b, i):
        return (jnp.maximum((b * nq + i) * r - 1, 0), kcol)

    def main(b, i):
        return (b * nq + i, kcol)

    def nxt(b, i):
        return (jnp.minimum((b * nq + i + 1) * r, last), kcol)

    return pl.pallas_call(
        functools.partial(_attn_b_kernel, S=S, TQ=TQ),
        out_shape=jax.ShapeDtypeStruct((N, B_Q_HEADS * B_HEAD_DIM), BF16),
        grid=(B, nq),
        in_specs=[pl.BlockSpec(memory_space=pltpu.SMEM),
                  pl.BlockSpec((TQ, B_Q_HEADS * B_HEAD_DIM), lambda b, i: (b * nq + i, 0)),
                  pl.BlockSpec((W, LANES), prev), pl.BlockSpec((TQ, LANES), main), pl.BlockSpec((W, LANES), nxt),
                  pl.BlockSpec((W, LANES), prev), pl.BlockSpec((TQ, LANES), main), pl.BlockSpec((W, LANES), nxt)],
        out_specs=pl.BlockSpec((TQ, B_Q_HEADS * B_HEAD_DIM), lambda b, i: (b * nq + i, 0)),
        compiler_params=_cparams(("parallel", "arbitrary")),
        name="attn_window",
    )(sink, qk, qk, qk, qk, av, av, av)


def _attn_c_kernel(q_ref, kp_ref, km_ref, kn_ref, vp_ref, vm_ref, vn_ref, o_ref, lse_ref, *, M, TQ, W):
    q0 = pl.program_id(1) * TQ
    k_all = jnp.concatenate([kp_ref[...], km_ref[...], kn_ref[...]], axis=0)
    v_all = jnp.concatenate([vp_ref[...], vm_ref[...], vn_ref[...]], axis=0)
    valid = _band_mask(TQ, W, W, q0, M)
    q = q_ref[...]
    lane = lax.broadcasted_iota(I32, (TQ, LANES), 1)
    lse_tile = jnp.zeros((TQ, LANES), F32)
    outs = []
    for h in range(C_HEADS):
        sl = slice(h * C_HEAD_DIM, (h + 1) * C_HEAD_DIM)
        s = lax.dot_general(q[:, sl], k_all[:, sl], (((1,), (1,)), ((), ())), preferred_element_type=F32)
        s = s * (C_HEAD_DIM ** -0.5)
        s = jnp.where(valid, s, NEG_INF)
        m = jnp.max(s, axis=1, keepdims=True)
        p = jnp.exp(s - m)
        denom = jnp.sum(p, axis=1, keepdims=True)
        outs.append(jnp.dot(p.astype(BF16), v_all[:, sl], preferred_element_type=F32) / denom)
        lse_tile = jnp.where(lane == h, m + jnp.log(denom), lse_tile)
    o_ref[...] = jnp.concatenate(outs, axis=1).astype(o_ref.dtype)
    lse_ref[...] = lse_tile


def _attn_c(qkv, g, B, S, TQ):
    window, d = C_CONFIGS[g]
    W = window // (2 * d)
    M = S // d
    r = TQ // W
    GW = C_GROUP_W
    last = M // W - 1

    def mk(col):
        return [pl.BlockSpec((None, W, GW), lambda b, i: (b, jnp.maximum(i * r - 1, 0), col)),
                pl.BlockSpec((None, TQ, GW), lambda b, i: (b, i, col)),
                pl.BlockSpec((None, W, GW), lambda b, i: (b, jnp.minimum((i + 1) * r, last), col))]

    return pl.pallas_call(
        functools.partial(_attn_c_kernel, M=M, TQ=TQ, W=W),
        out_shape=(jax.ShapeDtypeStruct((B * d, M, GW), BF16), jax.ShapeDtypeStruct((B * d, M, LANES), F32)),
        grid=(B * d, M // TQ),
        in_specs=[pl.BlockSpec((None, TQ, GW), lambda b, i: (b, i, 0))] + mk(1) + mk(2),
        out_specs=(pl.BlockSpec((None, TQ, GW), lambda b, i: (b, i, 0)),
                   pl.BlockSpec((None, TQ, LANES), lambda b, i: (b, i, 0))),
        compiler_params=_cparams(("parallel", "arbitrary")),
        name=f"attn_dilated_{d}",
    )(qkv, qkv, qkv, qkv, qkv, qkv, qkv)


def _layer_norm(h, g, b):
    mu = jnp.mean(h, axis=-1, keepdims=True)
    var = jnp.mean(jnp.square(h - mu), axis=-1, keepdims=True)
    return (h - mu) * lax.rsqrt(var + LN_EPS) * g + b


def _router_aff(xb, wr):
    logits = jnp.dot(xb, wr, preferred_element_type=F32)
    lane = lax.broadcasted_iota(I32, logits.shape, 1)
    logits = jnp.where(lane < N_EXPERTS, logits, NEG_INF)
    e = jnp.exp(logits - jnp.max(logits, axis=1, keepdims=True))
    return e / jnp.sum(e, axis=1, keepdims=True)


def _unfold(ref, scr_ref):
    d, rows, _ = ref.shape
    if d == 1:
        return ref[0]
    for rr in range(d):
        scr_ref[pl.ds(rr, rows, stride=d), :] = ref[rr]
    return scr_ref[...]


def _unfold_bf16(ref):
    d, rows, _ = ref.shape
    if d == 1:
        return ref[0].astype(F32)
    per = PERM_BLOCK // d
    perm = _fold_perm(d, inverse=True)
    outs = []
    for blk in range(d * rows // PERM_BLOCK):
        stacked = jnp.concatenate([ref[rr, blk * per:(blk + 1) * per, :] for rr in range(d)], axis=0)
        outs.append(jnp.dot(perm, stacked, preferred_element_type=F32))
    return outs[0] if len(outs) == 1 else jnp.concatenate(outs, axis=0)


def _merge_kernel(x_ref, a_ref, ap_ref, an_ref, ob_ref, oc0_ref, oc1_ref, oc2_ref, l0_ref, l1_ref, l2_ref,
                  zg_ref, pmix_ref, pscale_ref, wbr_ref, wout_ref, g_ref, b_ref, wr_ref,
                  x1_ref, aff_ref, sl1_ref, sl2_ref, *, S, tm, alpha):
    HW = max(POOL_WINDOWS) // 2
    L = tm + 2 * HW
    pos0 = (pl.program_id(0) * tm) % S
    xa = jnp.concatenate([ap_ref[...], a_ref[...], an_ref[...]], axis=0)
    xpos = pos0 - HW + lax.broadcasted_iota(I32, (L, 1), 0)
    xa = jnp.where((xpos >= 0) & (xpos < S), xa, 0.0)
    sums = {}
    t, w = xa, 1
    while w < max(POOL_WINDOWS):
        t = t + pltpu.roll(t, L - w, 0)
        w *= 2
        sums[w] = t
    pos = pos0 + lax.broadcasted_iota(I32, (tm, 1), 0)
    mixed = []
    for g, w in enumerate(POOL_WINDOWS):
        cs = slice(g * POOL_CH, (g + 1) * POOL_CH)
        off = HW - w // 2
        sw = sums[w][:, cs]
        if off:
            sw = pltpu.roll(sw, L - off, 0)
        sw = sw[:tm]
        cnt = (jnp.clip(pos + w // 2, 0, S) - jnp.clip(pos - w // 2, 0, S)).astype(F32)
        pooled = sw / cnt - a_ref[:, cs]
        mixed.append(jnp.dot(pooled.astype(BF16), pmix_ref[g], preferred_element_type=F32) * pscale_ref[:, cs])
    o_a = jnp.concatenate(mixed, axis=1)
    ls = [_unfold(l0_ref, None), _unfold(l1_ref, sl1_ref), _unfold(l2_ref, sl2_ref)]
    mx = jnp.maximum(jnp.maximum(ls[0], ls[1]), ls[2])
    es = [jnp.exp(l - mx) for l in ls]
    tot = es[0] + es[1] + es[2]
    ocs = [_unfold_bf16(oc0_ref), _unfold_bf16(oc1_ref), _unfold_bf16(oc2_ref)]
    pieces = []
    for h in range(C_HEADS):
        sl = slice(h * C_HEAD_DIM, (h + 1) * C_HEAD_DIM)
        acc = None
        for gi in range(3):
            term = (es[gi] / tot)[:, h:h + 1] * ocs[gi][:, sl].astype(F32)
            acc = term if acc is None else acc + term
        pieces.append(acc)
    o_c = jnp.concatenate(pieces, axis=1)
    branches = [o_a.astype(BF16), ob_ref[...], o_c.astype(BF16)]
    D = x_ref.shape[1]
    merged = None
    for k in range(3):
        gate = jax.nn.sigmoid(zg_ref[:, k * D:(k + 1) * D].astype(F32))
        term = gate * jnp.dot(branches[k], wbr_ref[k], preferred_element_type=F32)
        merged = term if merged is None else merged + term
    mix = jnp.dot(merged.astype(BF16), wout_ref[...], preferred_element_type=F32)
    x1 = _layer_norm(alpha * x_ref[...] + mix, g_ref[...], b_ref[...])
    x1_ref[...] = x1
    aff = _router_aff(x1.astype(BF16), wr_ref[...])
    aff_ref[...] = aff.T[:N_EXPERTS]


def _merge(x, av, o_b, o_cs, lses, vz, pmix, pscale, wbr, wout, g, b, wr, S, tm, alpha):
    N, D = x.shape
    HW = max(POOL_WINDOWS) // 2
    r = tm // HW
    last = N // HW - 1
    row = lambda i: (i, 0)
    full2 = lambda i: (0, 0)
    full3 = lambda i: (0, 0, 0)
    spb = S // tm
    dils = [d for _, d in C_CONFIGS]

    def folded(width):
        return [pl.BlockSpec((d, tm // d, width), lambda i: (i // spb, i % spb, 0)) for d in dils]

    in_specs = [
        pl.BlockSpec((tm, D), row),
        pl.BlockSpec((tm, A_WIDTH), row),
        pl.BlockSpec((HW, A_WIDTH), lambda i: (jnp.maximum(i * r - 1, 0), 0)),
        pl.BlockSpec((HW, A_WIDTH), lambda i: (jnp.minimum((i + 1) * r, last), 0)),
        pl.BlockSpec((tm, B_Q_HEADS * B_HEAD_DIM), row),
        *folded(C_GROUP_W),
        *folded(LANES),
        pl.BlockSpec((tm, 3 * D), row),
        pl.BlockSpec(pmix.shape, full3),
        pl.BlockSpec(pscale.shape, full2),
        pl.BlockSpec(wbr.shape, full3),
        pl.BlockSpec(wout.shape, full2),
        pl.BlockSpec(g.shape, full2), pl.BlockSpec(b.shape, full2),
        pl.BlockSpec(wr.shape, full2),
    ]
    return pl.pallas_call(
        functools.partial(_merge_kernel, S=S, tm=tm, alpha=alpha),
        out_shape=(jax.ShapeDtypeStruct((N, D), F32), jax.ShapeDtypeStruct((N_EXPERTS, N), F32)),
        grid=(N // tm,),
        in_specs=in_specs,
        out_specs=(pl.BlockSpec((tm, D), row), pl.BlockSpec((N_EXPERTS, tm), lambda i: (0, i))),
        scratch_shapes=[pltpu.VMEM((tm, LANES), F32)] * 2,
        compiler_params=_cparams(("parallel",)),
        name="merge_ln1_router",
    )(x, av, av, av, o_b, *o_cs, *lses, vz, pmix, pscale, wbr, wout, g, b, wr)


def _count(mask):
    return jnp.sum(jnp.sum(jnp.where(mask, 1.0, 0.0), axis=1), axis=1, keepdims=True)


def _thr_kernel(aff_ref, t_ref, need_ref, *, cap):
    bits = pltpu.bitcast(aff_ref[...], I32)

    def body(k, t):
        cand = t | jnp.left_shift(jnp.int32(1), 30 - k)
        return jnp.where(_count(bits >= cand[:, :, None]) >= cap, cand, t)

    t = lax.fori_loop(0, 31, body, jnp.zeros((N_EXPERTS, 1), I32))
    need = cap - _count(bits > t[:, :, None])
    t_ref[...] = jnp.broadcast_to(t, t_ref.shape)
    need_ref[...] = jnp.broadcast_to(need.astype(I32), need_ref.shape)


def _pos_kernel(t_ref, need_ref, aff_ref, clt_ref, ext_ref, lo_ref, hi_ref, *, NC):
    e = pl.program_id(0)
    bits = pltpu.bitcast(aff_ref[...], I32)
    t = t_ref[e]
    need = need_ref[e].astype(F32)
    upper = jnp.where(lax.broadcasted_iota(I32, (LANES, LANES), 0) <= lax.broadcasted_iota(I32, (LANES, LANES), 1),
                      1.0, 0.0).astype(BF16)
    ones = jnp.ones((LANES, LANES), BF16)
    before = jnp.where(lax.broadcasted_iota(I32, (NC, NC), 1) < lax.broadcasted_iota(I32, (NC, NC), 0),
                       1.0, 0.0).astype(BF16)

    def cums(xf):
        xb = xf.astype(BF16)
        local = jnp.dot(xb, upper, preferred_element_type=F32)
        tot = jnp.dot(xb, ones, preferred_element_type=F32)
        offs = jnp.dot(before, tot.astype(BF16), preferred_element_type=F32)
        return local, tot, offs

    eq = jnp.where(bits == t, 1.0, 0.0)
    local, _, offs = cums(eq)
    sel = (bits > t) | ((bits == t) & (local + offs - eq < need))
    local, tot, offs = cums(jnp.where(sel, 1.0, 0.0))
    lo_ref[...] = offs
    hi_ref[...] = offs + tot
    clt_ref[...] = local.T
    chunk = lax.broadcasted_iota(I32, (NC, LANES), 0)
    lane = lax.broadcasted_iota(I32, (NC, LANES), 1)
    offs_i = offs.astype(I32)
    ext = jnp.where(lane == 0, offs_i >> 8,
                    jnp.where(lane == 1, offs_i & 255,
                              jnp.where(lane == 2, chunk >> 8, jnp.where(lane == 3, chunk & 255, 0))))
    ext_ref[...] = ext.astype(F32).T[:SUBLANES]


def _idx_kernel(clt_ref, ext_ref, lo_ref, hi_ref, idx_ref, *, SB):
    s = (pl.program_id(1) * SB + lax.broadcasted_iota(I32, (1, SB), 1)).astype(F32)
    onehot = jnp.where((lo_ref[:, 0:1] <= s) & (s < hi_ref[:, 0:1]), 1.0, 0.0).astype(BF16)
    cg = jnp.dot(clt_ref[...].astype(BF16), onehot, preferred_element_type=F32)
    ex = jnp.dot(ext_ref[...].astype(BF16), onehot, preferred_element_type=F32)
    slot_lo = ex[0:1] * 256.0 + ex[1:2]
    chunk = ex[2:3] * 256.0 + ex[3:4]
    within = jnp.sum(jnp.where(cg <= s - slot_lo, 1.0, 0.0), axis=0, keepdims=True)
    idx_ref[...] = (chunk * LANES + within).astype(I32)


def _select(aff_t, cap):
    E, N = aff_t.shape
    NC = N // LANES
    aff3 = aff_t.reshape(E, NC, LANES)
    t, need = pl.pallas_call(
        functools.partial(_thr_kernel, cap=cap),
        out_shape=(jax.ShapeDtypeStruct((E, LANES), I32), jax.ShapeDtypeStruct((E, LANES), I32)),
        compiler_params=pltpu.CompilerParams(vmem_limit_bytes=VMEM_LIMIT),
        name="select_threshold",
    )(aff3)
    per_e = lambda e: (e, 0, 0)
    smem = pl.BlockSpec(memory_space=pltpu.SMEM)
    clt, ext, lo, hi = pl.pallas_call(
        functools.partial(_pos_kernel, NC=NC),
        out_shape=(jax.ShapeDtypeStruct((E, LANES, NC), F32), jax.ShapeDtypeStruct((E, SUBLANES, NC), F32),
                   jax.ShapeDtypeStruct((E, NC, LANES), F32), jax.ShapeDtypeStruct((E, NC, LANES), F32)),
        grid=(E,),
        in_specs=[smem, smem, pl.BlockSpec((None, NC, LANES), per_e)],
        out_specs=(pl.BlockSpec((None, LANES, NC), per_e), pl.BlockSpec((None, SUBLANES, NC), per_e),
                   pl.BlockSpec((None, NC, LANES), per_e), pl.BlockSpec((None, NC, LANES), per_e)),
        compiler_params=_cparams(("parallel",)),
        name="select_prefix",
    )(t[:, 0], need[:, 0], aff3)
    SB = min(cap, 1024)
    per_e2 = lambda e, s: (e, 0, 0)
    idx = pl.pallas_call(
        functools.partial(_idx_kernel, SB=SB),
        out_shape=jax.ShapeDtypeStruct((E, 1, cap), I32),
        grid=(E, cap // SB),
        in_specs=[pl.BlockSpec((None, LANES, NC), per_e2), pl.BlockSpec((None, SUBLANES, NC), per_e2),
                  pl.BlockSpec((None, NC, LANES), per_e2), pl.BlockSpec((None, NC, LANES), per_e2)],
        out_specs=pl.BlockSpec((None, 1, SB), lambda e, s: (e, 0, s)),
        compiler_params=_cparams(("parallel", "arbitrary")),
        name="select_index",
    )(clt, ext, lo, hi)
    return idx.reshape(E, cap)


def _gather(x, idx_flat, K):
    N, D = x.shape
    R = idx_flat.shape[0]
    workers = SC_CORES * SC_SUBCORES
    per_w = R // workers
    mesh = plsc.VectorSubcoreMesh(core_axis_name="c", subcore_axis_name="s")

    @functools.partial(
        pl.kernel, mesh=mesh,
        out_type=jax.ShapeDtypeStruct((R, D), x.dtype),
        scratch_types=[pltpu.VMEM((K,), I32), pltpu.VMEM((K, D), x.dtype), pltpu.SemaphoreType.DMA],
        name="gather_rows_sc",
    )
    def run(x_hbm, idx_hbm, o_hbm, idx_v, rows_v, sem):
        base = (lax.axis_index("s") * SC_CORES + lax.axis_index("c")) * per_w

        @pl.loop(0, per_w // K)
        def _(c):
            off = base + c * K
            pltpu.sync_copy(idx_hbm.at[pl.ds(off, K)], idx_v)
            pltpu.async_copy(x_hbm.at[idx_v], rows_v, sem).wait()
            pltpu.sync_copy(rows_v, o_hbm.at[pl.ds(off, K)])

    return run(x, idx_flat)


def _ffn_kernel(x_ref, wg_ref, wu_ref, wd_ref, wr_ref, lo_ref, hi_ref):
    e = pl.program_id(0)
    xb = x_ref[...].astype(BF16)
    aff = _router_aff(xb, wr_ref[...])
    lane = lax.broadcasted_iota(I32, aff.shape, 1)
    val = jnp.sum(jnp.where(lane == e, aff, 0.0), axis=1, keepdims=True)
    gate = jnp.dot(xb, wg_ref[...], preferred_element_type=F32)
    up = jnp.dot(xb, wu_ref[...], preferred_element_type=F32)
    hid = gate * jax.nn.sigmoid(gate) * up
    y = jnp.dot(hid.astype(BF16), wd_ref[...], preferred_element_type=F32) * val
    half = y.shape[1] // 2
    lo_ref[...] = y[:, :half]
    hi_ref[...] = y[:, half:]


def _ffn(xe, wg, wu, wd, wr, cap, tc):
    R, D = xe.shape
    E, _, F = wg.shape
    nb = cap // tc
    half = jax.ShapeDtypeStruct((R, D // 2), F32)
    return pl.pallas_call(
        _ffn_kernel,
        out_shape=(half, half),
        grid=(E, nb),
        in_specs=[pl.BlockSpec((tc, D), lambda e, c: (e * nb + c, 0)),
                  pl.BlockSpec((None, D, F), lambda e, c: (e, 0, 0)),
                  pl.BlockSpec((None, D, F), lambda e, c: (e, 0, 0)),
                  pl.BlockSpec((None, F, D), lambda e, c: (e, 0, 0)),
                  pl.BlockSpec(wr.shape, lambda e, c: (0, 0))],
        out_specs=(pl.BlockSpec((tc, D // 2), lambda e, c: (e * nb + c, 0)),
                   pl.BlockSpec((tc, D // 2), lambda e, c: (e * nb + c, 0))),
        compiler_params=_cparams(("parallel", "arbitrary")),
        name="expert_swiglu",
    )(xe, wg, wu, wd, wr)


def _scatter_add(n_tokens, ye_lo, ye_hi, idx_flat, cap, K):
    R, H = ye_lo.shape
    per_s = cap // SC_SUBCORES
    mesh = plsc.VectorSubcoreMesh(core_axis_name="c", subcore_axis_name="s")
    lanes = SC_LANES

    @functools.partial(
        pl.kernel, mesh=mesh, out_type=(),
        scratch_types=[pltpu.VMEM((K,), I32), pltpu.VMEM((K, H), F32), pltpu.VMEM((K, H), F32),
                       pltpu.SemaphoreType.DMA],
        name="scatter_add_sc",
    )
    def run(ylo_hbm, yhi_hbm, yelo_hbm, yehi_hbm, idx_hbm, idx_v, acc_v, add_v, sem):
        s = lax.axis_index("s")

        def half(y_hbm, ye_hbm):
            @pl.loop(0, N_EXPERTS)
            def _(e):
                @pl.loop(0, per_s // K)
                def _(ch):
                    off = e * cap + s * per_s + ch * K
                    pltpu.sync_copy(idx_hbm.at[pl.ds(off, K)], idx_v)
                    pltpu.sync_copy(ye_hbm.at[pl.ds(off, K)], add_v)
                    pltpu.async_copy(y_hbm.at[idx_v], acc_v, sem).wait()

                    @pl.loop(0, K)
                    def _(r):
                        for j in range(H // lanes):
                            sl = pl.ds(j * lanes, lanes)
                            acc_v[r, sl] = acc_v[r, sl] + add_v[r, sl]

                    pltpu.async_copy(acc_v, y_hbm.at[idx_v], sem).wait()

                plsc.subcore_barrier()

        @pl.when(lax.axis_index("c") == 0)
        def _():
            half(ylo_hbm, yelo_hbm)

        @pl.when(lax.axis_index("c") == 1)
        def _():
            half(yhi_hbm, yehi_hbm)

    y_lo = jax.new_ref(jnp.zeros((n_tokens, H), F32))
    y_hi = jax.new_ref(jnp.zeros((n_tokens, H), F32))
    run(y_lo, y_hi, ye_lo, ye_hi, idx_flat)
    return y_lo[...], y_hi[...]


def _ln2_kernel(x1_ref, ylo_ref, yhi_ref, g_ref, b_ref, x2_ref, xb_ref, *, alpha):
    y = jnp.concatenate([ylo_ref[...], yhi_ref[...]], axis=1)
    x2 = _layer_norm(alpha * x1_ref[...] + y, g_ref[...], b_ref[...])
    x2_ref[...] = x2
    xb_ref[...] = x2.astype(BF16)


def _ln2(x1, y_lo, y_hi, g, b, tm, alpha):
    N, D = x1.shape
    row = lambda i: (i, 0)
    full = lambda i: (0, 0)
    return pl.pallas_call(
        functools.partial(_ln2_kernel, alpha=alpha),
        out_shape=(jax.ShapeDtypeStruct((N, D), F32), jax.ShapeDtypeStruct((N, D), BF16)),
        grid=(N // tm,),
        in_specs=[pl.BlockSpec((tm, D), row), pl.BlockSpec((tm, D // 2), row), pl.BlockSpec((tm, D // 2), row),
                  pl.BlockSpec(g.shape, full), pl.BlockSpec(b.shape, full)],
        out_specs=(pl.BlockSpec((tm, D), row), pl.BlockSpec((tm, D), row)),
        compiler_params=_cparams(("parallel",)),
        name="residual_ln2",
    )(x1, y_lo, y_hi, g, b)


def _tiles(B, S):
    N = B * S
    tm = 1024 if S % 1024 == 0 else S
    return dict(tm_proj=tm, tq_b=min(256, S), tq_c=min(256, S // max(d for _, d in C_CONFIGS)), tm_merge=512,
                tc=min(512, CAPACITY_FACTOR * N // N_EXPERTS))


def _prep_layer(l, w_in, pool_mix, pool_scale, sink_logit, w_branch, w_out, ln1_g, ln1_b, w_router,
                w_gate_e, w_up_e, w_down_e, ln2_g, ln2_b):
    D = w_in.shape[1]
    a0, qb0 = 0, A_WIDTH
    kb0 = qb0 + B_Q_HEADS * B_HEAD_DIM
    vb0 = kb0 + B_KV_HEADS * B_HEAD_DIM
    qc0 = vb0 + B_KV_HEADS * B_HEAD_DIM
    kc0 = qc0 + len(C_CONFIGS) * C_GROUP_W
    vc0 = kc0 + len(C_CONFIGS) * C_GROUP_W
    zg0 = vc0 + len(C_CONFIGS) * C_GROUP_W
    wl = w_in[l]
    wr = jnp.zeros((D, LANES), F32).at[:, :N_EXPERTS].set(w_router[l])
    return dict(
        w_qkb=wl[:, qb0:vb0].astype(BF16),
        w_avb=jnp.concatenate([wl[:, a0:qb0], wl[:, vb0:qc0]], axis=1).astype(BF16),
        w_c=[jnp.concatenate([wl[:, c0 + g * C_GROUP_W:c0 + (g + 1) * C_GROUP_W] for c0 in (qc0, kc0, vc0)],
                             axis=1).astype(BF16) for g in range(len(C_CONFIGS))],
        w_zg=wl[:, zg0:].astype(BF16),
        pmix=pool_mix[l].astype(BF16), pscale=pool_scale[l][None, :], sink=sink_logit[l],
        wbr=w_branch[l].astype(BF16), wout=w_out[l].astype(BF16),
        g1=ln1_g[l][None, :], b1=ln1_b[l][None, :], wr=wr.astype(BF16),
        wg=w_gate_e[l].astype(BF16), wu=w_up_e[l].astype(BF16), wd=w_down_e[l].astype(BF16),
        g2=ln2_g[l][None, :], b2=ln2_b[l][None, :],
    )


def _layer(x, xb, p, B, S, alpha, rot_b, rot_c):
    N, D = x.shape
    t = _tiles(B, S)
    cap = CAPACITY_FACTOR * N // N_EXPERTS
    qkb = _proj(xb, p["w_qkb"], S, t["tm_proj"], p["w_qkb"].shape[1], BF16, rot_b)
    avb = _proj(xb, p["w_avb"], S, t["tm_proj"], p["w_avb"].shape[1], F32)
    zg = _proj(xb, p["w_zg"], S, t["tm_proj"], 1024, BF16)
    o_b = _attn_b(qkb, avb, p["sink"], B, S, t["tq_b"])
    o_cs, lses = [], []
    for g, (_, d) in enumerate(C_CONFIGS):
        qkv = _proj_fold(xb, p["w_c"][g], B, S, d, t["tm_proj"], C_GROUP_W, 2, rot_c)
        o, lse = _attn_c(qkv, g, B, S, t["tq_c"])
        o_cs.append(o)
        lses.append(lse)
    x1, aff_t = _merge(x, avb, o_b, o_cs, lses, zg, p["pmix"], p["pscale"], p["wbr"], p["wout"], p["g1"], p["b1"],
                       p["wr"], S, t["tm_merge"], alpha)
    idx_flat = _select(aff_t, cap).reshape(-1)
    xe = _gather(x1, idx_flat, SC_GATHER_ROWS)
    ye_lo, ye_hi = _ffn(xe, p["wg"], p["wu"], p["wd"], p["wr"], cap, t["tc"])
    y_lo, y_hi = _scatter_add(N, ye_lo, ye_hi, idx_flat, cap, SC_GATHER_ROWS)
    return _ln2(x1, y_lo, y_hi, p["g2"], p["b2"], t["tm_proj"], alpha)


def _trunk(x, layers, alpha):
    B, S, D = x.shape
    rot_b = _rot_tables(S, B_HEAD_DIM)
    rot_c = _rot_tables(S, C_HEAD_DIM)
    xf = x.reshape(B * S, D)
    xb = xf.astype(BF16)
    for p in layers:
        xf, xb = _layer(xf, xb, p, B, S, alpha, rot_b, rot_c)
    return xf.reshape(B, S, D)


def kernel(x_prompt, x_sample, w_in, pool_mix, pool_scale, sink_logit, w_branch, w_out, ln1_g, ln1_b, w_router,
           w_gate_e, w_up_e, w_down_e, ln2_g, ln2_b):
    depth = w_in.shape[0]
    alpha = (2 * depth) ** 0.25
    layers = [_prep_layer(l, w_in, pool_mix, pool_scale, sink_logit, w_branch, w_out, ln1_g, ln1_b, w_router,
                          w_gate_e, w_up_e, w_down_e, ln2_g, ln2_b) for l in range(depth)]
    return (_trunk(x_prompt, layers, alpha), _trunk(x_sample, layers, alpha))
```

```python
import functools

import numpy as np
import jax
import jax.numpy as jnp
from jax import lax
from jax.experimental import pallas as pl
from jax.experimental.pallas import tpu as pltpu
from jax.experimental.pallas import tpu_sc as plsc

F32 = jnp.float32
BF16 = jnp.bfloat16
I32 = jnp.int32

POOL_WINDOWS = (2, 4, 8, 16)
POOL_CH = 128
A_WIDTH = 512
B_Q_HEADS = 8
B_KV_HEADS = 2
B_HEAD_DIM = 64
B_HALF_WINDOW = 128
C_CONFIGS = ((128, 1), (512, 4), (2048, 16))
C_HEADS = 4
C_HEAD_DIM = 128
C_GROUP_W = C_HEADS * C_HEAD_DIM
N_EXPERTS = 16
CAPACITY_FACTOR = 2
ROPE_THETA = 500000.0
ROT_DIV = 4
LN_EPS = 1e-5
NEG_INF = -1e30
LOG2E = 1.4426950408889634
LN2 = 0.6931471805599453

LANES = 128
SUBLANES = 8
VMEM_LIMIT = 56 * 1024 * 1024
SC_CORES = 2
SC_SUBCORES = 16
SC_LANES = 16
SC_GATHER_ROWS = 64
MXU_WIDTH = 256
ATTN_STRIP = 32
ATTN_UNROLL = 4
PERM_BLOCK = MXU_WIDTH


def _cparams(sem):
    return pltpu.CompilerParams(dimension_semantics=sem, vmem_limit_bytes=VMEM_LIMIT)


def _col_chunks(width):
    return [(c, min(c + MXU_WIDTH, width)) for c in range(0, width, MXU_WIDTH)]


def _rotary(acc, c, s1, s2, rot_half):
    outs = []
    for g in range(acc.shape[1] // LANES):
        a = acc[:, g * LANES:(g + 1) * LANES]
        outs.append(a * c + pltpu.roll(a, LANES - rot_half, 1) * s1 + pltpu.roll(a, rot_half, 1) * s2)
    return outs[0] if len(outs) == 1 else jnp.concatenate(outs, axis=1)


def _proj_kernel(x_ref, w_ref, *rest, rot_half):
    if rot_half:
        cos_ref, s1_ref, s2_ref, o_ref = rest
    else:
        (o_ref,) = rest
    x = x_ref[...]
    for c0, c1 in _col_chunks(o_ref.shape[1]):
        acc = jnp.dot(x, w_ref[:, c0:c1], preferred_element_type=F32)
        if rot_half:
            acc = _rotary(acc, cos_ref[...], s1_ref[...], s2_ref[...], rot_half)
        o_ref[:, c0:c1] = acc.astype(o_ref.dtype)


def _proj(x, w, S, tm, tn, out_dtype, rot=None):
    N, K = x.shape
    C = w.shape[1]
    in_specs = [pl.BlockSpec((tm, K), lambda i, j: (i, 0)),
                pl.BlockSpec((K, tn), lambda i, j: (0, j))]
    args = [x, w]
    rot_half = 0
    if rot is not None:
        rot_half, tabs = rot
        spb = S // tm
        in_specs += [pl.BlockSpec((tm, LANES), lambda i, j: (i % spb, 0))] * 3
        args += list(tabs)
    return pl.pallas_call(
        functools.partial(_proj_kernel, rot_half=rot_half),
        out_shape=jax.ShapeDtypeStruct((N, C), out_dtype),
        grid=(N // tm, C // tn),
        in_specs=in_specs,
        out_specs=pl.BlockSpec((tm, tn), lambda i, j: (i, j)),
        compiler_params=_cparams(("parallel", "arbitrary")),
        name="proj",
    )(*args)


def _fold_perm(d, inverse=False):
    i = lax.broadcasted_iota(I32, (PERM_BLOCK, PERM_BLOCK), 1 if inverse else 0)
    j = lax.broadcasted_iota(I32, (PERM_BLOCK, PERM_BLOCK), 0 if inverse else 1)
    per = PERM_BLOCK // d
    return jnp.where(j == (i % per) * d + i // per, 1.0, 0.0).astype(BF16)


def _proj_fold_kernel(x_ref, w_ref, cos_ref, s1_ref, s2_ref, o_ref, scr_ref, *, rot_half, d, n_rot):
    tm, tn = scr_ref.shape
    dst = o_ref.at[0] if d == 1 else scr_ref

    def project(rotate):
        x = x_ref[...]
        for c0, c1 in _col_chunks(tn):
            acc = jnp.dot(x, w_ref[:, c0:c1], preferred_element_type=F32)
            if rotate:
                acc = _rotary(acc, cos_ref[...], s1_ref[...], s2_ref[...], rot_half)
            dst[:, c0:c1] = acc.astype(BF16)

    @pl.when(pl.program_id(1) < n_rot)
    def _():
        project(True)

    @pl.when(pl.program_id(1) >= n_rot)
    def _():
        project(False)

    if d == 1:
        return
    perm = _fold_perm(d)
    per = PERM_BLOCK // d
    for blk in range(tm // PERM_BLOCK):
        y = jnp.dot(perm, scr_ref[blk * PERM_BLOCK:(blk + 1) * PERM_BLOCK, :], preferred_element_type=F32)
        y = y.astype(o_ref.dtype)
        for rr in range(d):
            o_ref[rr, blk * per:(blk + 1) * per, :] = y[rr * per:(rr + 1) * per]


def _proj_fold(x, w, B, S, d, tm, tn, n_rot, rot):
    N, K = x.shape
    C = w.shape[1]
    rot_half, tabs = rot
    spb = S // tm
    tab_spec = pl.BlockSpec((tm, LANES), lambda i, j: (i % spb, 0))
    return pl.pallas_call(
        functools.partial(_proj_fold_kernel, rot_half=rot_half, d=d, n_rot=n_rot),
        out_shape=jax.ShapeDtypeStruct((B * d, S // d, C), BF16),
        grid=(N // tm, C // tn),
        in_specs=[pl.BlockSpec((tm, K), lambda i, j: (i, 0)), pl.BlockSpec((K, tn), lambda i, j: (0, j)),
                  tab_spec, tab_spec, tab_spec],
        out_specs=pl.BlockSpec((d, tm // d, tn), lambda i, j: (i // spb, i % spb, j)),
        scratch_shapes=[pltpu.VMEM((tm, tn), BF16)],
        compiler_params=_cparams(("parallel", "arbitrary")),
        name=f"proj_fold_{d}",
    )(x, w, *tabs)


def _rot_tables(S, head_dim):
    rot = head_dim // ROT_DIV
    half = rot // 2
    inv = 1.0 / (ROPE_THETA ** (jnp.arange(0, rot, 2, dtype=F32) / rot))
    ang = jnp.arange(S, dtype=F32)[:, None] * inv[None, :]
    cos, sin = jnp.cos(ang), jnp.sin(ang)
    d = np.arange(LANES) % head_dim
    first = d < half
    second = (d >= half) & (d < rot)
    src = np.where(first, d, np.where(second, d - half, 0))
    cos_t = jnp.where(first | second, cos[:, src], 1.0)
    s1_t = jnp.where(first, -sin[:, src], 0.0)
    s2_t = jnp.where(second, sin[:, src], 0.0)
    return half, (cos_t, s1_t, s2_t)


def _band_bias(bias_ref, W, q0, L):
    tq, nk = bias_ref.shape
    row = lax.broadcasted_iota(I32, (tq, nk), 0)
    col = lax.broadcasted_iota(I32, (tq, nk), 1)
    kpos = q0 - W + col
    valid = (jnp.abs(col - W - row) <= W) & (kpos >= 0) & (kpos < L)
    bias_ref[...] = jnp.where(valid, 0.0, NEG_INF)


def _softmax_strips(s_ref, p_ref, den_ref, bias_ref, *, unroll, sink_fn=None, max_ref=None):
    rows = s_ref.shape[0]
    tq = bias_ref.shape[0]

    def body(t, carry):
        r0 = pl.multiple_of(t * ATTN_STRIP, ATTN_STRIP)
        s = s_ref[pl.ds(r0, ATTN_STRIP), :] + bias_ref[pl.ds(pl.multiple_of(r0 % tq, ATTN_STRIP), ATTN_STRIP), :]
        m = jnp.max(s, axis=1, keepdims=True)
        if sink_fn is not None:
            sink = sink_fn(r0 // tq) * LOG2E
            m = jnp.maximum(m, sink)
        p = jnp.exp2(s - m)
        den = jnp.sum(p, axis=1, keepdims=True)
        if sink_fn is not None:
            den = den + jnp.exp2(sink - m)
        p_ref[pl.ds(r0, ATTN_STRIP), :] = p.astype(BF16)
        den_ref[pl.ds(r0, ATTN_STRIP), :] = jnp.broadcast_to(den, (ATTN_STRIP, LANES))
        if max_ref is not None:
            max_ref[pl.ds(r0, ATTN_STRIP), :] = jnp.broadcast_to(m, (ATTN_STRIP, LANES))
        return carry

    lax.fori_loop(0, rows // ATTN_STRIP, body, 0, unroll=unroll)


def _attn_b_kernel(sink_ref, q_ref, kp_ref, km_ref, kn_ref, vp_ref, vm_ref, vn_ref, o_ref,
                   s_ref, p_ref, den_ref, bias_ref, *, S, TQ):
    W = B_HALF_WINDOW
    G = B_Q_HEADS // B_KV_HEADS
    q0 = pl.program_id(1) * TQ
    _band_bias(bias_ref, W, q0, S)
    k_all = jnp.concatenate([kp_ref[...], km_ref[...], kn_ref[...]], axis=0)
    v_all = jnp.concatenate([vp_ref[...], vm_ref[...], vn_ref[...]], axis=0).astype(BF16)
    q = q_ref[...]
    outs = []
    for j in range(B_KV_HEADS):
        kj = k_all[:, j * B_HEAD_DIM:(j + 1) * B_HEAD_DIM]
        vj = v_all[:, j * B_HEAD_DIM:(j + 1) * B_HEAD_DIM]
        qs = jnp.concatenate([q[:, (G * j + g) * B_HEAD_DIM:(G * j + g + 1) * B_HEAD_DIM] for g in range(G)], axis=0)
        s_ref[j] = lax.dot_general(qs, kj, (((1,), (1,)), ((), ())),
                                   preferred_element_type=F32) * (B_HEAD_DIM ** -0.5 * LOG2E)
        _softmax_strips(s_ref.at[j], p_ref.at[j], den_ref.at[j], bias_ref, unroll=ATTN_UNROLL,
                        sink_fn=lambda g, j=j: sink_ref[G * j + g])
        o = jnp.dot(p_ref[j], vj, preferred_element_type=F32) / den_ref[j, :, 0:1]
        outs += [o[g * TQ:(g + 1) * TQ] for g in range(G)]
    o_ref[...] = jnp.concatenate(outs, axis=1).astype(o_ref.dtype)


def _attn_b(qk, av, sink, B, S, TQ):
    N = B * S
    W = B_HALF_WINDOW
    nq = S // TQ
    r = TQ // W
    kcol = (B_Q_HEADS * B_HEAD_DIM) // LANES
    last = N // W - 1
    rows = (B_Q_HEADS // B_KV_HEADS) * TQ

    def prev(b, i):
        return (jnp.maximum((b * nq + i) * r - 1, 0), kcol)

    def main(b, i):
        return (b * nq + i, kcol)

    def nxt(b, i):
        return (jnp.minimum((b * nq + i + 1) * r, last), kcol)

    return pl.pallas_call(
        functools.partial(_attn_b_kernel, S=S, TQ=TQ),
        out_shape=jax.ShapeDtypeStruct((N, B_Q_HEADS * B_HEAD_DIM), BF16),
        grid=(B, nq),
        in_specs=[pl.BlockSpec(memory_space=pltpu.SMEM),
                  pl.BlockSpec((TQ, B_Q_HEADS * B_HEAD_DIM), lambda b, i: (b * nq + i, 0)),
                  pl.BlockSpec((W, LANES), prev), pl.BlockSpec((TQ, LANES), main), pl.BlockSpec((W, LANES), nxt),
                  pl.BlockSpec((W, LANES), prev), pl.BlockSpec((TQ, LANES), main), pl.BlockSpec((W, LANES), nxt)],
        out_specs=pl.BlockSpec((TQ, B_Q_HEADS * B_HEAD_DIM), lambda b, i: (b * nq + i, 0)),
        scratch_shapes=[pltpu.VMEM((B_KV_HEADS, rows, TQ + 2 * W), F32),
                        pltpu.VMEM((B_KV_HEADS, rows, TQ + 2 * W), BF16),
                        pltpu.VMEM((B_KV_HEADS, rows, LANES), F32), pltpu.VMEM((TQ, TQ + 2 * W), F32)],
        compiler_params=_cparams(("parallel", "arbitrary")),
        name="attn_window",
    )(sink, qk, qk, qk, qk, av, av, av)


def _attn_c_kernel(q_ref, kp_ref, km_ref, kn_ref, vp_ref, vm_ref, vn_ref, o_ref, lse_ref,
                   s_ref, p_ref, den_ref, max_ref, bias_ref, *, M, TQ, W):
    q0 = pl.program_id(1) * TQ
    _band_bias(bias_ref, W, q0, M)
    k_all = jnp.concatenate([kp_ref[...], km_ref[...], kn_ref[...]], axis=0)
    v_all = jnp.concatenate([vp_ref[...], vm_ref[...], vn_ref[...]], axis=0)
    q = q_ref[...]
    lane = lax.broadcasted_iota(I32, (TQ, LANES), 1)
    lse_tile = jnp.zeros((TQ, LANES), F32)
    outs = []
    for h in range(C_HEADS):
        sl = slice(h * C_HEAD_DIM, (h + 1) * C_HEAD_DIM)
        s_ref[h] = lax.dot_general(q[:, sl], k_all[:, sl], (((1,), (1,)), ((), ())),
                                   preferred_element_type=F32) * (C_HEAD_DIM ** -0.5 * LOG2E)
        _softmax_strips(s_ref.at[h], p_ref.at[h], den_ref.at[h], bias_ref, unroll=True, max_ref=max_ref.at[h])
        den = den_ref[h]
        outs.append(jnp.dot(p_ref[h], v_all[:, sl], preferred_element_type=F32) / den[:, 0:1])
        lse_tile = jnp.where(lane == h, max_ref[h] * LN2 + jnp.log(den), lse_tile)
    o_ref[...] = jnp.concatenate(outs, axis=1).astype(o_ref.dtype)
    lse_ref[...] = lse_tile


def _attn_c(qkv, g, B, S, TQ):
    window, d = C_CONFIGS[g]
    W = window // (2 * d)
    M = S // d
    r = TQ // W
    GW = C_GROUP_W
    last = M // W - 1

    def mk(col):
        return [pl.BlockSpec((None, W, GW), lambda b, i: (b, jnp.maximum(i * r - 1, 0), col)),
                pl.BlockSpec((None, TQ, GW), lambda b, i: (b, i, col)),
                pl.BlockSpec((None, W, GW), lambda b, i: (b, jnp.minimum((i + 1) * r, last), col))]

    return pl.pallas_call(
        functools.partial(_attn_c_kernel, M=M, TQ=TQ, W=W),
        out_shape=(jax.ShapeDtypeStruct((B * d, M, GW), BF16), jax.ShapeDtypeStruct((B * d, M, LANES), F32)),
        grid=(B * d, M // TQ),
        in_specs=[pl.BlockSpec((None, TQ, GW), lambda b, i: (b, i, 0))] + mk(1) + mk(2),
        out_specs=(pl.BlockSpec((None, TQ, GW), lambda b, i: (b, i, 0)),
                   pl.BlockSpec((None, TQ, LANES), lambda b, i: (b, i, 0))),
        scratch_shapes=[pltpu.VMEM((C_HEADS, TQ, TQ + 2 * W), F32), pltpu.VMEM((C_HEADS, TQ, TQ + 2 * W), BF16),
                        pltpu.VMEM((C_HEADS, TQ, LANES), F32), pltpu.VMEM((C_HEADS, TQ, LANES), F32),
                        pltpu.VMEM((TQ, TQ + 2 * W), F32)],
        compiler_params=_cparams(("parallel", "arbitrary")),
        name=f"attn_dilated_{d}",
    )(qkv, qkv, qkv, qkv, qkv, qkv, qkv)


def _layer_norm(h, g, b):
    mu = jnp.mean(h, axis=-1, keepdims=True)
    var = jnp.mean(jnp.square(h - mu), axis=-1, keepdims=True)
    return (h - mu) * lax.rsqrt(var + LN_EPS) * g + b


def _router_aff(xb, wr):
    logits = jnp.dot(xb, wr, preferred_element_type=F32)
    lane = lax.broadcasted_iota(I32, logits.shape, 1)
    logits = jnp.where(lane < N_EXPERTS, logits, NEG_INF)
    e = jnp.exp(logits - jnp.max(logits, axis=1, keepdims=True))
    return e / jnp.sum(e, axis=1, keepdims=True)


def _unfold(ref, scr_ref):
    d, rows, _ = ref.shape
    if d == 1:
        return ref[0]
    for rr in range(d):
        scr_ref[pl.ds(rr, rows, stride=d), :] = ref[rr]
    return scr_ref[...]


def _unfold_bf16(ref):
    d, rows, _ = ref.shape
    if d == 1:
        return ref[0].astype(F32)
    per = PERM_BLOCK // d
    perm = _fold_perm(d, inverse=True)
    outs = []
    for blk in range(d * rows // PERM_BLOCK):
        stacked = jnp.concatenate([ref[rr, blk * per:(blk + 1) * per, :] for rr in range(d)], axis=0)
        outs.append(jnp.dot(perm, stacked, preferred_element_type=F32))
    return outs[0] if len(outs) == 1 else jnp.concatenate(outs, axis=0)


def _merge_kernel(x_ref, a_ref, ap_ref, an_ref, ob_ref, oc0_ref, oc1_ref, oc2_ref, l0_ref, l1_ref, l2_ref,
                  zg_ref, pmix_ref, pscale_ref, wbr_ref, wout_ref, g_ref, b_ref, wr_ref,
                  x1_ref, aff_ref, sl1_ref, sl2_ref, *, S, tm, alpha):
    HW = max(POOL_WINDOWS) // 2
    L = tm + 2 * HW
    pos0 = (pl.program_id(0) * tm) % S
    xa = jnp.concatenate([ap_ref[...], a_ref[...], an_ref[...]], axis=0)
    xpos = pos0 - HW + lax.broadcasted_iota(I32, (L, 1), 0)
    xa = jnp.where((xpos >= 0) & (xpos < S), xa, 0.0)
    sums = {}
    t, w = xa, 1
    while w < max(POOL_WINDOWS):
        t = t + pltpu.roll(t, L - w, 0)
        w *= 2
        sums[w] = t
    pos = pos0 + lax.broadcasted_iota(I32, (tm, 1), 0)
    mixed = []
    for g, w in enumerate(POOL_WINDOWS):
        cs = slice(g * POOL_CH, (g + 1) * POOL_CH)
        off = HW - w // 2
        sw = sums[w][:, cs]
        if off:
            sw = pltpu.roll(sw, L - off, 0)
        sw = sw[:tm]
        cnt = (jnp.clip(pos + w // 2, 0, S) - jnp.clip(pos - w // 2, 0, S)).astype(F32)
        pooled = sw / cnt - a_ref[:, cs]
        mixed.append(jnp.dot(pooled.astype(BF16), pmix_ref[g], preferred_element_type=F32) * pscale_ref[:, cs])
    o_a = jnp.concatenate(mixed, axis=1)
    ls = [_unfold(l0_ref, None), _unfold(l1_ref, sl1_ref), _unfold(l2_ref, sl2_ref)]
    mx = jnp.maximum(jnp.maximum(ls[0], ls[1]), ls[2])
    es = [jnp.exp(l - mx) for l in ls]
    tot = es[0] + es[1] + es[2]
    ocs = [_unfold_bf16(oc0_ref), _unfold_bf16(oc1_ref), _unfold_bf16(oc2_ref)]
    pieces = []
    for h in range(C_HEADS):
        sl = slice(h * C_HEAD_DIM, (h + 1) * C_HEAD_DIM)
        acc = None
        for gi in range(3):
            term = (es[gi] / tot)[:, h:h + 1] * ocs[gi][:, sl].astype(F32)
            acc = term if acc is None else acc + term
        pieces.append(acc)
    o_c = jnp.concatenate(pieces, axis=1)
    branches = [o_a.astype(BF16), ob_ref[...], o_c.astype(BF16)]
    D = x_ref.shape[1]
    merged = None
    for k in range(3):
        gate = jax.nn.sigmoid(zg_ref[:, k * D:(k + 1) * D].astype(F32))
        term = gate * jnp.dot(branches[k], wbr_ref[k], preferred_element_type=F32)
        merged = term if merged is None else merged + term
    mix = jnp.dot(merged.astype(BF16), wout_ref[...], preferred_element_type=F32)
    x1 = _layer_norm(alpha * x_ref[...] + mix, g_ref[...], b_ref[...])
    x1_ref[...] = x1
    aff = _router_aff(x1.astype(BF16), wr_ref[...])
    aff_ref[...] = aff.T[:N_EXPERTS]


def _merge(x, av, o_b, o_cs, lses, vz, pmix, pscale, wbr, wout, g, b, wr, S, tm, alpha):
    N, D = x.shape
    HW = max(POOL_WINDOWS) // 2
    r = tm // HW
    last = N // HW - 1
    row = lambda i: (i, 0)
    full2 = lambda i: (0, 0)
    full3 = lambda i: (0, 0, 0)
    spb = S // tm
    dils = [d for _, d in C_CONFIGS]

    def folded(width):
        return [pl.BlockSpec((d, tm // d, width), lambda i: (i // spb, i % spb, 0)) for d in dils]

    in_specs = [
        pl.BlockSpec((tm, D), row),
        pl.BlockSpec((tm, A_WIDTH), row),
        pl.BlockSpec((HW, A_WIDTH), lambda i: (jnp.maximum(i * r - 1, 0), 0)),
        pl.BlockSpec((HW, A_WIDTH), lambda i: (jnp.minimum((i + 1) * r, last), 0)),
        pl.BlockSpec((tm, B_Q_HEADS * B_HEAD_DIM), row),
        *folded(C_GROUP_W),
        *folded(LANES),
        pl.BlockSpec((tm, 3 * D), row),
        pl.BlockSpec(pmix.shape, full3),
        pl.BlockSpec(pscale.shape, full2),
        pl.BlockSpec(wbr.shape, full3),
        pl.BlockSpec(wout.shape, full2),
        pl.BlockSpec(g.shape, full2), pl.BlockSpec(b.shape, full2),
        pl.BlockSpec(wr.shape, full2),
    ]
    return pl.pallas_call(
        functools.partial(_merge_kernel, S=S, tm=tm, alpha=alpha),
        out_shape=(jax.ShapeDtypeStruct((N, D), F32), jax.ShapeDtypeStruct((N_EXPERTS, N), F32)),
        grid=(N // tm,),
        in_specs=in_specs,
        out_specs=(pl.BlockSpec((tm, D), row), pl.BlockSpec((N_EXPERTS, tm), lambda i: (0, i))),
        scratch_shapes=[pltpu.VMEM((tm, LANES), F32)] * 2,
        compiler_params=_cparams(("parallel",)),
        name="merge_ln1_router",
    )(x, av, av, av, o_b, *o_cs, *lses, vz, pmix, pscale, wbr, wout, g, b, wr)


def _count(mask):
    return jnp.sum(jnp.sum(jnp.where(mask, 1.0, 0.0), axis=1), axis=1, keepdims=True)


def _thr_kernel(aff_ref, t_ref, need_ref, *, cap):
    bits = pltpu.bitcast(aff_ref[...], I32)

    def body(k, t):
        cand = t | jnp.left_shift(jnp.int32(1), 30 - k)
        return jnp.where(_count(bits >= cand[:, :, None]) >= cap, cand, t)

    t = lax.fori_loop(0, 31, body, jnp.zeros((N_EXPERTS, 1), I32))
    need = cap - _count(bits > t[:, :, None])
    t_ref[...] = jnp.broadcast_to(t, t_ref.shape)
    need_ref[...] = jnp.broadcast_to(need.astype(I32), need_ref.shape)


def _pos_kernel(t_ref, need_ref, aff_ref, clt_ref, ext_ref, lo_ref, hi_ref, *, NC):
    e = pl.program_id(0)
    bits = pltpu.bitcast(aff_ref[...], I32)
    t = t_ref[e]
    need = need_ref[e].astype(F32)
    upper = jnp.where(lax.broadcasted_iota(I32, (LANES, LANES), 0) <= lax.broadcasted_iota(I32, (LANES, LANES), 1),
                      1.0, 0.0).astype(BF16)
    ones = jnp.ones((LANES, LANES), BF16)
    before = jnp.where(lax.broadcasted_iota(I32, (NC, NC), 1) < lax.broadcasted_iota(I32, (NC, NC), 0),
                       1.0, 0.0).astype(BF16)

    def cums(xf):
        xb = xf.astype(BF16)
        local = jnp.dot(xb, upper, preferred_element_type=F32)
        tot = jnp.dot(xb, ones, preferred_element_type=F32)
        offs = jnp.dot(before, tot.astype(BF16), preferred_element_type=F32)
        return local, tot, offs

    eq = jnp.where(bits == t, 1.0, 0.0)
    local, _, offs = cums(eq)
    sel = (bits > t) | ((bits == t) & (local + offs - eq < need))
    local, tot, offs = cums(jnp.where(sel, 1.0, 0.0))
    lo_ref[...] = offs
    hi_ref[...] = offs + tot
    clt_ref[...] = local.T
    chunk = lax.broadcasted_iota(I32, (NC, LANES), 0)
    lane = lax.broadcasted_iota(I32, (NC, LANES), 1)
    offs_i = offs.astype(I32)
    ext = jnp.where(lane == 0, offs_i >> 8,
                    jnp.where(lane == 1, offs_i & 255,
                              jnp.where(lane == 2, chunk >> 8, jnp.where(lane == 3, chunk & 255, 0))))
    ext_ref[...] = ext.astype(F32).T[:SUBLANES]


def _idx_kernel(clt_ref, ext_ref, lo_ref, hi_ref, idx_ref, *, SB):
    s = (pl.program_id(1) * SB + lax.broadcasted_iota(I32, (1, SB), 1)).astype(F32)
    onehot = jnp.where((lo_ref[:, 0:1] <= s) & (s < hi_ref[:, 0:1]), 1.0, 0.0).astype(BF16)
    cg = jnp.dot(clt_ref[...].astype(BF16), onehot, preferred_element_type=F32)
    ex = jnp.dot(ext_ref[...].astype(BF16), onehot, preferred_element_type=F32)
    slot_lo = ex[0:1] * 256.0 + ex[1:2]
    chunk = ex[2:3] * 256.0 + ex[3:4]
    within = jnp.sum(jnp.where(cg <= s - slot_lo, 1.0, 0.0), axis=0, keepdims=True)
    idx_ref[...] = (chunk * LANES + within).astype(I32)


def _select(aff_t, cap):
    E, N = aff_t.shape
    NC = N // LANES
    aff3 = aff_t.reshape(E, NC, LANES)
    t, need = pl.pallas_call(
        functools.partial(_thr_kernel, cap=cap),
        out_shape=(jax.ShapeDtypeStruct((E, LANES), I32), jax.ShapeDtypeStruct((E, LANES), I32)),
        compiler_params=pltpu.CompilerParams(vmem_limit_bytes=VMEM_LIMIT),
        name="select_threshold",
    )(aff3)
    per_e = lambda e: (e, 0, 0)
    smem = pl.BlockSpec(memory_space=pltpu.SMEM)
    clt, ext, lo, hi = pl.pallas_call(
        functools.partial(_pos_kernel, NC=NC),
        out_shape=(jax.ShapeDtypeStruct((E, LANES, NC), F32), jax.ShapeDtypeStruct((E, SUBLANES, NC), F32),
                   jax.ShapeDtypeStruct((E, NC, LANES), F32), jax.ShapeDtypeStruct((E, NC, LANES), F32)),
        grid=(E,),
        in_specs=[smem, smem, pl.BlockSpec((None, NC, LANES), per_e)],
        out_specs=(pl.BlockSpec((None, LANES, NC), per_e), pl.BlockSpec((None, SUBLANES, NC), per_e),
                   pl.BlockSpec((None, NC, LANES), per_e), pl.BlockSpec((None, NC, LANES), per_e)),
        compiler_params=_cparams(("parallel",)),
        name="select_prefix",
    )(t[:, 0], need[:, 0], aff3)
    SB = min(cap, 1024)
    per_e2 = lambda e, s: (e, 0, 0)
    idx = pl.pallas_call(
        functools.partial(_idx_kernel, SB=SB),
        out_shape=jax.ShapeDtypeStruct((E, 1, cap), I32),
        grid=(E, cap // SB),
        in_specs=[pl.BlockSpec((None, LANES, NC), per_e2), pl.BlockSpec((None, SUBLANES, NC), per_e2),
                  pl.BlockSpec((None, NC, LANES), per_e2), pl.BlockSpec((None, NC, LANES), per_e2)],
        out_specs=pl.BlockSpec((None, 1, SB), lambda e, s: (e, 0, s)),
        compiler_params=_cparams(("parallel", "arbitrary")),
        name="select_index",
    )(clt, ext, lo, hi)
    return idx.reshape(E, cap)


def _gather(x, idx_flat, K):
    N, D = x.shape
    R = idx_flat.shape[0]
    workers = SC_CORES * SC_SUBCORES
    per_w = R // workers
    mesh = plsc.VectorSubcoreMesh(core_axis_name="c", subcore_axis_name="s")

    @functools.partial(
        pl.kernel, mesh=mesh,
        out_type=jax.ShapeDtypeStruct((R, D), x.dtype),
        scratch_types=[pltpu.VMEM((K,), I32), pltpu.VMEM((K, D), x.dtype), pltpu.SemaphoreType.DMA],
        name="gather_rows_sc",
    )
    def run(x_hbm, idx_hbm, o_hbm, idx_v, rows_v, sem):
        base = (lax.axis_index("s") * SC_CORES + lax.axis_index("c")) * per_w

        @pl.loop(0, per_w // K)
        def _(c):
            off = base + c * K
            pltpu.sync_copy(idx_hbm.at[pl.ds(off, K)], idx_v)
            pltpu.async_copy(x_hbm.at[idx_v], rows_v, sem).wait()
            pltpu.sync_copy(rows_v, o_hbm.at[pl.ds(off, K)])

    return run(x, idx_flat)


def _ffn_kernel(x_ref, wg_ref, wu_ref, wd_ref, wr_ref, lo_ref, hi_ref):
    e = pl.program_id(0)
    xb = x_ref[...].astype(BF16)
    aff = _router_aff(xb, wr_ref[...])
    lane = lax.broadcasted_iota(I32, aff.shape, 1)
    val = jnp.sum(jnp.where(lane == e, aff, 0.0), axis=1, keepdims=True)
    gate = jnp.dot(xb, wg_ref[...], preferred_element_type=F32)
    up = jnp.dot(xb, wu_ref[...], preferred_element_type=F32)
    hid = gate * jax.nn.sigmoid(gate) * up
    y = jnp.dot(hid.astype(BF16), wd_ref[...], preferred_element_type=F32) * val
    half = y.shape[1] // 2
    lo_ref[...] = y[:, :half]
    hi_ref[...] = y[:, half:]


def _ffn(xe, wg, wu, wd, wr, cap, tc):
    R, D = xe.shape
    E, _, F = wg.shape
    nb = cap // tc
    half = jax.ShapeDtypeStruct((R, D // 2), F32)
    return pl.pallas_call(
        _ffn_kernel,
        out_shape=(half, half),
        grid=(E, nb),
        in_specs=[pl.BlockSpec((tc, D), lambda e, c: (e * nb + c, 0)),
                  pl.BlockSpec((None, D, F), lambda e, c: (e, 0, 0)),
                  pl.BlockSpec((None, D, F), lambda e, c: (e, 0, 0)),
                  pl.BlockSpec((None, F, D), lambda e, c: (e, 0, 0)),
                  pl.BlockSpec(wr.shape, lambda e, c: (0, 0))],
        out_specs=(pl.BlockSpec((tc, D // 2), lambda e, c: (e * nb + c, 0)),
                   pl.BlockSpec((tc, D // 2), lambda e, c: (e * nb + c, 0))),
        compiler_params=_cparams(("parallel", "arbitrary")),
        name="expert_swiglu",
    )(xe, wg, wu, wd, wr)


def _scatter_add(n_tokens, ye_lo, ye_hi, idx_flat, cap, K):
    R, H = ye_lo.shape
    per_s = cap // SC_SUBCORES
    mesh = plsc.VectorSubcoreMesh(core_axis_name="c", subcore_axis_name="s")
    lanes = SC_LANES

    @functools.partial(
        pl.kernel, mesh=mesh, out_type=(),
        scratch_types=[pltpu.VMEM((K,), I32), pltpu.VMEM((K, H), F32), pltpu.VMEM((K, H), F32),
                       pltpu.SemaphoreType.DMA],
        name="scatter_add_sc",
    )
    def run(ylo_hbm, yhi_hbm, yelo_hbm, yehi_hbm, idx_hbm, idx_v, acc_v, add_v, sem):
        s = lax.axis_index("s")

        def half(y_hbm, ye_hbm):
            @pl.loop(0, N_EXPERTS)
            def _(e):
                @pl.loop(0, per_s // K)
                def _(ch):
                    off = e * cap + s * per_s + ch * K
                    pltpu.sync_copy(idx_hbm.at[pl.ds(off, K)], idx_v)
                    pltpu.sync_copy(ye_hbm.at[pl.ds(off, K)], add_v)
                    pltpu.async_copy(y_hbm.at[idx_v], acc_v, sem).wait()

                    @pl.loop(0, K)
                    def _(r):
                        for j in range(H // lanes):
                            sl = pl.ds(j * lanes, lanes)
                            acc_v[r, sl] = acc_v[r, sl] + add_v[r, sl]

                    pltpu.async_copy(acc_v, y_hbm.at[idx_v], sem).wait()

                plsc.subcore_barrier()

        @pl.when(lax.axis_index("c") == 0)
        def _():
            half(ylo_hbm, yelo_hbm)

        @pl.when(lax.axis_index("c") == 1)
        def _():
            half(yhi_hbm, yehi_hbm)

    y_lo = jax.new_ref(jnp.zeros((n_tokens, H), F32))
    y_hi = jax.new_ref(jnp.zeros((n_tokens, H), F32))
    run(y_lo, y_hi, ye_lo, ye_hi, idx_flat)
    return y_lo[...], y_hi[...]


def _ln2_kernel(x1_ref, ylo_ref, yhi_ref, g_ref, b_ref, x2_ref, xb_ref, *, alpha):
    y = jnp.concatenate([ylo_ref[...], yhi_ref[...]], axis=1)
    x2 = _layer_norm(alpha * x1_ref[...] + y, g_ref[...], b_ref[...])
    x2_ref[...] = x2
    xb_ref[...] = x2.astype(BF16)


def _ln2(x1, y_lo, y_hi, g, b, tm, alpha):
    N, D = x1.shape
    row = lambda i: (i, 0)
    full = lambda i: (0, 0)
    return pl.pallas_call(
        functools.partial(_ln2_kernel, alpha=alpha),
        out_shape=(jax.ShapeDtypeStruct((N, D), F32), jax.ShapeDtypeStruct((N, D), BF16)),
        grid=(N // tm,),
        in_specs=[pl.BlockSpec((tm, D), row), pl.BlockSpec((tm, D // 2), row), pl.BlockSpec((tm, D // 2), row),
                  pl.BlockSpec(g.shape, full), pl.BlockSpec(b.shape, full)],
        out_specs=(pl.BlockSpec((tm, D), row), pl.BlockSpec((tm, D), row)),
        compiler_params=_cparams(("parallel",)),
        name="residual_ln2",
    )(x1, y_lo, y_hi, g, b)


def _tiles(B, S):
    N = B * S
    tm = 1024 if S % 1024 == 0 else S
    return dict(tm_proj=tm, tq_b=min(256, S), tq_c=min(256, S // max(d for _, d in C_CONFIGS)), tm_merge=512,
                tc=min(512, CAPACITY_FACTOR * N // N_EXPERTS))


def _prep_layer(l, w_in, pool_mix, pool_scale, sink_logit, w_branch, w_out, ln1_g, ln1_b, w_router,
                w_gate_e, w_up_e, w_down_e, ln2_g, ln2_b):
    D = w_in.shape[1]
    a0, qb0 = 0, A_WIDTH
    kb0 = qb0 + B_Q_HEADS * B_HEAD_DIM
    vb0 = kb0 + B_KV_HEADS * B_HEAD_DIM
    qc0 = vb0 + B_KV_HEADS * B_HEAD_DIM
    kc0 = qc0 + len(C_CONFIGS) * C_GROUP_W
    vc0 = kc0 + len(C_CONFIGS) * C_GROUP_W
    zg0 = vc0 + len(C_CONFIGS) * C_GROUP_W
    wl = w_in[l]
    wr = jnp.zeros((D, LANES), F32).at[:, :N_EXPERTS].set(w_router[l])
    return dict(
        w_qkb=wl[:, qb0:vb0].astype(BF16),
        w_avb=jnp.concatenate([wl[:, a0:qb0], wl[:, vb0:qc0]], axis=1).astype(BF16),
        w_c=[jnp.concatenate([wl[:, c0 + g * C_GROUP_W:c0 + (g + 1) * C_GROUP_W] for c0 in (qc0, kc0, vc0)],
                             axis=1).astype(BF16) for g in range(len(C_CONFIGS))],
        w_zg=wl[:, zg0:].astype(BF16),
        pmix=pool_mix[l].astype(BF16), pscale=pool_scale[l][None, :], sink=sink_logit[l],
        wbr=w_branch[l].astype(BF16), wout=w_out[l].astype(BF16),
        g1=ln1_g[l][None, :], b1=ln1_b[l][None, :], wr=wr.astype(BF16),
        wg=w_gate_e[l].astype(BF16), wu=w_up_e[l].astype(BF16), wd=w_down_e[l].astype(BF16),
        g2=ln2_g[l][None, :], b2=ln2_b[l][None, :],
    )


def _layer(x, xb, p, B, S, alpha, rot_b, rot_c):
    N, D = x.shape
    t = _tiles(B, S)
    cap = CAPACITY_FACTOR * N // N_EXPERTS
    qkb = _proj(xb, p["w_qkb"], S, t["tm_proj"], p["w_qkb"].shape[1], BF16, rot_b)
    avb = _proj(xb, p["w_avb"], S, t["tm_proj"], p["w_avb"].shape[1], F32)
    zg = _proj(xb, p["w_zg"], S, t["tm_proj"], 1024, BF16)
    o_b = _attn_b(qkb, avb, p["sink"], B, S, t["tq_b"])
    o_cs, lses = [], []
    for g, (_, d) in enumerate(C_CONFIGS):
        qkv = _proj_fold(xb, p["w_c"][g], B, S, d, t["tm_proj"], C_GROUP_W, 2, rot_c)
        o, lse = _attn_c(qkv, g, B, S, t["tq_c"])
        o_cs.append(o)
        lses.append(lse)
    x1, aff_t = _merge(x, avb, o_b, o_cs, lses, zg, p["pmix"], p["pscale"], p["wbr"], p["wout"], p["g1"], p["b1"],
                       p["wr"], S, t["tm_merge"], alpha)
    idx_flat = _select(aff_t, cap).reshape(-1)
    xe = _gather(x1, idx_flat, SC_GATHER_ROWS)
    ye_lo, ye_hi = _ffn(xe, p["wg"], p["wu"], p["wd"], p["wr"], cap, t["tc"])
    y_lo, y_hi = _scatter_add(N, ye_lo, ye_hi, idx_flat, cap, SC_GATHER_ROWS)
    return _ln2(x1, y_lo, y_hi, p["g2"], p["b2"], t["tm_proj"], alpha)


def _trunk(x, layers, alpha):
    B, S, D = x.shape
    rot_b = _rot_tables(S, B_HEAD_DIM)
    rot_c = _rot_tables(S, C_HEAD_DIM)
    xf = x.reshape(B * S, D)
    xb = xf.astype(BF16)
    for p in layers:
        xf, xb = _layer(xf, xb, p, B, S, alpha, rot_b, rot_c)
    return xf.reshape(B, S, D)


def kernel(x_prompt, x_sample, w_in, pool_mix, pool_scale, sink_logit, w_branch, w_out, ln1_g, ln1_b, w_router,
           w_gate_e, w_up_e, w_down_e, ln2_g, ln2_b):
    depth = w_in.shape[0]
    alpha = (2 * depth) ** 0.25
    layers = [_prep_layer(l, w_in, pool_mix, pool_scale, sink_logit, w_branch, w_out, ln1_g, ln1_b, w_router,
                          w_gate_e, w_up_e, w_down_e, ln2_g, ln2_b) for l in range(depth)]
    return (_trunk(x_prompt, layers, alpha), _trunk(x_sample, layers, alpha))
```

```python
import functools

import numpy as np
import jax
import jax.numpy as jnp
from jax import lax
from jax.experimental import pallas as pl
from jax.experimental.pallas import tpu as pltpu
from jax.experimental.pallas import tpu_sc as plsc

F32 = jnp.float32
BF16 = jnp.bfloat16
I32 = jnp.int32

POOL_WINDOWS = (2, 4, 8, 16)
POOL_CH = 128
A_WIDTH = 512
B_Q_HEADS = 8
B_KV_HEADS = 2
B_HEAD_DIM = 64
B_HALF_WINDOW = 128
C_CONFIGS = ((128, 1), (512, 4), (2048, 16))
C_HEADS = 4
C_HEAD_DIM = 128
C_GROUP_W = C_HEADS * C_HEAD_DIM
N_EXPERTS = 16
CAPACITY_FACTOR = 2
ROPE_THETA = 500000.0
ROT_DIV = 4
LN_EPS = 1e-5
NEG_INF = -1e30
LOG2E = 1.4426950408889634
LN2 = 0.6931471805599453

LANES = 128
SUBLANES = 8
VMEM_LIMIT = 56 * 1024 * 1024
SC_CORES = 2
SC_SUBCORES = 16
SC_LANES = 16
SC_GATHER_ROWS = 64
MXU_WIDTH = 256
ATTN_STRIP = 32
ATTN_UNROLL = 8
PERM_BLOCK = MXU_WIDTH


def _cparams(sem):
    return pltpu.CompilerParams(dimension_semantics=sem, vmem_limit_bytes=VMEM_LIMIT)


def _col_chunks(width):
    return [(c, min(c + MXU_WIDTH, width)) for c in range(0, width, MXU_WIDTH)]


def _rotary(acc, c, s1, s2, rot_half):
    outs = []
    for g in range(acc.shape[1] // LANES):
        a = acc[:, g * LANES:(g + 1) * LANES]
        if 2 * rot_half == LANES:
            outs.append(a * c + pltpu.roll(a, rot_half, 1) * s1)
        else:
            outs.append(a * c + pltpu.roll(a, LANES - rot_half, 1) * s1 + pltpu.roll(a, rot_half, 1) * s2)
    return outs[0] if len(outs) == 1 else jnp.concatenate(outs, axis=1)


def _proj_kernel(x_ref, w_ref, *rest, rot_half):
    if rot_half:
        cos_ref, s1_ref, s2_ref, o_ref = rest
    else:
        (o_ref,) = rest
    x = x_ref[...]
    for c0, c1 in _col_chunks(o_ref.shape[1]):
        acc = jnp.dot(x, w_ref[:, c0:c1], preferred_element_type=F32)
        if rot_half:
            acc = _rotary(acc, cos_ref[...], s1_ref[...], s2_ref[...], rot_half)
        o_ref[:, c0:c1] = acc.astype(o_ref.dtype)


def _proj(x, w, S, tm, tn, out_dtype, rot=None):
    N, K = x.shape
    C = w.shape[1]
    in_specs = [pl.BlockSpec((tm, K), lambda i, j: (i, 0)),
                pl.BlockSpec((K, tn), lambda i, j: (0, j))]
    args = [x, w]
    rot_half = 0
    if rot is not None:
        rot_half, tabs = rot
        spb = S // tm
        in_specs += [pl.BlockSpec((tm, LANES), lambda i, j: (i % spb, 0))] * 3
        args += list(tabs)
    return pl.pallas_call(
        functools.partial(_proj_kernel, rot_half=rot_half),
        out_shape=jax.ShapeDtypeStruct((N, C), out_dtype),
        grid=(N // tm, C // tn),
        in_specs=in_specs,
        out_specs=pl.BlockSpec((tm, tn), lambda i, j: (i, j)),
        compiler_params=_cparams(("parallel", "arbitrary")),
        name="proj",
    )(*args)


def _fold_perm(d, inverse=False):
    i = lax.broadcasted_iota(I32, (PERM_BLOCK, PERM_BLOCK), 1 if inverse else 0)
    j = lax.broadcasted_iota(I32, (PERM_BLOCK, PERM_BLOCK), 0 if inverse else 1)
    per = PERM_BLOCK // d
    return jnp.where(j == (i % per) * d + i // per, 1.0, 0.0).astype(BF16)


def _proj_fold_kernel(x_ref, w_ref, cos_ref, s1_ref, s2_ref, o_ref, *, rot_half, d, rot_cols):
    x = x_ref[...]
    tm = x.shape[0]
    per = PERM_BLOCK // d
    perm = _fold_perm(d) if d > 1 else None
    for c0, c1 in _col_chunks(w_ref.shape[1]):
        acc = jnp.dot(x, w_ref[:, c0:c1], preferred_element_type=F32)
        if c0 < rot_cols:
            acc = _rotary(acc, cos_ref[...], s1_ref[...], s2_ref[...], rot_half)
        y = acc.astype(BF16)
        if d == 1:
            o_ref[0, :, c0:c1] = y
            continue
        for blk in range(tm // PERM_BLOCK):
            z = jnp.dot(perm, y[blk * PERM_BLOCK:(blk + 1) * PERM_BLOCK], preferred_element_type=F32).astype(BF16)
            for rr in range(d):
                o_ref[rr, blk * per:(blk + 1) * per, c0:c1] = z[rr * per:(rr + 1) * per]


def _proj_fold(x, w, B, S, d, tm, rot_cols, rot):
    N, K = x.shape
    C = w.shape[1]
    rot_half, tabs = rot
    spb = S // tm
    tab_spec = pl.BlockSpec((tm, LANES), lambda i: (i % spb, 0))
    return pl.pallas_call(
        functools.partial(_proj_fold_kernel, rot_half=rot_half, d=d, rot_cols=rot_cols),
        out_shape=jax.ShapeDtypeStruct((B * d, S // d, C), BF16),
        grid=(N // tm,),
        in_specs=[pl.BlockSpec((tm, K), lambda i: (i, 0)), pl.BlockSpec((K, C), lambda i: (0, 0)),
                  tab_spec, tab_spec, tab_spec],
        out_specs=pl.BlockSpec((d, tm // d, C), lambda i: (i // spb, i % spb, 0)),
        compiler_params=_cparams(("parallel",)),
        name=f"proj_fold_{d}",
    )(x, w, *tabs)


def _rot_tables(S, head_dim):
    rot = head_dim // ROT_DIV
    half = rot // 2
    inv = 1.0 / (ROPE_THETA ** (jnp.arange(0, rot, 2, dtype=F32) / rot))
    ang = jnp.arange(S, dtype=F32)[:, None] * inv[None, :]
    cos, sin = jnp.cos(ang), jnp.sin(ang)
    d = np.arange(LANES) % head_dim
    first = d < half
    second = (d >= half) & (d < rot)
    src = np.where(first, d, np.where(second, d - half, 0))
    cos_t = jnp.where(first | second, cos[:, src], 1.0)
    s1_t = jnp.where(first, -sin[:, src], 0.0)
    s2_t = jnp.where(second, sin[:, src], 0.0)
    return half, (cos_t, s1_t, s2_t)


def _spread_head_perm(head_dim):
    half = head_dim // ROT_DIV // 2
    mid = LANES // 2
    return np.concatenate([np.arange(0, half), np.arange(2 * half, mid + half), np.arange(half, 2 * half),
                           np.arange(mid + half, head_dim)])


def _rot_tables_spread(S, head_dim):
    half, (cos_t, s1_t, s2_t) = _rot_tables(S, head_dim)
    perm = _spread_head_perm(head_dim)
    return LANES // 2, (cos_t[:, perm], (s1_t + s2_t)[:, perm], s2_t)


def _band_bias(bias_ref, W, q0, L, sub):
    tq, nk = bias_ref.shape
    row = lax.broadcasted_iota(I32, (tq, nk), 0)
    col = lax.broadcasted_iota(I32, (tq, nk), 1)
    first = (row // sub) * sub
    kpos = q0 - W + first + col
    valid = (jnp.abs(col - W - (row - first)) <= W) & (kpos >= 0) & (kpos < L)
    bias_ref[...] = jnp.where(valid, 0.0, NEG_INF)


def _softmax_strips(s_ref, p_ref, den_ref, max_ref, bias_ref, *, unroll, sink_fn=None):
    rows, nk = s_ref.shape
    tq = bias_ref.shape[0]

    def strip(t):
        r0 = pl.multiple_of(t * ATTN_STRIP, ATTN_STRIP)
        rows_sl = pl.ds(r0, ATTN_STRIP)
        s = s_ref[rows_sl, :] + bias_ref[pl.ds(pl.multiple_of(r0 % tq, ATTN_STRIP), ATTN_STRIP), :]
        return r0, rows_sl, s

    def row_max(t, carry):
        r0, rows_sl, s = strip(t)
        m = jnp.max(s, axis=1, keepdims=True)
        if sink_fn is not None:
            m = jnp.maximum(m, sink_fn(r0 // tq) * LOG2E)
        max_ref[rows_sl, :] = jnp.broadcast_to(m, (ATTN_STRIP, LANES))
        return carry

    def numerators(t, carry):
        r0, rows_sl, s = strip(t)
        m = max_ref[rows_sl, :]
        p = jnp.exp2(s - jnp.concatenate([m] * (nk // LANES), axis=1))
        den = jnp.broadcast_to(jnp.sum(p, axis=1, keepdims=True), (ATTN_STRIP, LANES))
        if sink_fn is not None:
            den = den + jnp.exp2(sink_fn(r0 // tq) * LOG2E - m)
        p_ref[rows_sl, :] = p.astype(BF16)
        den_ref[rows_sl, :] = den
        return carry

    lax.fori_loop(0, rows // ATTN_STRIP, row_max, 0, unroll=unroll)
    lax.fori_loop(0, rows // ATTN_STRIP, numerators, 0, unroll=unroll)


def _attn_b_kernel(sink_ref, q_ref, kp_ref, km_ref, kn_ref, vp_ref, vm_ref, vn_ref, o_ref,
                   s_ref, p_ref, den_ref, max_ref, bias_ref, *, S, TQ):
    W = B_HALF_WINDOW
    G = B_Q_HEADS // B_KV_HEADS
    q0 = pl.program_id(1) * TQ
    _band_bias(bias_ref, W, q0, S, TQ)
    k_all = jnp.concatenate([kp_ref[...], km_ref[...], kn_ref[...]], axis=0)
    v_all = jnp.concatenate([vp_ref[...], vm_ref[...], vn_ref[...]], axis=0).astype(BF16)
    q = q_ref[...]
    outs = []
    for j in range(B_KV_HEADS):
        kj = k_all[:, j * B_HEAD_DIM:(j + 1) * B_HEAD_DIM]
        vj = v_all[:, j * B_HEAD_DIM:(j + 1) * B_HEAD_DIM]
        qs = jnp.concatenate([q[:, (G * j + g) * B_HEAD_DIM:(G * j + g + 1) * B_HEAD_DIM] for g in range(G)], axis=0)
        s_ref[j] = lax.dot_general(qs, kj, (((1,), (1,)), ((), ())),
                                   preferred_element_type=F32) * (B_HEAD_DIM ** -0.5 * LOG2E)
        _softmax_strips(s_ref.at[j], p_ref.at[j], den_ref.at[j], max_ref.at[j], bias_ref, unroll=ATTN_UNROLL,
                        sink_fn=lambda g, j=j: sink_ref[G * j + g])
        o = jnp.dot(p_ref[j], vj, preferred_element_type=F32) / den_ref[j, :, 0:1]
        outs += [o[g * TQ:(g + 1) * TQ] for g in range(G)]
    o_ref[...] = jnp.concatenate(outs, axis=1).astype(o_ref.dtype)


def _attn_b(qk, av, sink, B, S, TQ):
    N = B * S
    W = B_HALF_WINDOW
    nq = S // TQ
    r = TQ // W
    kcol = (B_Q_HEADS * B_HEAD_DIM) // LANES
    last = N // W - 1
    rows = (B_Q_HEADS // B_KV_HEADS) * TQ

    def prev(b, i):
        return (jnp.maximum((b * nq + i) * r - 1, 0), kcol)

    def main(b, i):
        return (b * nq + i, kcol)

    def nxt(b, i):
        return (jnp.minimum((b * nq + i + 1) * r, last), kcol)

    return pl.pallas_call(
        functools.partial(_attn_b_kernel, S=S, TQ=TQ),
        out_shape=jax.ShapeDtypeStruct((N, B_Q_HEADS * B_HEAD_DIM), BF16),
        grid=(B, nq),
        in_specs=[pl.BlockSpec(memory_space=pltpu.SMEM),
                  pl.BlockSpec((TQ, B_Q_HEADS * B_HEAD_DIM), lambda b, i: (b * nq + i, 0)),
                  pl.BlockSpec((W, LANES), prev), pl.BlockSpec((TQ, LANES), main), pl.BlockSpec((W, LANES), nxt),
                  pl.BlockSpec((W, LANES), prev), pl.BlockSpec((TQ, LANES), main), pl.BlockSpec((W, LANES), nxt)],
        out_specs=pl.BlockSpec((TQ, B_Q_HEADS * B_HEAD_DIM), lambda b, i: (b * nq + i, 0)),
        scratch_shapes=[pltpu.VMEM((B_KV_HEADS, rows, TQ + 2 * W), F32),
                        pltpu.VMEM((B_KV_HEADS, rows, TQ + 2 * W), BF16),
                        pltpu.VMEM((B_KV_HEADS, rows, LANES), F32), pltpu.VMEM((B_KV_HEADS, rows, LANES), F32),
                        pltpu.VMEM((TQ, TQ + 2 * W), F32)],
        compiler_params=_cparams(("parallel", "arbitrary")),
        name="attn_window",
    )(sink, qk, qk, qk, qk, av, av, av)


def _attn_c_kernel(q_ref, kp_ref, km_ref, kn_ref, vp_ref, vm_ref, vn_ref, o_ref, lse_ref,
                   s_ref, p_ref, den_ref, max_ref, bias_ref, *, M, TQ, W):
    sub, span = 2 * W, 4 * W
    q0 = pl.program_id(1) * TQ
    _band_bias(bias_ref, W, q0, M, sub)
    k_all = jnp.concatenate([kp_ref[...], km_ref[...], kn_ref[...]], axis=0)
    v_all = jnp.concatenate([vp_ref[...], vm_ref[...], vn_ref[...]], axis=0)
    q = q_ref[...]
    lane = lax.broadcasted_iota(I32, (TQ, LANES), 1)
    lse_tile = jnp.zeros((TQ, LANES), F32)
    outs = []
    for h in range(C_HEADS):
        sl = slice(h * C_HEAD_DIM, (h + 1) * C_HEAD_DIM)
        for b in range(TQ // sub):
            s_ref[h, b * sub:(b + 1) * sub, :] = lax.dot_general(
                q[b * sub:(b + 1) * sub, sl], k_all[b * sub:b * sub + span, sl], (((1,), (1,)), ((), ())),
                preferred_element_type=F32) * (C_HEAD_DIM ** -0.5 * LOG2E)
        _softmax_strips(s_ref.at[h], p_ref.at[h], den_ref.at[h], max_ref.at[h], bias_ref, unroll=True)
        den = den_ref[h]
        pv = [jnp.dot(p_ref[h, b * sub:(b + 1) * sub, :], v_all[b * sub:b * sub + span, sl],
                      preferred_element_type=F32) for b in range(TQ // sub)]
        outs.append(jnp.concatenate(pv, axis=0) / den[:, 0:1])
        lse_tile = jnp.where(lane == h, max_ref[h] * LN2 + jnp.log(den), lse_tile)
    o_ref[...] = jnp.concatenate(outs, axis=1).astype(o_ref.dtype)
    lse_ref[...] = lse_tile


def _attn_c(qkv, g, B, S, TQ):
    window, d = C_CONFIGS[g]
    W = window // (2 * d)
    M = S // d
    TQ = min(TQ, M)
    r = TQ // W
    GW = C_GROUP_W
    last = M // W - 1

    def mk(col):
        return [pl.BlockSpec((None, W, GW), lambda b, i: (b, jnp.maximum(i * r - 1, 0), col)),
                pl.BlockSpec((None, TQ, GW), lambda b, i: (b, i, col)),
                pl.BlockSpec((None, W, GW), lambda b, i: (b, jnp.minimum((i + 1) * r, last), col))]

    return pl.pallas_call(
        functools.partial(_attn_c_kernel, M=M, TQ=TQ, W=W),
        out_shape=(jax.ShapeDtypeStruct((B * d, M, GW), BF16), jax.ShapeDtypeStruct((B * d, M, LANES), F32)),
        grid=(B * d, M // TQ),
        in_specs=[pl.BlockSpec((None, TQ, GW), lambda b, i: (b, i, 0))] + mk(1) + mk(2),
        out_specs=(pl.BlockSpec((None, TQ, GW), lambda b, i: (b, i, 0)),
                   pl.BlockSpec((None, TQ, LANES), lambda b, i: (b, i, 0))),
        scratch_shapes=[pltpu.VMEM((C_HEADS, TQ, 4 * W), F32), pltpu.VMEM((C_HEADS, TQ, 4 * W), BF16),
                        pltpu.VMEM((C_HEADS, TQ, LANES), F32), pltpu.VMEM((C_HEADS, TQ, LANES), F32),
                        pltpu.VMEM((TQ, 4 * W), F32)],
        compiler_params=_cparams(("parallel", "arbitrary")),
        name=f"attn_dilated_{d}",
    )(qkv, qkv, qkv, qkv, qkv, qkv, qkv)


def _layer_norm(h, g, b):
    mu = jnp.mean(h, axis=-1, keepdims=True)
    var = jnp.mean(jnp.square(h - mu), axis=-1, keepdims=True)
    return (h - mu) * lax.rsqrt(var + LN_EPS) * g + b


def _router_aff(xb, wr):
    logits = jnp.dot(xb, wr, preferred_element_type=F32)
    lane = lax.broadcasted_iota(I32, logits.shape, 1)
    logits = jnp.where(lane < N_EXPERTS, logits, NEG_INF)
    e = jnp.exp(logits - jnp.max(logits, axis=1, keepdims=True))
    return e / jnp.sum(e, axis=1, keepdims=True)


def _unfold(ref, scr_ref):
    d, rows, _ = ref.shape
    if d == 1:
        return ref[0]
    for rr in range(d):
        scr_ref[pl.ds(rr, rows, stride=d), :] = ref[rr]
    return scr_ref[...]


def _unfold_bf16(ref):
    d, rows, _ = ref.shape
    if d == 1:
        return ref[0].astype(F32)
    per = PERM_BLOCK // d
    perm = _fold_perm(d, inverse=True)
    outs = []
    for blk in range(d * rows // PERM_BLOCK):
        stacked = jnp.concatenate([ref[rr, blk * per:(blk + 1) * per, :] for rr in range(d)], axis=0)
        outs.append(jnp.dot(perm, stacked, preferred_element_type=F32))
    return outs[0] if len(outs) == 1 else jnp.concatenate(outs, axis=0)


def _merge_kernel(x_ref, a_ref, ap_ref, an_ref, ob_ref, oc0_ref, oc1_ref, oc2_ref, l0_ref, l1_ref, l2_ref,
                  zg_ref, pmix_ref, pscale_ref, wbr_ref, wout_ref, g_ref, b_ref, wr_ref,
                  x1_ref, aff_ref, sl1_ref, sl2_ref, *, S, tm, alpha):
    HW = max(POOL_WINDOWS) // 2
    L = tm + 2 * HW
    pos0 = (pl.program_id(0) * tm) % S
    xa = jnp.concatenate([ap_ref[...], a_ref[...], an_ref[...]], axis=0)
    xpos = pos0 - HW + lax.broadcasted_iota(I32, (L, 1), 0)
    xa = jnp.where((xpos >= 0) & (xpos < S), xa, 0.0)
    sums = {}
    t, w = xa, 1
    while w < max(POOL_WINDOWS):
        t = t + pltpu.roll(t, L - w, 0)
        w *= 2
        sums[w] = t
    pos = pos0 + lax.broadcasted_iota(I32, (tm, 1), 0)
    mixed = []
    for g, w in enumerate(POOL_WINDOWS):
        cs = slice(g * POOL_CH, (g + 1) * POOL_CH)
        off = HW - w // 2
        sw = sums[w][:, cs]
        if off:
            sw = pltpu.roll(sw, L - off, 0)
        sw = sw[:tm]
        cnt = (jnp.clip(pos + w // 2, 0, S) - jnp.clip(pos - w // 2, 0, S)).astype(F32)
        pooled = sw / cnt - a_ref[:, cs]
        mixed.append(jnp.dot(pooled.astype(BF16), pmix_ref[g], preferred_element_type=F32) * pscale_ref[:, cs])
    o_a = jnp.concatenate(mixed, axis=1)
    ls = [_unfold(l0_ref, None), _unfold(l1_ref, sl1_ref), _unfold(l2_ref, sl2_ref)]
    mx = jnp.maximum(jnp.maximum(ls[0], ls[1]), ls[2])
    es = [jnp.exp(l - mx) for l in ls]
    tot = es[0] + es[1] + es[2]
    ocs = [_unfold_bf16(oc0_ref), _unfold_bf16(oc1_ref), _unfold_bf16(oc2_ref)]
    pieces = []
    for h in range(C_HEADS):
        sl = slice(h * C_HEAD_DIM, (h + 1) * C_HEAD_DIM)
        acc = None
        for gi in range(3):
            term = (es[gi] / tot)[:, h:h + 1] * ocs[gi][:, sl].astype(F32)
            acc = term if acc is None else acc + term
        pieces.append(acc)
    o_c = jnp.concatenate(pieces, axis=1)
    branches = [o_a.astype(BF16), ob_ref[...], o_c.astype(BF16)]
    D = x_ref.shape[1]
    merged = None
    for k in range(3):
        gate = jax.nn.sigmoid(zg_ref[:, k * D:(k + 1) * D].astype(F32))
        term = gate * jnp.dot(branches[k], wbr_ref[k], preferred_element_type=F32)
        merged = term if merged is None else merged + term
    mix = jnp.dot(merged.astype(BF16), wout_ref[...], preferred_element_type=F32)
    x1 = _layer_norm(alpha * x_ref[...] + mix, g_ref[...], b_ref[...])
    x1_ref[...] = x1
    aff = _router_aff(x1.astype(BF16), wr_ref[...])
    aff_ref[...] = aff.T[:N_EXPERTS]


def _merge(x, av, o_b, o_cs, lses, vz, pmix, pscale, wbr, wout, g, b, wr, S, tm, alpha):
    N, D = x.shape
    HW = max(POOL_WINDOWS) // 2
    r = tm // HW
    last = N // HW - 1
    row = lambda i: (i, 0)
    full2 = lambda i: (0, 0)
    full3 = lambda i: (0, 0, 0)
    spb = S // tm
    dils = [d for _, d in C_CONFIGS]

    def folded(width):
        return [pl.BlockSpec((d, tm // d, width), lambda i: (i // spb, i % spb, 0)) for d in dils]

    in_specs = [
        pl.BlockSpec((tm, D), row),
        pl.BlockSpec((tm, A_WIDTH), row),
        pl.BlockSpec((HW, A_WIDTH), lambda i: (jnp.maximum(i * r - 1, 0), 0)),
        pl.BlockSpec((HW, A_WIDTH), lambda i: (jnp.minimum((i + 1) * r, last), 0)),
        pl.BlockSpec((tm, B_Q_HEADS * B_HEAD_DIM), row),
        *folded(C_GROUP_W),
        *folded(LANES),
        pl.BlockSpec((tm, 3 * D), row),
        pl.BlockSpec(pmix.shape, full3),
        pl.BlockSpec(pscale.shape, full2),
        pl.BlockSpec(wbr.shape, full3),
        pl.BlockSpec(wout.shape, full2),
        pl.BlockSpec(g.shape, full2), pl.BlockSpec(b.shape, full2),
        pl.BlockSpec(wr.shape, full2),
    ]
    return pl.pallas_call(
        functools.partial(_merge_kernel, S=S, tm=tm, alpha=alpha),
        out_shape=(jax.ShapeDtypeStruct((N, D), F32), jax.ShapeDtypeStruct((N_EXPERTS, N), F32)),
        grid=(N // tm,),
        in_specs=in_specs,
        out_specs=(pl.BlockSpec((tm, D), row), pl.BlockSpec((N_EXPERTS, tm), lambda i: (0, i))),
        scratch_shapes=[pltpu.VMEM((tm, LANES), F32)] * 2,
        compiler_params=_cparams(("parallel",)),
        name="merge_ln1_router",
    )(x, av, av, av, o_b, *o_cs, *lses, vz, pmix, pscale, wbr, wout, g, b, wr)


def _count(mask):
    return jnp.sum(jnp.sum(jnp.where(mask, 1.0, 0.0), axis=1), axis=1, keepdims=True)


def _thr_kernel(aff_ref, t_ref, need_ref, *, cap):
    bits = pltpu.bitcast(aff_ref[...], I32)

    def body(k, t):
        cand = t | jnp.left_shift(jnp.int32(1), 30 - k)
        return jnp.where(_count(bits >= cand[:, :, None]) >= cap, cand, t)

    t = lax.fori_loop(0, 31, body, jnp.zeros((N_EXPERTS, 1), I32))
    need = cap - _count(bits > t[:, :, None])
    t_ref[...] = jnp.broadcast_to(t, t_ref.shape)
    need_ref[...] = jnp.broadcast_to(need.astype(I32), need_ref.shape)


def _pos_kernel(t_ref, need_ref, aff_ref, clt_ref, ext_ref, lo_ref, hi_ref, *, NC):
    e = pl.program_id(0)
    bits = pltpu.bitcast(aff_ref[...], I32)
    t = t_ref[e]
    need = need_ref[e].astype(F32)
    upper = jnp.where(lax.broadcasted_iota(I32, (LANES, LANES), 0) <= lax.broadcasted_iota(I32, (LANES, LANES), 1),
                      1.0, 0.0).astype(BF16)
    ones = jnp.ones((LANES, LANES), BF16)
    before = jnp.where(lax.broadcasted_iota(I32, (NC, NC), 1) < lax.broadcasted_iota(I32, (NC, NC), 0),
                       1.0, 0.0).astype(BF16)

    def cums(xf):
        xb = xf.astype(BF16)
        local = jnp.dot(xb, upper, preferred_element_type=F32)
        tot = jnp.dot(xb, ones, preferred_element_type=F32)
        offs = jnp.dot(before, tot.astype(BF16), preferred_element_type=F32)
        return local, tot, offs

    eq = jnp.where(bits == t, 1.0, 0.0)
    local, _, offs = cums(eq)
    sel = (bits > t) | ((bits == t) & (local + offs - eq < need))
    local, tot, offs = cums(jnp.where(sel, 1.0, 0.0))
    lo_ref[...] = offs
    hi_ref[...] = offs + tot
    clt_ref[...] = local.T
    chunk = lax.broadcasted_iota(I32, (NC, LANES), 0)
    lane = lax.broadcasted_iota(I32, (NC, LANES), 1)
    offs_i = offs.astype(I32)
    ext = jnp.where(lane == 0, offs_i >> 8,
                    jnp.where(lane == 1, offs_i & 255,
                              jnp.where(lane == 2, chunk >> 8, jnp.where(lane == 3, chunk & 255, 0))))
    ext_ref[...] = ext.astype(F32).T[:SUBLANES]


def _idx_kernel(clt_ref, ext_ref, lo_ref, hi_ref, idx_ref, *, SB):
    s = (pl.program_id(1) * SB + lax.broadcasted_iota(I32, (1, SB), 1)).astype(F32)
    onehot = jnp.where((lo_ref[:, 0:1] <= s) & (s < hi_ref[:, 0:1]), 1.0, 0.0).astype(BF16)
    cg = jnp.dot(clt_ref[...].astype(BF16), onehot, preferred_element_type=F32)
    ex = jnp.dot(ext_ref[...].astype(BF16), onehot, preferred_element_type=F32)
    slot_lo = ex[0:1] * 256.0 + ex[1:2]
    chunk = ex[2:3] * 256.0 + ex[3:4]
    within = jnp.sum(jnp.where(cg <= s - slot_lo, 1.0, 0.0), axis=0, keepdims=True)
    idx_ref[...] = (chunk * LANES + within).astype(I32)


def _select(aff_t, cap):
    E, N = aff_t.shape
    NC = N // LANES
    aff3 = aff_t.reshape(E, NC, LANES)
    t, need = pl.pallas_call(
        functools.partial(_thr_kernel, cap=cap),
        out_shape=(jax.ShapeDtypeStruct((E, LANES), I32), jax.ShapeDtypeStruct((E, LANES), I32)),
        compiler_params=pltpu.CompilerParams(vmem_limit_bytes=VMEM_LIMIT),
        name="select_threshold",
    )(aff3)
    per_e = lambda e: (e, 0, 0)
    smem = pl.BlockSpec(memory_space=pltpu.SMEM)
    clt, ext, lo, hi = pl.pallas_call(
        functools.partial(_pos_kernel, NC=NC),
        out_shape=(jax.ShapeDtypeStruct((E, LANES, NC), F32), jax.ShapeDtypeStruct((E, SUBLANES, NC), F32),
                   jax.ShapeDtypeStruct((E, NC, LANES), F32), jax.ShapeDtypeStruct((E, NC, LANES), F32)),
        grid=(E,),
        in_specs=[smem, smem, pl.BlockSpec((None, NC, LANES), per_e)],
        out_specs=(pl.BlockSpec((None, LANES, NC), per_e), pl.BlockSpec((None, SUBLANES, NC), per_e),
                   pl.BlockSpec((None, NC, LANES), per_e), pl.BlockSpec((None, NC, LANES), per_e)),
        compiler_params=_cparams(("parallel",)),
        name="select_prefix",
    )(t[:, 0], need[:, 0], aff3)
    SB = min(cap, 1024)
    per_e2 = lambda e, s: (e, 0, 0)
    idx = pl.pallas_call(
        functools.partial(_idx_kernel, SB=SB),
        out_shape=jax.ShapeDtypeStruct((E, 1, cap), I32),
        grid=(E, cap // SB),
        in_specs=[pl.BlockSpec((None, LANES, NC), per_e2), pl.BlockSpec((None, SUBLANES, NC), per_e2),
                  pl.BlockSpec((None, NC, LANES), per_e2), pl.BlockSpec((None, NC, LANES), per_e2)],
        out_specs=pl.BlockSpec((None, 1, SB), lambda e, s: (e, 0, s)),
        compiler_params=_cparams(("parallel", "arbitrary")),
        name="select_index",
    )(clt, ext, lo, hi)
    return idx.reshape(E, cap)


def _gather(x, idx_flat, K):
    N, D = x.shape
    R = idx_flat.shape[0]
    workers = SC_CORES * SC_SUBCORES
    per_w = R // workers
    mesh = plsc.VectorSubcoreMesh(core_axis_name="c", subcore_axis_name="s")

    @functools.partial(
        pl.kernel, mesh=mesh,
        out_type=jax.ShapeDtypeStruct((R, D), x.dtype),
        scratch_types=[pltpu.VMEM((K,), I32), pltpu.VMEM((K, D), x.dtype), pltpu.SemaphoreType.DMA],
        name="gather_rows_sc",
    )
    def run(x_hbm, idx_hbm, o_hbm, idx_v, rows_v, sem):
        base = (lax.axis_index("s") * SC_CORES + lax.axis_index("c")) * per_w

        @pl.loop(0, per_w // K)
        def _(c):
            off = base + c * K
            pltpu.sync_copy(idx_hbm.at[pl.ds(off, K)], idx_v)
            pltpu.async_copy(x_hbm.at[idx_v], rows_v, sem).wait()
            pltpu.sync_copy(rows_v, o_hbm.at[pl.ds(off, K)])

    return run(x, idx_flat)


def _ffn_kernel(x_ref, wg_ref, wu_ref, wd_ref, wr_ref, lo_ref, hi_ref):
    e = pl.program_id(0)
    xb = x_ref[...].astype(BF16)
    aff = _router_aff(xb, wr_ref[...])
    lane = lax.broadcasted_iota(I32, aff.shape, 1)
    val = jnp.sum(jnp.where(lane == e, aff, 0.0), axis=1, keepdims=True)
    gate = jnp.dot(xb, wg_ref[...], preferred_element_type=F32)
    up = jnp.dot(xb, wu_ref[...], preferred_element_type=F32)
    hid = gate * jax.nn.sigmoid(gate) * up
    y = jnp.dot(hid.astype(BF16), wd_ref[...], preferred_element_type=F32) * val
    half = y.shape[1] // 2
    lo_ref[...] = y[:, :half]
    hi_ref[...] = y[:, half:]


def _ffn(xe, wg, wu, wd, wr, cap, tc):
    R, D = xe.shape
    E, _, F = wg.shape
    nb = cap // tc
    half = jax.ShapeDtypeStruct((R, D // 2), F32)
    return pl.pallas_call(
        _ffn_kernel,
        out_shape=(half, half),
        grid=(E, nb),
        in_specs=[pl.BlockSpec((tc, D), lambda e, c: (e * nb + c, 0)),
                  pl.BlockSpec((None, D, F), lambda e, c: (e, 0, 0)),
                  pl.BlockSpec((None, D, F), lambda e, c: (e, 0, 0)),
                  pl.BlockSpec((None, F, D), lambda e, c: (e, 0, 0)),
                  pl.BlockSpec(wr.shape, lambda e, c: (0, 0))],
        out_specs=(pl.BlockSpec((tc, D // 2), lambda e, c: (e * nb + c, 0)),
                   pl.BlockSpec((tc, D // 2), lambda e, c: (e * nb + c, 0))),
        compiler_params=_cparams(("parallel", "arbitrary")),
        name="expert_swiglu",
    )(xe, wg, wu, wd, wr)


def _scatter_add(n_tokens, ye_lo, ye_hi, idx_flat, cap, K):
    R, H = ye_lo.shape
    per_s = cap // SC_SUBCORES
    mesh = plsc.VectorSubcoreMesh(core_axis_name="c", subcore_axis_name="s")
    lanes = SC_LANES

    @functools.partial(
        pl.kernel, mesh=mesh, out_type=(),
        scratch_types=[pltpu.VMEM((K,), I32), pltpu.VMEM((K, H), F32), pltpu.VMEM((K, H), F32),
                       pltpu.SemaphoreType.DMA],
        name="scatter_add_sc",
    )
    def run(ylo_hbm, yhi_hbm, yelo_hbm, yehi_hbm, idx_hbm, idx_v, acc_v, add_v, sem):
        s = lax.axis_index("s")

        def half(y_hbm, ye_hbm):
            @pl.loop(0, N_EXPERTS)
            def _(e):
                @pl.loop(0, per_s // K)
                def _(ch):
                    off = e * cap + s * per_s + ch * K
                    pltpu.sync_copy(idx_hbm.at[pl.ds(off, K)], idx_v)
                    pltpu.sync_copy(ye_hbm.at[pl.ds(off, K)], add_v)
                    pltpu.async_copy(y_hbm.at[idx_v], acc_v, sem).wait()

                    @pl.loop(0, K)
                    def _(r):
                        for j in range(H // lanes):
                            sl = pl.ds(j * lanes, lanes)
                            acc_v[r, sl] = acc_v[r, sl] + add_v[r, sl]

                    pltpu.async_copy(acc_v, y_hbm.at[idx_v], sem).wait()

                plsc.subcore_barrier()

        @pl.when(lax.axis_index("c") == 0)
        def _():
            half(ylo_hbm, yelo_hbm)

        @pl.when(lax.axis_index("c") == 1)
        def _():
            half(yhi_hbm, yehi_hbm)

    y_lo = jax.new_ref(jnp.zeros((n_tokens, H), F32))
    y_hi = jax.new_ref(jnp.zeros((n_tokens, H), F32))
    run(y_lo, y_hi, ye_lo, ye_hi, idx_flat)
    return y_lo[...], y_hi[...]


def _ln2_kernel(x1_ref, ylo_ref, yhi_ref, g_ref, b_ref, x2_ref, xb_ref, *, alpha):
    y = jnp.concatenate([ylo_ref[...], yhi_ref[...]], axis=1)
    x2 = _layer_norm(alpha * x1_ref[...] + y, g_ref[...], b_ref[...])
    x2_ref[...] = x2
    xb_ref[...] = x2.astype(BF16)


def _ln2(x1, y_lo, y_hi, g, b, tm, alpha):
    N, D = x1.shape
    row = lambda i: (i, 0)
    full = lambda i: (0, 0)
    return pl.pallas_call(
        functools.partial(_ln2_kernel, alpha=alpha),
        out_shape=(jax.ShapeDtypeStruct((N, D), F32), jax.ShapeDtypeStruct((N, D), BF16)),
        grid=(N // tm,),
        in_specs=[pl.BlockSpec((tm, D), row), pl.BlockSpec((tm, D // 2), row), pl.BlockSpec((tm, D // 2), row),
                  pl.BlockSpec(g.shape, full), pl.BlockSpec(b.shape, full)],
        out_specs=(pl.BlockSpec((tm, D), row), pl.BlockSpec((tm, D), row)),
        compiler_params=_cparams(("parallel",)),
        name="residual_ln2",
    )(x1, y_lo, y_hi, g, b)


def _tiles(B, S):
    N = B * S
    tm = 1024 if S % 1024 == 0 else S
    return dict(tm_proj=tm, tq_b=min(256, S), tq_c=512, tm_merge=512,
                tc=min(512, CAPACITY_FACTOR * N // N_EXPERTS))


def _prep_layer(l, w_in, pool_mix, pool_scale, sink_logit, w_branch, w_out, ln1_g, ln1_b, w_router,
                w_gate_e, w_up_e, w_down_e, ln2_g, ln2_b):
    D = w_in.shape[1]
    a0, qb0 = 0, A_WIDTH
    kb0 = qb0 + B_Q_HEADS * B_HEAD_DIM
    vb0 = kb0 + B_KV_HEADS * B_HEAD_DIM
    qc0 = vb0 + B_KV_HEADS * B_HEAD_DIM
    kc0 = qc0 + len(C_CONFIGS) * C_GROUP_W
    vc0 = kc0 + len(C_CONFIGS) * C_GROUP_W
    zg0 = vc0 + len(C_CONFIGS) * C_GROUP_W
    spread = np.concatenate([h * C_HEAD_DIM + _spread_head_perm(C_HEAD_DIM) for h in range(C_HEADS)])
    wl = w_in[l]
    wr = jnp.zeros((D, LANES), F32).at[:, :N_EXPERTS].set(w_router[l])
    return dict(
        w_qkb=wl[:, qb0:vb0].astype(BF16),
        w_avb=jnp.concatenate([wl[:, a0:qb0], wl[:, vb0:qc0]], axis=1).astype(BF16),
        w_c=[jnp.concatenate([wl[:, c0 + g * C_GROUP_W:c0 + (g + 1) * C_GROUP_W][:, cols]
                              for c0, cols in ((qc0, spread), (kc0, spread), (vc0, np.arange(C_GROUP_W)))],
                             axis=1).astype(BF16) for g in range(len(C_CONFIGS))],
        w_zg=wl[:, zg0:].astype(BF16),
        pmix=pool_mix[l].astype(BF16), pscale=pool_scale[l][None, :], sink=sink_logit[l],
        wbr=w_branch[l].astype(BF16), wout=w_out[l].astype(BF16),
        g1=ln1_g[l][None, :], b1=ln1_b[l][None, :], wr=wr.astype(BF16),
        wg=w_gate_e[l].astype(BF16), wu=w_up_e[l].astype(BF16), wd=w_down_e[l].astype(BF16),
        g2=ln2_g[l][None, :], b2=ln2_b[l][None, :],
    )


def _layer(x, xb, p, B, S, alpha, rot_b, rot_c):
    N, D = x.shape
    t = _tiles(B, S)
    cap = CAPACITY_FACTOR * N // N_EXPERTS
    qkb = _proj(xb, p["w_qkb"], S, t["tm_proj"], p["w_qkb"].shape[1], BF16, rot_b)
    avb = _proj(xb, p["w_avb"], S, t["tm_proj"], p["w_avb"].shape[1], F32)
    zg = _proj(xb, p["w_zg"], S, t["tm_proj"], 1024, BF16)
    o_b = _attn_b(qkb, avb, p["sink"], B, S, t["tq_b"])
    o_cs, lses = [], []
    for g, (_, d) in enumerate(C_CONFIGS):
        qkv = _proj_fold(xb, p["w_c"][g], B, S, d, t["tm_proj"], 2 * C_GROUP_W, rot_c)
        o, lse = _attn_c(qkv, g, B, S, t["tq_c"])
        o_cs.append(o)
        lses.append(lse)
    x1, aff_t = _merge(x, avb, o_b, o_cs, lses, zg, p["pmix"], p["pscale"], p["wbr"], p["wout"], p["g1"], p["b1"],
                       p["wr"], S, t["tm_merge"], alpha)
    idx_flat = _select(aff_t, cap).reshape(-1)
    xe = _gather(x1, idx_flat, SC_GATHER_ROWS)
    ye_lo, ye_hi = _ffn(xe, p["wg"], p["wu"], p["wd"], p["wr"], cap, t["tc"])
    y_lo, y_hi = _scatter_add(N, ye_lo, ye_hi, idx_flat, cap, SC_GATHER_ROWS)
    return _ln2(x1, y_lo, y_hi, p["g2"], p["b2"], t["tm_proj"], alpha)


def _trunk(x, layers, alpha):
    B, S, D = x.shape
    rot_b = _rot_tables(S, B_HEAD_DIM)
    rot_c = _rot_tables_spread(S, C_HEAD_DIM)
    xf = x.reshape(B * S, D)
    xb = xf.astype(BF16)
    for p in layers:
        xf, xb = _layer(xf, xb, p, B, S, alpha, rot_b, rot_c)
    return xf.reshape(B, S, D)


def kernel(x_prompt, x_sample, w_in, pool_mix, pool_scale, sink_logit, w_branch, w_out, ln1_g, ln1_b, w_router,
           w_gate_e, w_up_e, w_down_e, ln2_g, ln2_b):
    depth = w_in.shape[0]
    alpha = (2 * depth) ** 0.25
    layers = [_prep_layer(l, w_in, pool_mix, pool_scale, sink_logit, w_branch, w_out, ln1_g, ln1_b, w_router,
                          w_gate_e, w_up_e, w_down_e, ln2_g, ln2_b) for l in range(depth)]
    return (_trunk(x_prompt, layers, alpha), _trunk(x_sample, layers, alpha))
```

```python
import functools

import numpy as np
import jax
import jax.numpy as jnp
from jax import lax
from jax.experimental import pallas as pl
from jax.experimental.pallas import tpu as pltpu
from jax.experimental.pallas import tpu_sc as plsc

F32 = jnp.float32
BF16 = jnp.bfloat16
I32 = jnp.int32

POOL_WINDOWS = (2, 4, 8, 16)
POOL_CH = 128
A_WIDTH = 512
B_Q_HEADS = 8
B_KV_HEADS = 2
B_HEAD_DIM = 64
B_HALF_WINDOW = 128
C_CONFIGS = ((128, 1), (512, 4), (2048, 16))
C_HEADS = 4
C_HEAD_DIM = 128
C_GROUP_W = C_HEADS * C_HEAD_DIM
N_EXPERTS = 16
CAPACITY_FACTOR = 2
ROPE_THETA = 500000.0
ROT_DIV = 4
LN_EPS = 1e-5
NEG_INF = -1e30
LOG2E = 1.4426950408889634
LN2 = 0.6931471805599453

LANES = 128
SUBLANES = 8
VMEM_LIMIT = 56 * 1024 * 1024
SC_CORES = 2
SC_SUBCORES = 16
SC_LANES = 16
SC_GATHER_ROWS = 64
MXU_WIDTH = 256
ATTN_STRIP = 32
ATTN_UNROLL = 8
PERM_BLOCK = MXU_WIDTH


def _cparams(sem):
    return pltpu.CompilerParams(dimension_semantics=sem, vmem_limit_bytes=VMEM_LIMIT)


def _col_chunks(width):
    return [(c, min(c + MXU_WIDTH, width)) for c in range(0, width, MXU_WIDTH)]


def _rotary(acc, c, s1, s2, rot_half):
    outs = []
    for g in range(acc.shape[1] // LANES):
        a = acc[:, g * LANES:(g + 1) * LANES]
        if 2 * rot_half == LANES:
            outs.append(a * c + pltpu.roll(a, rot_half, 1) * s1)
        else:
            outs.append(a * c + pltpu.roll(a, LANES - rot_half, 1) * s1 + pltpu.roll(a, rot_half, 1) * s2)
    return outs[0] if len(outs) == 1 else jnp.concatenate(outs, axis=1)


def _proj_kernel(x_ref, w_ref, *rest, rot_half):
    if rot_half:
        cos_ref, s1_ref, s2_ref, o_ref = rest
    else:
        (o_ref,) = rest
    x = x_ref[...]
    for c0, c1 in _col_chunks(o_ref.shape[1]):
        acc = jnp.dot(x, w_ref[:, c0:c1], preferred_element_type=F32)
        if rot_half:
            acc = _rotary(acc, cos_ref[...], s1_ref[...], s2_ref[...], rot_half)
        o_ref[:, c0:c1] = acc.astype(o_ref.dtype)


def _proj(x, w, S, tm, tn, out_dtype, rot=None):
    N, K = x.shape
    C = w.shape[1]
    in_specs = [pl.BlockSpec((tm, K), lambda i, j: (i, 0)),
                pl.BlockSpec((K, tn), lambda i, j: (0, j))]
    args = [x, w]
    rot_half = 0
    if rot is not None:
        rot_half, tabs = rot
        spb = S // tm
        in_specs += [pl.BlockSpec((tm, LANES), lambda i, j: (i % spb, 0))] * 3
        args += list(tabs)
    return pl.pallas_call(
        functools.partial(_proj_kernel, rot_half=rot_half),
        out_shape=jax.ShapeDtypeStruct((N, C), out_dtype),
        grid=(N // tm, C // tn),
        in_specs=in_specs,
        out_specs=pl.BlockSpec((tm, tn), lambda i, j: (i, j)),
        compiler_params=_cparams(("parallel", "arbitrary")),
        name="proj",
    )(*args)


def _fold_perm(d, inverse=False):
    i = lax.broadcasted_iota(I32, (PERM_BLOCK, PERM_BLOCK), 1 if inverse else 0)
    j = lax.broadcasted_iota(I32, (PERM_BLOCK, PERM_BLOCK), 0 if inverse else 1)
    per = PERM_BLOCK // d
    return jnp.where(j == (i % per) * d + i // per, 1.0, 0.0).astype(BF16)


def _proj_fold_kernel(x_ref, w_ref, cos_ref, s1_ref, s2_ref, o_ref, *scr, rot_half, d, rot_cols):
    x = x_ref[...]
    tm = x.shape[0]
    for c0, c1 in _col_chunks(w_ref.shape[1]):
        acc = jnp.dot(x, w_ref[:, c0:c1], preferred_element_type=F32)
        if c0 < rot_cols:
            acc = _rotary(acc, cos_ref[...], s1_ref[...], s2_ref[...], rot_half)
        if d == 1:
            o_ref[0, :, c0:c1] = acc.astype(BF16)
            continue
        for g in range(c0 // LANES, c1 // LANES):
            scr[0][g] = acc[:, g * LANES - c0:(g + 1) * LANES - c0]
            for rr in range(d):
                o_ref[rr, :, g * LANES:(g + 1) * LANES] = scr[0][g, pl.ds(rr, tm // d, stride=d), :].astype(BF16)


def _proj_fold(x, w, B, S, d, tm, rot_cols, rot):
    N, K = x.shape
    C = w.shape[1]
    rot_half, tabs = rot
    spb = S // tm
    tab_spec = pl.BlockSpec((tm, LANES), lambda i: (i % spb, 0))
    return pl.pallas_call(
        functools.partial(_proj_fold_kernel, rot_half=rot_half, d=d, rot_cols=rot_cols),
        out_shape=jax.ShapeDtypeStruct((B * d, S // d, C), BF16),
        grid=(N // tm,),
        in_specs=[pl.BlockSpec((tm, K), lambda i: (i, 0)), pl.BlockSpec((K, C), lambda i: (0, 0)),
                  tab_spec, tab_spec, tab_spec],
        out_specs=pl.BlockSpec((d, tm // d, C), lambda i: (i // spb, i % spb, 0)),
        scratch_shapes=[pltpu.VMEM((C // LANES, tm, LANES), F32)] if d > 1 else [],
        compiler_params=_cparams(("parallel",)),
        name=f"proj_fold_{d}",
    )(x, w, *tabs)


def _rot_tables(S, head_dim):
    rot = head_dim // ROT_DIV
    half = rot // 2
    inv = 1.0 / (ROPE_THETA ** (jnp.arange(0, rot, 2, dtype=F32) / rot))
    ang = jnp.arange(S, dtype=F32)[:, None] * inv[None, :]
    cos, sin = jnp.cos(ang), jnp.sin(ang)
    d = np.arange(LANES) % head_dim
    first = d < half
    second = (d >= half) & (d < rot)
    src = np.where(first, d, np.where(second, d - half, 0))
    cos_t = jnp.where(first | second, cos[:, src], 1.0)
    s1_t = jnp.where(first, -sin[:, src], 0.0)
    s2_t = jnp.where(second, sin[:, src], 0.0)
    return half, (cos_t, s1_t, s2_t)


def _spread_head_perm(head_dim):
    half = head_dim // ROT_DIV // 2
    mid = LANES // 2
    return np.concatenate([np.arange(0, half), np.arange(2 * half, mid + half), np.arange(half, 2 * half),
                           np.arange(mid + half, head_dim)])


def _rot_tables_spread(S, head_dim):
    half, (cos_t, s1_t, s2_t) = _rot_tables(S, head_dim)
    perm = _spread_head_perm(head_dim)
    return LANES // 2, (cos_t[:, perm], (s1_t + s2_t)[:, perm], s2_t)


def _band_bias(bias_ref, W, q0, L, sub):
    tq, nk = bias_ref.shape
    row = lax.broadcasted_iota(I32, (tq, nk), 0)
    col = lax.broadcasted_iota(I32, (tq, nk), 1)
    first = (row // sub) * sub
    kpos = q0 - W + first + col
    valid = (jnp.abs(col - W - (row - first)) <= W) & (kpos >= 0) & (kpos < L)
    bias_ref[...] = jnp.where(valid, 0.0, NEG_INF)


def _softmax_strips(s_ref, p_ref, den_ref, max_ref, bias_ref, *, unroll, sink_fn=None):
    rows, nk = s_ref.shape
    tq = bias_ref.shape[0]

    def strip(t):
        r0 = pl.multiple_of(t * ATTN_STRIP, ATTN_STRIP)
        rows_sl = pl.ds(r0, ATTN_STRIP)
        s = s_ref[rows_sl, :] + bias_ref[pl.ds(pl.multiple_of(r0 % tq, ATTN_STRIP), ATTN_STRIP), :]
        return r0, rows_sl, s

    def row_max(t, carry):
        r0, rows_sl, s = strip(t)
        m = jnp.max(s, axis=1, keepdims=True)
        if sink_fn is not None:
            m = jnp.maximum(m, sink_fn(r0 // tq) * LOG2E)
        max_ref[rows_sl, :] = jnp.broadcast_to(m, (ATTN_STRIP, LANES))
        return carry

    def numerators(t, carry):
        r0, rows_sl, s = strip(t)
        m = max_ref[rows_sl, :]
        p = jnp.exp2(s - jnp.concatenate([m] * (nk // LANES), axis=1))
        den = jnp.broadcast_to(jnp.sum(p, axis=1, keepdims=True), (ATTN_STRIP, LANES))
        if sink_fn is not None:
            den = den + jnp.exp2(sink_fn(r0 // tq) * LOG2E - m)
        p_ref[rows_sl, :] = p.astype(BF16)
        den_ref[rows_sl, :] = den
        return carry

    lax.fori_loop(0, rows // ATTN_STRIP, row_max, 0, unroll=unroll)
    lax.fori_loop(0, rows // ATTN_STRIP, numerators, 0, unroll=unroll)


def _attn_b_kernel(sink_ref, q_ref, kp_ref, km_ref, kn_ref, vp_ref, vm_ref, vn_ref, o_ref,
                   s_ref, p_ref, den_ref, max_ref, bias_ref, *, S, TQ):
    W = B_HALF_WINDOW
    G = B_Q_HEADS // B_KV_HEADS
    q0 = pl.program_id(1) * TQ
    _band_bias(bias_ref, W, q0, S, TQ)
    k_all = jnp.concatenate([kp_ref[...], km_ref[...], kn_ref[...]], axis=0)
    v_all = jnp.concatenate([vp_ref[...], vm_ref[...], vn_ref[...]], axis=0).astype(BF16)
    q = q_ref[...]
    outs = []
    for j in range(B_KV_HEADS):
        kj = k_all[:, j * B_HEAD_DIM:(j + 1) * B_HEAD_DIM]
        vj = v_all[:, j * B_HEAD_DIM:(j + 1) * B_HEAD_DIM]
        qs = jnp.concatenate([q[:, (G * j + g) * B_HEAD_DIM:(G * j + g + 1) * B_HEAD_DIM] for g in range(G)], axis=0)
        s_ref[j] = lax.dot_general(qs, kj, (((1,), (1,)), ((), ())),
                                   preferred_element_type=F32) * (B_HEAD_DIM ** -0.5 * LOG2E)
        _softmax_strips(s_ref.at[j], p_ref.at[j], den_ref.at[j], max_ref.at[j], bias_ref, unroll=ATTN_UNROLL,
                        sink_fn=lambda g, j=j: sink_ref[G * j + g])
        o = jnp.dot(p_ref[j], vj, preferred_element_type=F32) / den_ref[j, :, 0:1]
        outs += [o[g * TQ:(g + 1) * TQ] for g in range(G)]
    o_ref[...] = jnp.concatenate(outs, axis=1).astype(o_ref.dtype)


def _attn_b(qk, av, sink, B, S, TQ):
    N = B * S
    W = B_HALF_WINDOW
    nq = S // TQ
    r = TQ // W
    kcol = (B_Q_HEADS * B_HEAD_DIM) // LANES
    last = N // W - 1
    rows = (B_Q_HEADS // B_KV_HEADS) * TQ

    def prev(b, i):
        return (jnp.maximum((b * nq + i) * r - 1, 0), kcol)

    def main(b, i):
        return (b * nq + i, kcol)

    def nxt(b, i):
        return (jnp.minimum((b * nq + i + 1) * r, last), kcol)

    return pl.pallas_call(
        functools.partial(_attn_b_kernel, S=S, TQ=TQ),
        out_shape=jax.ShapeDtypeStruct((N, B_Q_HEADS * B_HEAD_DIM), BF16),
        grid=(B, nq),
        in_specs=[pl.BlockSpec(memory_space=pltpu.SMEM),
                  pl.BlockSpec((TQ, B_Q_HEADS * B_HEAD_DIM), lambda b, i: (b * nq + i, 0)),
                  pl.BlockSpec((W, LANES), prev), pl.BlockSpec((TQ, LANES), main), pl.BlockSpec((W, LANES), nxt),
                  pl.BlockSpec((W, LANES), prev), pl.BlockSpec((TQ, LANES), main), pl.BlockSpec((W, LANES), nxt)],
        out_specs=pl.BlockSpec((TQ, B_Q_HEADS * B_HEAD_DIM), lambda b, i: (b * nq + i, 0)),
        scratch_shapes=[pltpu.VMEM((B_KV_HEADS, rows, TQ + 2 * W), F32),
                        pltpu.VMEM((B_KV_HEADS, rows, TQ + 2 * W), BF16),
                        pltpu.VMEM((B_KV_HEADS, rows, LANES), F32), pltpu.VMEM((B_KV_HEADS, rows, LANES), F32),
                        pltpu.VMEM((TQ, TQ + 2 * W), F32)],
        compiler_params=_cparams(("parallel", "arbitrary")),
        name="attn_window",
    )(sink, qk, qk, qk, qk, av, av, av)


def _attn_c_kernel(q_ref, kp_ref, km_ref, kn_ref, vp_ref, vm_ref, vn_ref, o_ref, lse_ref,
                   s_ref, p_ref, den_ref, max_ref, bias_ref, *, M, TQ, W):
    sub, span = 2 * W, 4 * W
    q0 = pl.program_id(1) * TQ
    _band_bias(bias_ref, W, q0, M, sub)
    k_all = jnp.concatenate([kp_ref[...], km_ref[...], kn_ref[...]], axis=0)
    v_all = jnp.concatenate([vp_ref[...], vm_ref[...], vn_ref[...]], axis=0)
    q = q_ref[...]
    lane = lax.broadcasted_iota(I32, (TQ, LANES), 1)
    lse_tile = jnp.zeros((TQ, LANES), F32)
    outs = []
    for h in range(C_HEADS):
        sl = slice(h * C_HEAD_DIM, (h + 1) * C_HEAD_DIM)
        for b in range(TQ // sub):
            s_ref[h, b * sub:(b + 1) * sub, :] = lax.dot_general(
                q[b * sub:(b + 1) * sub, sl], k_all[b * sub:b * sub + span, sl], (((1,), (1,)), ((), ())),
                preferred_element_type=F32) * (C_HEAD_DIM ** -0.5 * LOG2E)
        _softmax_strips(s_ref.at[h], p_ref.at[h], den_ref.at[h], max_ref.at[h], bias_ref, unroll=True)
        den = den_ref[h]
        pv = [jnp.dot(p_ref[h, b * sub:(b + 1) * sub, :], v_all[b * sub:b * sub + span, sl],
                      preferred_element_type=F32) for b in range(TQ // sub)]
        outs.append(jnp.concatenate(pv, axis=0) / den[:, 0:1])
        lse_tile = jnp.where(lane == h, max_ref[h] * LN2 + jnp.log(den), lse_tile)
    o_ref[...] = jnp.concatenate(outs, axis=1).astype(o_ref.dtype)
    lse_ref[...] = lse_tile


def _attn_c(qkv, g, B, S, TQ):
    window, d = C_CONFIGS[g]
    W = window // (2 * d)
    M = S // d
    TQ = min(TQ, M)
    r = TQ // W
    GW = C_GROUP_W
    last = M // W - 1

    def mk(col):
        return [pl.BlockSpec((None, W, GW), lambda b, i: (b, jnp.maximum(i * r - 1, 0), col)),
                pl.BlockSpec((None, TQ, GW), lambda b, i: (b, i, col)),
                pl.BlockSpec((None, W, GW), lambda b, i: (b, jnp.minimum((i + 1) * r, last), col))]

    return pl.pallas_call(
        functools.partial(_attn_c_kernel, M=M, TQ=TQ, W=W),
        out_shape=(jax.ShapeDtypeStruct((B * d, M, GW), BF16), jax.ShapeDtypeStruct((B * d, M, LANES), F32)),
        grid=(B * d, M // TQ),
        in_specs=[pl.BlockSpec((None, TQ, GW), lambda b, i: (b, i, 0))] + mk(1) + mk(2),
        out_specs=(pl.BlockSpec((None, TQ, GW), lambda b, i: (b, i, 0)),
                   pl.BlockSpec((None, TQ, LANES), lambda b, i: (b, i, 0))),
        scratch_shapes=[pltpu.VMEM((C_HEADS, TQ, 4 * W), F32), pltpu.VMEM((C_HEADS, TQ, 4 * W), BF16),
                        pltpu.VMEM((C_HEADS, TQ, LANES), F32), pltpu.VMEM((C_HEADS, TQ, LANES), F32),
                        pltpu.VMEM((TQ, 4 * W), F32)],
        compiler_params=_cparams(("parallel", "arbitrary")),
        name=f"attn_dilated_{d}",
    )(qkv, qkv, qkv, qkv, qkv, qkv, qkv)


def _sigmoid(z):
    return 0.5 * jnp.tanh(0.5 * z) + 0.5


def _layer_norm(h, g, b):
    mu = jnp.mean(h, axis=-1, keepdims=True)
    var = jnp.mean(jnp.square(h - mu), axis=-1, keepdims=True)
    return (h - mu) * lax.rsqrt(var + LN_EPS) * g + b


def _router_aff(xb, wr):
    logits = jnp.dot(xb, wr, preferred_element_type=F32)
    lane = lax.broadcasted_iota(I32, logits.shape, 1)
    logits = jnp.where(lane < N_EXPERTS, logits, NEG_INF)
    e = jnp.exp(logits - jnp.max(logits, axis=1, keepdims=True))
    return e / jnp.sum(e, axis=1, keepdims=True)


def _unfold(ref, scr_ref):
    d, rows, _ = ref.shape
    if d == 1:
        return ref[0]
    for rr in range(d):
        scr_ref[pl.ds(rr, rows, stride=d), :] = ref[rr]
    return scr_ref[...]


def _unfold_bf16(ref):
    d, rows, _ = ref.shape
    if d == 1:
        return ref[0].astype(F32)
    per = PERM_BLOCK // d
    perm = _fold_perm(d, inverse=True)
    outs = []
    for blk in range(d * rows // PERM_BLOCK):
        stacked = jnp.concatenate([ref[rr, blk * per:(blk + 1) * per, :] for rr in range(d)], axis=0)
        outs.append(jnp.dot(perm, stacked, preferred_element_type=F32))
    return outs[0] if len(outs) == 1 else jnp.concatenate(outs, axis=0)


def _merge_kernel(x_ref, a_ref, ap_ref, an_ref, ob_ref, oc0_ref, oc1_ref, oc2_ref, l0_ref, l1_ref, l2_ref,
                  zg_ref, pmix_ref, pscale_ref, wbr_ref, wout_ref, g_ref, b_ref, wr_ref,
                  x1_ref, aff_ref, sl1_ref, sl2_ref, *, S, tm, alpha):
    HW = max(POOL_WINDOWS) // 2
    L = tm + 2 * HW
    pos0 = (pl.program_id(0) * tm) % S
    xa = jnp.concatenate([ap_ref[...], a_ref[...], an_ref[...]], axis=0)
    xpos = pos0 - HW + lax.broadcasted_iota(I32, (L, 1), 0)
    xa = jnp.where((xpos >= 0) & (xpos < S), xa, 0.0)
    sums = {}
    t, w = xa, 1
    while w < max(POOL_WINDOWS):
        t = t + pltpu.roll(t, L - w, 0)
        w *= 2
        sums[w] = t
    pos = pos0 + lax.broadcasted_iota(I32, (tm, 1), 0)
    mixed = []
    for g, w in enumerate(POOL_WINDOWS):
        cs = slice(g * POOL_CH, (g + 1) * POOL_CH)
        off = HW - w // 2
        sw = sums[w][:, cs]
        if off:
            sw = pltpu.roll(sw, L - off, 0)
        sw = sw[:tm]
        cnt = (jnp.clip(pos + w // 2, 0, S) - jnp.clip(pos - w // 2, 0, S)).astype(F32)
        pooled = sw / cnt - a_ref[:, cs]
        mixed.append(jnp.dot(pooled.astype(BF16), pmix_ref[g], preferred_element_type=F32) * pscale_ref[:, cs])
    o_a = jnp.concatenate(mixed, axis=1)
    ls = [_unfold(l0_ref, None), _unfold(l1_ref, sl1_ref), _unfold(l2_ref, sl2_ref)]
    mx = jnp.maximum(jnp.maximum(ls[0], ls[1]), ls[2])
    es = [jnp.exp(l - mx) for l in ls]
    tot = es[0] + es[1] + es[2]
    ocs = [_unfold_bf16(oc0_ref), _unfold_bf16(oc1_ref), _unfold_bf16(oc2_ref)]
    pieces = []
    for h in range(C_HEADS):
        sl = slice(h * C_HEAD_DIM, (h + 1) * C_HEAD_DIM)
        acc = None
        for gi in range(3):
            term = (es[gi] / tot)[:, h:h + 1] * ocs[gi][:, sl].astype(F32)
            acc = term if acc is None else acc + term
        pieces.append(acc)
    o_c = jnp.concatenate(pieces, axis=1)
    branches = [o_a.astype(BF16), ob_ref[...], o_c.astype(BF16)]
    D = x_ref.shape[1]
    merged = []
    for c0, c1 in _col_chunks(D):
        acc = None
        for k in range(3):
            gate2 = jnp.tanh(zg_ref[:, k * D + c0:k * D + c1].astype(F32)) + 1.0
            term = gate2 * jnp.dot(branches[k], wbr_ref[k, :, c0:c1], preferred_element_type=F32)
            acc = term if acc is None else acc + term
        merged.append(acc.astype(BF16))
    merged = jnp.concatenate(merged, axis=1)
    pre = [alpha * x_ref[:, c0:c1] + jnp.dot(merged, wout_ref[:, c0:c1], preferred_element_type=F32)
           for c0, c1 in _col_chunks(D)]
    x1 = _layer_norm(jnp.concatenate(pre, axis=1), g_ref[...], b_ref[...])
    x1_ref[...] = x1
    aff = _router_aff(x1.astype(BF16), wr_ref[...])
    aff_ref[...] = aff.T[:N_EXPERTS]


def _merge(x, av, o_b, o_cs, lses, vz, pmix, pscale, wbr, wout, g, b, wr, S, tm, alpha):
    N, D = x.shape
    HW = max(POOL_WINDOWS) // 2
    r = tm // HW
    last = N // HW - 1
    row = lambda i: (i, 0)
    full2 = lambda i: (0, 0)
    full3 = lambda i: (0, 0, 0)
    spb = S // tm
    dils = [d for _, d in C_CONFIGS]

    def folded(width):
        return [pl.BlockSpec((d, tm // d, width), lambda i: (i // spb, i % spb, 0)) for d in dils]

    in_specs = [
        pl.BlockSpec((tm, D), row),
        pl.BlockSpec((tm, A_WIDTH), row),
        pl.BlockSpec((HW, A_WIDTH), lambda i: (jnp.maximum(i * r - 1, 0), 0)),
        pl.BlockSpec((HW, A_WIDTH), lambda i: (jnp.minimum((i + 1) * r, last), 0)),
        pl.BlockSpec((tm, B_Q_HEADS * B_HEAD_DIM), row),
        *folded(C_GROUP_W),
        *folded(LANES),
        pl.BlockSpec((tm, 3 * D), row),
        pl.BlockSpec(pmix.shape, full3),
        pl.BlockSpec(pscale.shape, full2),
        pl.BlockSpec(wbr.shape, full3),
        pl.BlockSpec(wout.shape, full2),
        pl.BlockSpec(g.shape, full2), pl.BlockSpec(b.shape, full2),
        pl.BlockSpec(wr.shape, full2),
    ]
    return pl.pallas_call(
        functools.partial(_merge_kernel, S=S, tm=tm, alpha=alpha),
        out_shape=(jax.ShapeDtypeStruct((N, D), F32), jax.ShapeDtypeStruct((N_EXPERTS, N), F32)),
        grid=(N // tm,),
        in_specs=in_specs,
        out_specs=(pl.BlockSpec((tm, D), row), pl.BlockSpec((N_EXPERTS, tm), lambda i: (0, i))),
        scratch_shapes=[pltpu.VMEM((tm, LANES), F32)] * 2,
        compiler_params=_cparams(("parallel",)),
        name="merge_ln1_router",
    )(x, av, av, av, o_b, *o_cs, *lses, vz, pmix, pscale, wbr, wout, g, b, wr)


def _count(mask):
    return jnp.sum(jnp.sum(jnp.where(mask, 1.0, 0.0), axis=1), axis=1, keepdims=True)


def _thr_kernel(aff_ref, t_ref, need_ref, *, cap):
    bits = pltpu.bitcast(aff_ref[...], I32)

    def body(k, t):
        cand = t | jnp.left_shift(jnp.int32(1), 30 - k)
        return jnp.where(_count(bits >= cand[:, :, None]) >= cap, cand, t)

    t = lax.fori_loop(0, 31, body, jnp.zeros((N_EXPERTS, 1), I32))
    need = cap - _count(bits > t[:, :, None])
    t_ref[...] = jnp.broadcast_to(t, t_ref.shape)
    need_ref[...] = jnp.broadcast_to(need.astype(I32), need_ref.shape)


def _pos_kernel(t_ref, need_ref, aff_ref, clt_ref, ext_ref, lo_ref, hi_ref, *, NC):
    e = pl.program_id(0)
    bits = pltpu.bitcast(aff_ref[...], I32)
    t = t_ref[e]
    need = need_ref[e].astype(F32)
    upper = jnp.where(lax.broadcasted_iota(I32, (LANES, LANES), 0) <= lax.broadcasted_iota(I32, (LANES, LANES), 1),
                      1.0, 0.0).astype(BF16)
    ones = jnp.ones((LANES, LANES), BF16)
    before = jnp.where(lax.broadcasted_iota(I32, (NC, NC), 1) < lax.broadcasted_iota(I32, (NC, NC), 0),
                       1.0, 0.0).astype(BF16)

    def cums(xf):
        xb = xf.astype(BF16)
        local = jnp.dot(xb, upper, preferred_element_type=F32)
        tot = jnp.dot(xb, ones, preferred_element_type=F32)
        offs = jnp.dot(before, tot.astype(BF16), preferred_element_type=F32)
        return local, tot, offs

    eq = jnp.where(bits == t, 1.0, 0.0)
    local, _, offs = cums(eq)
    sel = (bits > t) | ((bits == t) & (local + offs - eq < need))
    local, tot, offs = cums(jnp.where(sel, 1.0, 0.0))
    lo_ref[...] = offs
    hi_ref[...] = offs + tot
    clt_ref[...] = local.T
    chunk = lax.broadcasted_iota(I32, (NC, LANES), 0)
    lane = lax.broadcasted_iota(I32, (NC, LANES), 1)
    offs_i = offs.astype(I32)
    ext = jnp.where(lane == 0, offs_i >> 8,
                    jnp.where(lane == 1, offs_i & 255,
                              jnp.where(lane == 2, chunk >> 8, jnp.where(lane == 3, chunk & 255, 0))))
    ext_ref[...] = ext.astype(F32).T[:SUBLANES]


def _idx_kernel(clt_ref, ext_ref, lo_ref, hi_ref, idx_ref, *, SB):
    s = (pl.program_id(1) * SB + lax.broadcasted_iota(I32, (1, SB), 1)).astype(F32)
    onehot = jnp.where((lo_ref[:, 0:1] <= s) & (s < hi_ref[:, 0:1]), 1.0, 0.0).astype(BF16)
    cg = jnp.dot(clt_ref[...].astype(BF16), onehot, preferred_element_type=F32)
    ex = jnp.dot(ext_ref[...].astype(BF16), onehot, preferred_element_type=F32)
    slot_lo = ex[0:1] * 256.0 + ex[1:2]
    chunk = ex[2:3] * 256.0 + ex[3:4]
    within = jnp.sum(jnp.where(cg <= s - slot_lo, 1.0, 0.0), axis=0, keepdims=True)
    idx_ref[...] = (chunk * LANES + within).astype(I32)


def _select(aff_t, cap):
    E, N = aff_t.shape
    NC = N // LANES
    aff3 = aff_t.reshape(E, NC, LANES)
    t, need = pl.pallas_call(
        functools.partial(_thr_kernel, cap=cap),
        out_shape=(jax.ShapeDtypeStruct((E, LANES), I32), jax.ShapeDtypeStruct((E, LANES), I32)),
        compiler_params=pltpu.CompilerParams(vmem_limit_bytes=VMEM_LIMIT),
        name="select_threshold",
    )(aff3)
    per_e = lambda e: (e, 0, 0)
    smem = pl.BlockSpec(memory_space=pltpu.SMEM)
    clt, ext, lo, hi = pl.pallas_call(
        functools.partial(_pos_kernel, NC=NC),
        out_shape=(jax.ShapeDtypeStruct((E, LANES, NC), F32), jax.ShapeDtypeStruct((E, SUBLANES, NC), F32),
                   jax.ShapeDtypeStruct((E, NC, LANES), F32), jax.ShapeDtypeStruct((E, NC, LANES), F32)),
        grid=(E,),
        in_specs=[smem, smem, pl.BlockSpec((None, NC, LANES), per_e)],
        out_specs=(pl.BlockSpec((None, LANES, NC), per_e), pl.BlockSpec((None, SUBLANES, NC), per_e),
                   pl.BlockSpec((None, NC, LANES), per_e), pl.BlockSpec((None, NC, LANES), per_e)),
        compiler_params=_cparams(("parallel",)),
        name="select_prefix",
    )(t[:, 0], need[:, 0], aff3)
    SB = min(cap, 1024)
    per_e2 = lambda e, s: (e, 0, 0)
    idx = pl.pallas_call(
        functools.partial(_idx_kernel, SB=SB),
        out_shape=jax.ShapeDtypeStruct((E, 1, cap), I32),
        grid=(E, cap // SB),
        in_specs=[pl.BlockSpec((None, LANES, NC), per_e2), pl.BlockSpec((None, SUBLANES, NC), per_e2),
                  pl.BlockSpec((None, NC, LANES), per_e2), pl.BlockSpec((None, NC, LANES), per_e2)],
        out_specs=pl.BlockSpec((None, 1, SB), lambda e, s: (e, 0, s)),
        compiler_params=_cparams(("parallel", "arbitrary")),
        name="select_index",
    )(clt, ext, lo, hi)
    return idx.reshape(E, cap)


def _gather(x, idx_flat, K):
    N, D = x.shape
    R = idx_flat.shape[0]
    workers = SC_CORES * SC_SUBCORES
    per_w = R // workers
    mesh = plsc.VectorSubcoreMesh(core_axis_name="c", subcore_axis_name="s")

    @functools.partial(
        pl.kernel, mesh=mesh,
        out_type=jax.ShapeDtypeStruct((R, D), x.dtype),
        scratch_types=[pltpu.VMEM((K,), I32), pltpu.VMEM((K, D), x.dtype), pltpu.SemaphoreType.DMA],
        name="gather_rows_sc",
    )
    def run(x_hbm, idx_hbm, o_hbm, idx_v, rows_v, sem):
        base = (lax.axis_index("s") * SC_CORES + lax.axis_index("c")) * per_w

        @pl.loop(0, per_w // K)
        def _(c):
            off = base + c * K
            pltpu.sync_copy(idx_hbm.at[pl.ds(off, K)], idx_v)
            pltpu.async_copy(x_hbm.at[idx_v], rows_v, sem).wait()
            pltpu.sync_copy(rows_v, o_hbm.at[pl.ds(off, K)])

    return run(x, idx_flat)


def _ffn_kernel(x_ref, wg_ref, wu_ref, wd_ref, wr_ref, lo_ref, hi_ref):
    e = pl.program_id(0)
    xb = x_ref[...].astype(BF16)
    aff = _router_aff(xb, wr_ref[...])
    lane = lax.broadcasted_iota(I32, aff.shape, 1)
    val = jnp.sum(jnp.where(lane == e, aff, 0.0), axis=1, keepdims=True)
    gate = jnp.dot(xb, wg_ref[...], preferred_element_type=F32)
    up = jnp.dot(xb, wu_ref[...], preferred_element_type=F32)
    hid = gate * _sigmoid(gate) * up
    y = jnp.dot(hid.astype(BF16), wd_ref[...], preferred_element_type=F32) * val
    half = y.shape[1] // 2
    lo_ref[...] = y[:, :half]
    hi_ref[...] = y[:, half:]


def _ffn(xe, wg, wu, wd, wr, cap, tc):
    R, D = xe.shape
    E, _, F = wg.shape
    nb = cap // tc
    half = jax.ShapeDtypeStruct((R, D // 2), F32)
    return pl.pallas_call(
        _ffn_kernel,
        out_shape=(half, half),
        grid=(E, nb),
        in_specs=[pl.BlockSpec((tc, D), lambda e, c: (e * nb + c, 0)),
                  pl.BlockSpec((None, D, F), lambda e, c: (e, 0, 0)),
                  pl.BlockSpec((None, D, F), lambda e, c: (e, 0, 0)),
                  pl.BlockSpec((None, F, D), lambda e, c: (e, 0, 0)),
                  pl.BlockSpec(wr.shape, lambda e, c: (0, 0))],
        out_specs=(pl.BlockSpec((tc, D // 2), lambda e, c: (e * nb + c, 0)),
                   pl.BlockSpec((tc, D // 2), lambda e, c: (e * nb + c, 0))),
        compiler_params=_cparams(("parallel", "arbitrary")),
        name="expert_swiglu",
    )(xe, wg, wu, wd, wr)


def _scatter_add(n_tokens, ye_lo, ye_hi, idx_flat, cap, K):
    R, H = ye_lo.shape
    per_s = cap // SC_SUBCORES
    mesh = plsc.VectorSubcoreMesh(core_axis_name="c", subcore_axis_name="s")
    lanes = SC_LANES

    @functools.partial(
        pl.kernel, mesh=mesh, out_type=(),
        scratch_types=[pltpu.VMEM((K,), I32), pltpu.VMEM((K, H), F32), pltpu.VMEM((K, H), F32),
                       pltpu.SemaphoreType.DMA],
        name="scatter_add_sc",
    )
    def run(ylo_hbm, yhi_hbm, yelo_hbm, yehi_hbm, idx_hbm, idx_v, acc_v, add_v, sem):
        s = lax.axis_index("s")

        def half(y_hbm, ye_hbm):
            @pl.loop(0, N_EXPERTS)
            def _(e):
                @pl.loop(0, per_s // K)
                def _(ch):
                    off = e * cap + s * per_s + ch * K
                    pltpu.sync_copy(idx_hbm.at[pl.ds(off, K)], idx_v)
                    pltpu.sync_copy(ye_hbm.at[pl.ds(off, K)], add_v)
                    pltpu.async_copy(y_hbm.at[idx_v], acc_v, sem).wait()

                    @pl.loop(0, K)
                    def _(r):
                        for j in range(H // lanes):
                            sl = pl.ds(j * lanes, lanes)
                            acc_v[r, sl] = acc_v[r, sl] + add_v[r, sl]

                    pltpu.async_copy(acc_v, y_hbm.at[idx_v], sem).wait()

                plsc.subcore_barrier()

        @pl.when(lax.axis_index("c") == 0)
        def _():
            half(ylo_hbm, yelo_hbm)

        @pl.when(lax.axis_index("c") == 1)
        def _():
            half(yhi_hbm, yehi_hbm)

    y_lo = jax.new_ref(jnp.zeros((n_tokens, H), F32))
    y_hi = jax.new_ref(jnp.zeros((n_tokens, H), F32))
    run(y_lo, y_hi, ye_lo, ye_hi, idx_flat)
    return y_lo[...], y_hi[...]


def _ln2_kernel(x1_ref, ylo_ref, yhi_ref, g_ref, b_ref, x2_ref, xb_ref, *, alpha):
    y = jnp.concatenate([ylo_ref[...], yhi_ref[...]], axis=1)
    x2 = _layer_norm(alpha * x1_ref[...] + y, g_ref[...], b_ref[...])
    x2_ref[...] = x2
    xb_ref[...] = x2.astype(BF16)


def _ln2(x1, y_lo, y_hi, g, b, tm, alpha):
    N, D = x1.shape
    row = lambda i: (i, 0)
    full = lambda i: (0, 0)
    return pl.pallas_call(
        functools.partial(_ln2_kernel, alpha=alpha),
        out_shape=(jax.ShapeDtypeStruct((N, D), F32), jax.ShapeDtypeStruct((N, D), BF16)),
        grid=(N // tm,),
        in_specs=[pl.BlockSpec((tm, D), row), pl.BlockSpec((tm, D // 2), row), pl.BlockSpec((tm, D // 2), row),
                  pl.BlockSpec(g.shape, full), pl.BlockSpec(b.shape, full)],
        out_specs=(pl.BlockSpec((tm, D), row), pl.BlockSpec((tm, D), row)),
        compiler_params=_cparams(("parallel",)),
        name="residual_ln2",
    )(x1, y_lo, y_hi, g, b)


def _tiles(B, S):
    N = B * S
    tm = 1024 if S % 1024 == 0 else S
    return dict(tm_proj=tm, tq_b=min(256, S), tq_c=512, tm_merge=512,
                tc=min(512, CAPACITY_FACTOR * N // N_EXPERTS))


def _prep_layer(l, w_in, pool_mix, pool_scale, sink_logit, w_branch, w_out, ln1_g, ln1_b, w_router,
                w_gate_e, w_up_e, w_down_e, ln2_g, ln2_b):
    D = w_in.shape[1]
    a0, qb0 = 0, A_WIDTH
    kb0 = qb0 + B_Q_HEADS * B_HEAD_DIM
    vb0 = kb0 + B_KV_HEADS * B_HEAD_DIM
    qc0 = vb0 + B_KV_HEADS * B_HEAD_DIM
    kc0 = qc0 + len(C_CONFIGS) * C_GROUP_W
    vc0 = kc0 + len(C_CONFIGS) * C_GROUP_W
    zg0 = vc0 + len(C_CONFIGS) * C_GROUP_W
    spread = np.concatenate([h * C_HEAD_DIM + _spread_head_perm(C_HEAD_DIM) for h in range(C_HEADS)])
    wl = w_in[l]
    wr = jnp.zeros((D, LANES), F32).at[:, :N_EXPERTS].set(w_router[l])
    return dict(
        w_qkb=wl[:, qb0:vb0].astype(BF16),
        w_avb=jnp.concatenate([wl[:, a0:qb0], wl[:, vb0:qc0]], axis=1).astype(BF16),
        w_c=[jnp.concatenate([wl[:, c0 + g * C_GROUP_W:c0 + (g + 1) * C_GROUP_W][:, cols]
                              for c0, cols in ((qc0, spread), (kc0, spread), (vc0, np.arange(C_GROUP_W)))],
                             axis=1).astype(BF16) for g in range(len(C_CONFIGS))],
        w_zg=(0.5 * wl[:, zg0:]).astype(BF16),
        pmix=pool_mix[l].astype(BF16), pscale=pool_scale[l][None, :], sink=sink_logit[l],
        wbr=(0.5 * w_branch[l]).astype(BF16), wout=w_out[l].astype(BF16),
        g1=ln1_g[l][None, :], b1=ln1_b[l][None, :], wr=wr.astype(BF16),
        wg=w_gate_e[l].astype(BF16), wu=w_up_e[l].astype(BF16), wd=w_down_e[l].astype(BF16),
        g2=ln2_g[l][None, :], b2=ln2_b[l][None, :],
    )


def _layer(x, xb, p, B, S, alpha, rot_b, rot_c):
    N, D = x.shape
    t = _tiles(B, S)
    cap = CAPACITY_FACTOR * N // N_EXPERTS
    qkb = _proj(xb, p["w_qkb"], S, t["tm_proj"], p["w_qkb"].shape[1], BF16, rot_b)
    avb = _proj(xb, p["w_avb"], S, t["tm_proj"], p["w_avb"].shape[1], F32)
    zg = _proj(xb, p["w_zg"], S, t["tm_proj"], 1024, BF16)
    o_b = _attn_b(qkb, avb, p["sink"], B, S, t["tq_b"])
    o_cs, lses = [], []
    for g, (_, d) in enumerate(C_CONFIGS):
        qkv = _proj_fold(xb, p["w_c"][g], B, S, d, t["tm_proj"], 2 * C_GROUP_W, rot_c)
        o, lse = _attn_c(qkv, g, B, S, t["tq_c"])
        o_cs.append(o)
        lses.append(lse)
    x1, aff_t = _merge(x, avb, o_b, o_cs, lses, zg, p["pmix"], p["pscale"], p["wbr"], p["wout"], p["g1"], p["b1"],
                       p["wr"], S, t["tm_merge"], alpha)
    idx_flat = _select(aff_t, cap).reshape(-1)
    xe = _gather(x1, idx_flat, SC_GATHER_ROWS)
    ye_lo, ye_hi = _ffn(xe, p["wg"], p["wu"], p["wd"], p["wr"], cap, t["tc"])
    y_lo, y_hi = _scatter_add(N, ye_lo, ye_hi, idx_flat, cap, SC_GATHER_ROWS)
    return _ln2(x1, y_lo, y_hi, p["g2"], p["b2"], t["tm_proj"], alpha)


def _trunk(x, layers, alpha):
    B, S, D = x.shape
    rot_b = _rot_tables(S, B_HEAD_DIM)
    rot_c = _rot_tables_spread(S, C_HEAD_DIM)
    xf = x.reshape(B * S, D)
    xb = xf.astype(BF16)
    for p in layers:
        xf, xb = _layer(xf, xb, p, B, S, alpha, rot_b, rot_c)
    return xf.reshape(B, S, D)


def kernel(x_prompt, x_sample, w_in, pool_mix, pool_scale, sink_logit, w_branch, w_out, ln1_g, ln1_b, w_router,
           w_gate_e, w_up_e, w_down_e, ln2_g, ln2_b):
    depth = w_in.shape[0]
    alpha = (2 * depth) ** 0.25
    layers = [_prep_layer(l, w_in, pool_mix, pool_scale, sink_logit, w_branch, w_out, ln1_g, ln1_b, w_router,
                          w_gate_e, w_up_e, w_down_e, ln2_g, ln2_b) for l in range(depth)]
    return (_trunk(x_prompt, layers, alpha), _trunk(x_sample, layers, alpha))
```

```python
import functools

import numpy as np
import jax
import jax.numpy as jnp
from jax import lax
from jax.experimental import pallas as pl
from jax.experimental.pallas import tpu as pltpu
from jax.experimental.pallas import tpu_sc as plsc

F32 = jnp.float32
BF16 = jnp.bfloat16
I32 = jnp.int32

POOL_WINDOWS = (2, 4, 8, 16)
POOL_CH = 128
A_WIDTH = 512
B_Q_HEADS = 8
B_KV_HEADS = 2
B_HEAD_DIM = 64
B_HALF_WINDOW = 128
C_CONFIGS = ((128, 1), (512, 4), (2048, 16))
C_HEADS = 4
C_HEAD_DIM = 128
C_GROUP_W = C_HEADS * C_HEAD_DIM
N_EXPERTS = 16
CAPACITY_FACTOR = 2
ROPE_THETA = 500000.0
ROT_DIV = 4
LN_EPS = 1e-5
NEG_INF = -1e30
LOG2E = 1.4426950408889634
LN2 = 0.6931471805599453

LANES = 128
SUBLANES = 8
VMEM_LIMIT = 56 * 1024 * 1024
SC_CORES = 2
SC_SUBCORES = 16
SC_LANES = 16
SC_GATHER_ROWS = 64
MXU_WIDTH = 256
ATTN_STRIP = 32
ATTN_UNROLL = 8
PERM_BLOCK = MXU_WIDTH


def _cparams(sem):
    return pltpu.CompilerParams(dimension_semantics=sem, vmem_limit_bytes=VMEM_LIMIT)


def _col_chunks(width):
    return [(c, min(c + MXU_WIDTH, width)) for c in range(0, width, MXU_WIDTH)]


def _rotary(acc, c, s1, s2, rot_half):
    outs = []
    for g in range(acc.shape[1] // LANES):
        a = acc[:, g * LANES:(g + 1) * LANES]
        if 2 * rot_half == LANES:
            outs.append(a * c + pltpu.roll(a, rot_half, 1) * s1)
        else:
            outs.append(a * c + pltpu.roll(a, LANES - rot_half, 1) * s1 + pltpu.roll(a, rot_half, 1) * s2)
    return outs[0] if len(outs) == 1 else jnp.concatenate(outs, axis=1)


def _proj_main_kernel(x_ref, w_ref, cos_ref, s1_ref, s2_ref, qk_ref, av_ref, zg_ref, *, rot_half):
    x = x_ref[...]
    base = 0
    for o_ref, rotate in ((qk_ref, True), (av_ref, False), (zg_ref, False)):
        for c0, c1 in _col_chunks(o_ref.shape[1]):
            acc = jnp.dot(x, w_ref[:, base + c0:base + c1], preferred_element_type=F32)
            if rotate:
                acc = _rotary(acc, cos_ref[...], s1_ref[...], s2_ref[...], rot_half)
            o_ref[:, c0:c1] = acc.astype(o_ref.dtype)
        base += o_ref.shape[1]


def _proj_main(x, w, widths, S, tm, rot):
    N, K = x.shape
    rot_half, tabs = rot
    spb = S // tm
    row = lambda i: (i, 0)
    tab_spec = pl.BlockSpec((tm, LANES), lambda i: (i % spb, 0))
    dtypes = (BF16, F32, BF16)
    return pl.pallas_call(
        functools.partial(_proj_main_kernel, rot_half=rot_half),
        out_shape=tuple(jax.ShapeDtypeStruct((N, c), dt) for c, dt in zip(widths, dtypes)),
        grid=(N // tm,),
        in_specs=[pl.BlockSpec((tm, K), row), pl.BlockSpec(w.shape, lambda i: (0, 0)), tab_spec, tab_spec, tab_spec],
        out_specs=tuple(pl.BlockSpec((tm, c), row) for c in widths),
        compiler_params=_cparams(("parallel",)),
        name="proj_main",
    )(x, w, *tabs)


def _fold_perm(d, inverse=False):
    i = lax.broadcasted_iota(I32, (PERM_BLOCK, PERM_BLOCK), 1 if inverse else 0)
    j = lax.broadcasted_iota(I32, (PERM_BLOCK, PERM_BLOCK), 0 if inverse else 1)
    per = PERM_BLOCK // d
    return jnp.where(j == (i % per) * d + i // per, 1.0, 0.0).astype(BF16)


def _proj_fold_kernel(x_ref, w_ref, cos_ref, s1_ref, s2_ref, o_ref, *scr, rot_half, d, rot_cols):
    x = x_ref[...]
    tm = x.shape[0]
    for c0, c1 in _col_chunks(w_ref.shape[1]):
        acc = jnp.dot(x, w_ref[:, c0:c1], preferred_element_type=F32)
        if c0 < rot_cols:
            acc = _rotary(acc, cos_ref[...], s1_ref[...], s2_ref[...], rot_half)
        if d == 1:
            o_ref[0, :, c0:c1] = acc.astype(BF16)
            continue
        for g in range(c0 // LANES, c1 // LANES):
            scr[0][g] = acc[:, g * LANES - c0:(g + 1) * LANES - c0]
            for rr in range(d):
                o_ref[rr, :, g * LANES:(g + 1) * LANES] = scr[0][g, pl.ds(rr, tm // d, stride=d), :].astype(BF16)


def _proj_fold(x, w, B, S, d, tm, rot_cols, rot):
    N, K = x.shape
    C = w.shape[1]
    rot_half, tabs = rot
    spb = S // tm
    tab_spec = pl.BlockSpec((tm, LANES), lambda i: (i % spb, 0))
    return pl.pallas_call(
        functools.partial(_proj_fold_kernel, rot_half=rot_half, d=d, rot_cols=rot_cols),
        out_shape=jax.ShapeDtypeStruct((B * d, S // d, C), BF16),
        grid=(N // tm,),
        in_specs=[pl.BlockSpec((tm, K), lambda i: (i, 0)), pl.BlockSpec((K, C), lambda i: (0, 0)),
                  tab_spec, tab_spec, tab_spec],
        out_specs=pl.BlockSpec((d, tm // d, C), lambda i: (i // spb, i % spb, 0)),
        scratch_shapes=[pltpu.VMEM((C // LANES, tm, LANES), F32)] if d > 1 else [],
        compiler_params=_cparams(("parallel",)),
        name=f"proj_fold_{d}",
    )(x, w, *tabs)


def _rot_tables(S, head_dim):
    rot = head_dim // ROT_DIV
    half = rot // 2
    inv = 1.0 / (ROPE_THETA ** (jnp.arange(0, rot, 2, dtype=F32) / rot))
    ang = jnp.arange(S, dtype=F32)[:, None] * inv[None, :]
    cos, sin = jnp.cos(ang), jnp.sin(ang)
    d = np.arange(LANES) % head_dim
    first = d < half
    second = (d >= half) & (d < rot)
    src = np.where(first, d, np.where(second, d - half, 0))
    cos_t = jnp.where(first | second, cos[:, src], 1.0)
    s1_t = jnp.where(first, -sin[:, src], 0.0)
    s2_t = jnp.where(second, sin[:, src], 0.0)
    return half, (cos_t, s1_t, s2_t)


def _spread_head_perm(head_dim):
    half = head_dim // ROT_DIV // 2
    mid = LANES // 2
    return np.concatenate([np.arange(0, half), np.arange(2 * half, mid + half), np.arange(half, 2 * half),
                           np.arange(mid + half, head_dim)])


def _rot_tables_spread(S, head_dim):
    half, (cos_t, s1_t, s2_t) = _rot_tables(S, head_dim)
    perm = _spread_head_perm(head_dim)
    return LANES // 2, (cos_t[:, perm], (s1_t + s2_t)[:, perm], s2_t)


def _band_bias(bias_ref, W, q0, L, sub):
    tq, nk = bias_ref.shape
    row = lax.broadcasted_iota(I32, (tq, nk), 0)
    col = lax.broadcasted_iota(I32, (tq, nk), 1)
    first = (row // sub) * sub
    kpos = q0 - W + first + col
    valid = (jnp.abs(col - W - (row - first)) <= W) & (kpos >= 0) & (kpos < L)
    bias_ref[...] = jnp.where(valid, 0.0, NEG_INF)


def _softmax_strips(s_ref, p_ref, den_ref, max_ref, bias_ref, *, unroll, sink_fn=None):
    rows, nk = s_ref.shape
    tq = bias_ref.shape[0]

    def strip(t):
        r0 = pl.multiple_of(t * ATTN_STRIP, ATTN_STRIP)
        rows_sl = pl.ds(r0, ATTN_STRIP)
        s = s_ref[rows_sl, :] + bias_ref[pl.ds(pl.multiple_of(r0 % tq, ATTN_STRIP), ATTN_STRIP), :]
        return r0, rows_sl, s

    def row_max(t, carry):
        r0, rows_sl, s = strip(t)
        m = jnp.max(s, axis=1, keepdims=True)
        if sink_fn is not None:
            m = jnp.maximum(m, sink_fn(r0 // tq) * LOG2E)
        max_ref[rows_sl, :] = jnp.broadcast_to(m, (ATTN_STRIP, LANES))
        return carry

    def numerators(t, carry):
        r0, rows_sl, s = strip(t)
        m = max_ref[rows_sl, :]
        p = jnp.exp2(s - jnp.concatenate([m] * (nk // LANES), axis=1))
        den = jnp.broadcast_to(jnp.sum(p, axis=1, keepdims=True), (ATTN_STRIP, LANES))
        if sink_fn is not None:
            den = den + jnp.exp2(sink_fn(r0 // tq) * LOG2E - m)
        p_ref[rows_sl, :] = p.astype(BF16)
        den_ref[rows_sl, :] = den
        return carry

    lax.fori_loop(0, rows // ATTN_STRIP, row_max, 0, unroll=unroll)
    lax.fori_loop(0, rows // ATTN_STRIP, numerators, 0, unroll=unroll)


def _attn_b_kernel(sink_ref, q_ref, kp_ref, km_ref, kn_ref, vp_ref, vm_ref, vn_ref, o_ref,
                   s_ref, p_ref, den_ref, max_ref, bias_ref, *, S, TQ):
    W = B_HALF_WINDOW
    G = B_Q_HEADS // B_KV_HEADS
    sub, span = W, 3 * W
    nb = TQ // sub
    q0 = pl.program_id(1) * TQ
    _band_bias(bias_ref, W, q0, S, sub)
    k_all = jnp.concatenate([kp_ref[...], km_ref[...], kn_ref[...]], axis=0)
    v_all = jnp.concatenate([vp_ref[...], vm_ref[...], vn_ref[...]], axis=0).astype(BF16)
    q = q_ref[...]
    outs = []
    for j in range(B_KV_HEADS):
        kj = k_all[:, j * B_HEAD_DIM:(j + 1) * B_HEAD_DIM]
        vj = v_all[:, j * B_HEAD_DIM:(j + 1) * B_HEAD_DIM]
        for b in range(nb):
            qs = jnp.concatenate([q[b * sub:(b + 1) * sub, (G * j + g) * B_HEAD_DIM:(G * j + g + 1) * B_HEAD_DIM]
                                  for g in range(G)], axis=0)
            s = lax.dot_general(qs, kj[b * sub:b * sub + span], (((1,), (1,)), ((), ())),
                                preferred_element_type=F32) * (B_HEAD_DIM ** -0.5 * LOG2E)
            for g in range(G):
                s_ref[j, g * TQ + b * sub:g * TQ + (b + 1) * sub, :] = s[g * sub:(g + 1) * sub]
        _softmax_strips(s_ref.at[j], p_ref.at[j], den_ref.at[j], max_ref.at[j], bias_ref, unroll=ATTN_UNROLL,
                        sink_fn=lambda g, j=j: sink_ref[G * j + g])
        pv = []
        for b in range(nb):
            pb = jnp.concatenate([p_ref[j, g * TQ + b * sub:g * TQ + (b + 1) * sub, :] for g in range(G)], axis=0)
            pv.append(jnp.dot(pb, vj[b * sub:b * sub + span], preferred_element_type=F32))
        for g in range(G):
            o = jnp.concatenate([pv[b][g * sub:(g + 1) * sub] for b in range(nb)], axis=0)
            outs.append(o / den_ref[j, g * TQ:(g + 1) * TQ, 0:1])
    o_ref[...] = jnp.concatenate(outs, axis=1).astype(o_ref.dtype)


def _attn_b(qk, av, sink, B, S, TQ):
    N = B * S
    W = B_HALF_WINDOW
    nq = S // TQ
    r = TQ // W
    kcol = (B_Q_HEADS * B_HEAD_DIM) // LANES
    last = N // W - 1
    rows = (B_Q_HEADS // B_KV_HEADS) * TQ

    def prev(b, i):
        return (jnp.maximum((b * nq + i) * r - 1, 0), kcol)

    def main(b, i):
        return (b * nq + i, kcol)

    def nxt(b, i):
        return (jnp.minimum((b * nq + i + 1) * r, last), kcol)

    return pl.pallas_call(
        functools.partial(_attn_b_kernel, S=S, TQ=TQ),
        out_shape=jax.ShapeDtypeStruct((N, B_Q_HEADS * B_HEAD_DIM), BF16),
        grid=(B, nq),
        in_specs=[pl.BlockSpec(memory_space=pltpu.SMEM),
                  pl.BlockSpec((TQ, B_Q_HEADS * B_HEAD_DIM), lambda b, i: (b * nq + i, 0)),
                  pl.BlockSpec((W, LANES), prev), pl.BlockSpec((TQ, LANES), main), pl.BlockSpec((W, LANES), nxt),
                  pl.BlockSpec((W, LANES), prev), pl.BlockSpec((TQ, LANES), main), pl.BlockSpec((W, LANES), nxt)],
        out_specs=pl.BlockSpec((TQ, B_Q_HEADS * B_HEAD_DIM), lambda b, i: (b * nq + i, 0)),
        scratch_shapes=[pltpu.VMEM((B_KV_HEADS, rows, 3 * W), F32),
                        pltpu.VMEM((B_KV_HEADS, rows, 3 * W), BF16),
                        pltpu.VMEM((B_KV_HEADS, rows, LANES), F32), pltpu.VMEM((B_KV_HEADS, rows, LANES), F32),
                        pltpu.VMEM((TQ, 3 * W), F32)],
        compiler_params=_cparams(("parallel", "arbitrary")),
        name="attn_window",
    )(sink, qk, qk, qk, qk, av, av, av)


def _attn_c_kernel(q_ref, kp_ref, km_ref, kn_ref, vp_ref, vm_ref, vn_ref, o_ref, lse_ref,
                   s_ref, p_ref, den_ref, max_ref, bias_ref, *, M, TQ, W):
    sub, span = 2 * W, 4 * W
    q0 = pl.program_id(1) * TQ
    _band_bias(bias_ref, W, q0, M, sub)
    k_all = jnp.concatenate([kp_ref[...], km_ref[...], kn_ref[...]], axis=0)
    v_all = jnp.concatenate([vp_ref[...], vm_ref[...], vn_ref[...]], axis=0)
    q = q_ref[...]
    lane = lax.broadcasted_iota(I32, (TQ, LANES), 1)
    lse_tile = jnp.zeros((TQ, LANES), F32)
    outs = []
    for h in range(C_HEADS):
        sl = slice(h * C_HEAD_DIM, (h + 1) * C_HEAD_DIM)
        for b in range(TQ // sub):
            s_ref[h, b * sub:(b + 1) * sub, :] = lax.dot_general(
                q[b * sub:(b + 1) * sub, sl], k_all[b * sub:b * sub + span, sl], (((1,), (1,)), ((), ())),
                preferred_element_type=F32) * (C_HEAD_DIM ** -0.5 * LOG2E)
        _softmax_strips(s_ref.at[h], p_ref.at[h], den_ref.at[h], max_ref.at[h], bias_ref, unroll=True)
        den = den_ref[h]
        pv = [jnp.dot(p_ref[h, b * sub:(b + 1) * sub, :], v_all[b * sub:b * sub + span, sl],
                      preferred_element_type=F32) for b in range(TQ // sub)]
        outs.append(jnp.concatenate(pv, axis=0) / den[:, 0:1])
        lse_tile = jnp.where(lane == h, max_ref[h] * LN2 + jnp.log(den), lse_tile)
    o_ref[...] = jnp.concatenate(outs, axis=1).astype(o_ref.dtype)
    lse_ref[...] = lse_tile


def _attn_c(qkv, g, B, S, TQ):
    window, d = C_CONFIGS[g]
    W = window // (2 * d)
    M = S // d
    TQ = min(TQ, M)
    r = TQ // W
    GW = C_GROUP_W
    last = M // W - 1

    def mk(col):
        return [pl.BlockSpec((None, W, GW), lambda b, i: (b, jnp.maximum(i * r - 1, 0), col)),
                pl.BlockSpec((None, TQ, GW), lambda b, i: (b, i, col)),
                pl.BlockSpec((None, W, GW), lambda b, i: (b, jnp.minimum((i + 1) * r, last), col))]

    return pl.pallas_call(
        functools.partial(_attn_c_kernel, M=M, TQ=TQ, W=W),
        out_shape=(jax.ShapeDtypeStruct((B * d, M, GW), BF16), jax.ShapeDtypeStruct((B * d, M, LANES), F32)),
        grid=(B * d, M // TQ),
        in_specs=[pl.BlockSpec((None, TQ, GW), lambda b, i: (b, i, 0))] + mk(1) + mk(2),
        out_specs=(pl.BlockSpec((None, TQ, GW), lambda b, i: (b, i, 0)),
                   pl.BlockSpec((None, TQ, LANES), lambda b, i: (b, i, 0))),
        scratch_shapes=[pltpu.VMEM((C_HEADS, TQ, 4 * W), F32), pltpu.VMEM((C_HEADS, TQ, 4 * W), BF16),
                        pltpu.VMEM((C_HEADS, TQ, LANES), F32), pltpu.VMEM((C_HEADS, TQ, LANES), F32),
                        pltpu.VMEM((TQ, 4 * W), F32)],
        compiler_params=_cparams(("parallel", "arbitrary")),
        name=f"attn_dilated_{d}",
    )(qkv, qkv, qkv, qkv, qkv, qkv, qkv)


def _sigmoid(z):
    return 0.5 * jnp.tanh(0.5 * z) + 0.5


def _layer_norm(h, g, b):
    mu = jnp.mean(h, axis=-1, keepdims=True)
    var = jnp.mean(jnp.square(h - mu), axis=-1, keepdims=True)
    return (h - mu) * lax.rsqrt(var + LN_EPS) * g + b


def _router_aff(xb, wr):
    logits = jnp.dot(xb, wr, preferred_element_type=F32)
    lane = lax.broadcasted_iota(I32, logits.shape, 1)
    logits = jnp.where(lane < N_EXPERTS, logits, NEG_INF)
    e = jnp.exp(logits - jnp.max(logits, axis=1, keepdims=True))
    return e / jnp.sum(e, axis=1, keepdims=True)


def _unfold(ref, scr_ref):
    d, rows, _ = ref.shape
    if d == 1:
        return ref[0]
    for rr in range(d):
        scr_ref[pl.ds(rr, rows, stride=d), :] = ref[rr]
    return scr_ref[...]


def _unfold_bf16(ref):
    d, rows, _ = ref.shape
    if d == 1:
        return ref[0].astype(F32)
    per = PERM_BLOCK // d
    perm = _fold_perm(d, inverse=True)
    outs = []
    for blk in range(d * rows // PERM_BLOCK):
        stacked = jnp.concatenate([ref[rr, blk * per:(blk + 1) * per, :] for rr in range(d)], axis=0)
        outs.append(jnp.dot(perm, stacked, preferred_element_type=F32))
    return outs[0] if len(outs) == 1 else jnp.concatenate(outs, axis=0)


def _merge_kernel(x_ref, a_ref, ap_ref, an_ref, ob_ref, oc0_ref, oc1_ref, oc2_ref, l0_ref, l1_ref, l2_ref,
                  zg_ref, pmix_ref, pscale_ref, wbr_ref, wout_ref, g_ref, b_ref, wr_ref,
                  x1_ref, aff_ref, sl1_ref, sl2_ref, *, S, tm, alpha):
    HW = max(POOL_WINDOWS) // 2
    L = tm + 2 * HW
    pos0 = (pl.program_id(0) * tm) % S
    xa = jnp.concatenate([ap_ref[...], a_ref[...], an_ref[...]], axis=0)
    xpos = pos0 - HW + lax.broadcasted_iota(I32, (L, 1), 0)
    xa = jnp.where((xpos >= 0) & (xpos < S), xa, 0.0)
    sums = {}
    t, w = xa, 1
    while w < max(POOL_WINDOWS):
        t = t + pltpu.roll(t, L - w, 0)
        w *= 2
        sums[w] = t
    pos = pos0 + lax.broadcasted_iota(I32, (tm, 1), 0)
    mixed = []
    for g, w in enumerate(POOL_WINDOWS):
        cs = slice(g * POOL_CH, (g + 1) * POOL_CH)
        off = HW - w // 2
        sw = sums[w][:, cs]
        if off:
            sw = pltpu.roll(sw, L - off, 0)
        sw = sw[:tm]
        cnt = (jnp.clip(pos + w // 2, 0, S) - jnp.clip(pos - w // 2, 0, S)).astype(F32)
        pooled = sw / cnt - a_ref[:, cs]
        mixed.append(jnp.dot(pooled.astype(BF16), pmix_ref[g], preferred_element_type=F32) * pscale_ref[:, cs])
    o_a = jnp.concatenate(mixed, axis=1)
    ls = [_unfold(l0_ref, None), _unfold(l1_ref, sl1_ref), _unfold(l2_ref, sl2_ref)]
    mx = jnp.maximum(jnp.maximum(ls[0], ls[1]), ls[2])
    es = [jnp.exp(l - mx) for l in ls]
    tot = es[0] + es[1] + es[2]
    ocs = [_unfold_bf16(oc0_ref), _unfold_bf16(oc1_ref), _unfold_bf16(oc2_ref)]
    pieces = []
    for h in range(C_HEADS):
        sl = slice(h * C_HEAD_DIM, (h + 1) * C_HEAD_DIM)
        acc = None
        for gi in range(3):
            term = (es[gi] / tot)[:, h:h + 1] * ocs[gi][:, sl].astype(F32)
            acc = term if acc is None else acc + term
        pieces.append(acc)
    o_c = jnp.concatenate(pieces, axis=1)
    branches = [o_a.astype(BF16), ob_ref[...], o_c.astype(BF16)]
    D = x_ref.shape[1]
    merged = []
    for c0, c1 in _col_chunks(D):
        acc = None
        for k in range(3):
            gate2 = jnp.tanh(zg_ref[:, k * D + c0:k * D + c1].astype(F32)) + 1.0
            term = gate2 * jnp.dot(branches[k], wbr_ref[k, :, c0:c1], preferred_element_type=F32)
            acc = term if acc is None else acc + term
        merged.append(acc.astype(BF16))
    merged = jnp.concatenate(merged, axis=1)
    pre = [alpha * x_ref[:, c0:c1] + jnp.dot(merged, wout_ref[:, c0:c1], preferred_element_type=F32)
           for c0, c1 in _col_chunks(D)]
    x1 = _layer_norm(jnp.concatenate(pre, axis=1), g_ref[...], b_ref[...])
    x1_ref[...] = x1
    aff = _router_aff(x1.astype(BF16), wr_ref[...])
    aff_ref[...] = aff.T[:N_EXPERTS]


def _merge(x, av, o_b, o_cs, lses, vz, pmix, pscale, wbr, wout, g, b, wr, S, tm, alpha):
    N, D = x.shape
    HW = max(POOL_WINDOWS) // 2
    r = tm // HW
    last = N // HW - 1
    row = lambda i: (i, 0)
    full2 = lambda i: (0, 0)
    full3 = lambda i: (0, 0, 0)
    spb = S // tm
    dils = [d for _, d in C_CONFIGS]

    def folded(width):
        return [pl.BlockSpec((d, tm // d, width), lambda i: (i // spb, i % spb, 0)) for d in dils]

    in_specs = [
        pl.BlockSpec((tm, D), row),
        pl.BlockSpec((tm, A_WIDTH), row),
        pl.BlockSpec((HW, A_WIDTH), lambda i: (jnp.maximum(i * r - 1, 0), 0)),
        pl.BlockSpec((HW, A_WIDTH), lambda i: (jnp.minimum((i + 1) * r, last), 0)),
        pl.BlockSpec((tm, B_Q_HEADS * B_HEAD_DIM), row),
        *folded(C_GROUP_W),
        *folded(LANES),
        pl.BlockSpec((tm, 3 * D), row),
        pl.BlockSpec(pmix.shape, full3),
        pl.BlockSpec(pscale.shape, full2),
        pl.BlockSpec(wbr.shape, full3),
        pl.BlockSpec(wout.shape, full2),
        pl.BlockSpec(g.shape, full2), pl.BlockSpec(b.shape, full2),
        pl.BlockSpec(wr.shape, full2),
    ]
    return pl.pallas_call(
        functools.partial(_merge_kernel, S=S, tm=tm, alpha=alpha),
        out_shape=(jax.ShapeDtypeStruct((N, D), F32), jax.ShapeDtypeStruct((N_EXPERTS, N), F32)),
        grid=(N // tm,),
        in_specs=in_specs,
        out_specs=(pl.BlockSpec((tm, D), row), pl.BlockSpec((N_EXPERTS, tm), lambda i: (0, i))),
        scratch_shapes=[pltpu.VMEM((tm, LANES), F32)] * 2,
        compiler_params=_cparams(("parallel",)),
        name="merge_ln1_router",
    )(x, av, av, av, o_b, *o_cs, *lses, vz, pmix, pscale, wbr, wout, g, b, wr)


def _count(mask):
    return jnp.sum(jnp.sum(jnp.where(mask, 1.0, 0.0), axis=1), axis=1, keepdims=True)


def _thr_kernel(aff_ref, t_ref, need_ref, *, cap):
    bits = pltpu.bitcast(aff_ref[...], I32)

    def body(k, t):
        cand = t | jnp.left_shift(jnp.int32(1), 30 - k)
        return jnp.where(_count(bits >= cand[:, :, None]) >= cap, cand, t)

    t = lax.fori_loop(0, 31, body, jnp.zeros((N_EXPERTS, 1), I32))
    need = cap - _count(bits > t[:, :, None])
    t_ref[...] = jnp.broadcast_to(t, t_ref.shape)
    need_ref[...] = jnp.broadcast_to(need.astype(I32), need_ref.shape)


def _pos_kernel(t_ref, need_ref, aff_ref, clt_ref, ext_ref, lo_ref, hi_ref, *, NC):
    e = pl.program_id(0)
    bits = pltpu.bitcast(aff_ref[...], I32)
    t = t_ref[e]
    need = need_ref[e].astype(F32)
    upper = jnp.where(lax.broadcasted_iota(I32, (LANES, LANES), 0) <= lax.broadcasted_iota(I32, (LANES, LANES), 1),
                      1.0, 0.0).astype(BF16)
    ones = jnp.ones((LANES, LANES), BF16)
    before = jnp.where(lax.broadcasted_iota(I32, (NC, NC), 1) < lax.broadcasted_iota(I32, (NC, NC), 0),
                       1.0, 0.0).astype(BF16)

    def cums(xf):
        xb = xf.astype(BF16)
        local = jnp.dot(xb, upper, preferred_element_type=F32)
        tot = jnp.dot(xb, ones, preferred_element_type=F32)
        offs = jnp.dot(before, tot.astype(BF16), preferred_element_type=F32)
        return local, tot, offs

    eq = jnp.where(bits == t, 1.0, 0.0)
    local, _, offs = cums(eq)
    sel = (bits > t) | ((bits == t) & (local + offs - eq < need))
    local, tot, offs = cums(jnp.where(sel, 1.0, 0.0))
    lo_ref[...] = offs
    hi_ref[...] = offs + tot
    clt_ref[...] = local.T
    chunk = lax.broadcasted_iota(I32, (NC, LANES), 0)
    lane = lax.broadcasted_iota(I32, (NC, LANES), 1)
    offs_i = offs.astype(I32)
    ext = jnp.where(lane == 0, offs_i >> 8,
                    jnp.where(lane == 1, offs_i & 255,
                              jnp.where(lane == 2, chunk >> 8, jnp.where(lane == 3, chunk & 255, 0))))
    ext_ref[...] = ext.astype(F32).T[:SUBLANES]


def _idx_kernel(clt_ref, ext_ref, lo_ref, hi_ref, idx_ref, *, SB):
    s = (pl.program_id(1) * SB + lax.broadcasted_iota(I32, (1, SB), 1)).astype(F32)
    onehot = jnp.where((lo_ref[:, 0:1] <= s) & (s < hi_ref[:, 0:1]), 1.0, 0.0).astype(BF16)
    cg = jnp.dot(clt_ref[...].astype(BF16), onehot, preferred_element_type=F32)
    ex = jnp.dot(ext_ref[...].astype(BF16), onehot, preferred_element_type=F32)
    slot_lo = ex[0:1] * 256.0 + ex[1:2]
    chunk = ex[2:3] * 256.0 + ex[3:4]
    within = jnp.sum(jnp.where(cg <= s - slot_lo, 1.0, 0.0), axis=0, keepdims=True)
    idx_ref[...] = (chunk * LANES + within).astype(I32)


def _select(aff_t, cap):
    E, N = aff_t.shape
    NC = N // LANES
    aff3 = aff_t.reshape(E, NC, LANES)
    t, need = pl.pallas_call(
        functools.partial(_thr_kernel, cap=cap),
        out_shape=(jax.ShapeDtypeStruct((E, LANES), I32), jax.ShapeDtypeStruct((E, LANES), I32)),
        compiler_params=pltpu.CompilerParams(vmem_limit_bytes=VMEM_LIMIT),
        name="select_threshold",
    )(aff3)
    per_e = lambda e: (e, 0, 0)
    smem = pl.BlockSpec(memory_space=pltpu.SMEM)
    clt, ext, lo, hi = pl.pallas_call(
        functools.partial(_pos_kernel, NC=NC),
        out_shape=(jax.ShapeDtypeStruct((E, LANES, NC), F32), jax.ShapeDtypeStruct((E, SUBLANES, NC), F32),
                   jax.ShapeDtypeStruct((E, NC, LANES), F32), jax.ShapeDtypeStruct((E, NC, LANES), F32)),
        grid=(E,),
        in_specs=[smem, smem, pl.BlockSpec((None, NC, LANES), per_e)],
        out_specs=(pl.BlockSpec((None, LANES, NC), per_e), pl.BlockSpec((None, SUBLANES, NC), per_e),
                   pl.BlockSpec((None, NC, LANES), per_e), pl.BlockSpec((None, NC, LANES), per_e)),
        compiler_params=_cparams(("parallel",)),
        name="select_prefix",
    )(t[:, 0], need[:, 0], aff3)
    SB = min(cap, 1024)
    per_e2 = lambda e, s: (e, 0, 0)
    idx = pl.pallas_call(
        functools.partial(_idx_kernel, SB=SB),
        out_shape=jax.ShapeDtypeStruct((E, 1, cap), I32),
        grid=(E, cap // SB),
        in_specs=[pl.BlockSpec((None, LANES, NC), per_e2), pl.BlockSpec((None, SUBLANES, NC), per_e2),
                  pl.BlockSpec((None, NC, LANES), per_e2), pl.BlockSpec((None, NC, LANES), per_e2)],
        out_specs=pl.BlockSpec((None, 1, SB), lambda e, s: (e, 0, s)),
        compiler_params=_cparams(("parallel", "arbitrary")),
        name="select_index",
    )(clt, ext, lo, hi)
    return idx.reshape(E, cap)


def _gather(x, idx_flat, K):
    N, D = x.shape
    R = idx_flat.shape[0]
    workers = SC_CORES * SC_SUBCORES
    per_w = R // workers
    mesh = plsc.VectorSubcoreMesh(core_axis_name="c", subcore_axis_name="s")

    @functools.partial(
        pl.kernel, mesh=mesh,
        out_type=jax.ShapeDtypeStruct((R, D), x.dtype),
        scratch_types=[pltpu.VMEM((K,), I32), pltpu.VMEM((K, D), x.dtype), pltpu.SemaphoreType.DMA],
        name="gather_rows_sc",
    )
    def run(x_hbm, idx_hbm, o_hbm, idx_v, rows_v, sem):
        base = (lax.axis_index("s") * SC_CORES + lax.axis_index("c")) * per_w

        @pl.loop(0, per_w // K)
        def _(c):
            off = base + c * K
            pltpu.sync_copy(idx_hbm.at[pl.ds(off, K)], idx_v)
            pltpu.async_copy(x_hbm.at[idx_v], rows_v, sem).wait()
            pltpu.sync_copy(rows_v, o_hbm.at[pl.ds(off, K)])

    return run(x, idx_flat)


def _ffn_kernel(x_ref, wg_ref, wu_ref, wd_ref, wr_ref, lo_ref, hi_ref):
    e = pl.program_id(0)
    xb = x_ref[...].astype(BF16)
    aff = _router_aff(xb, wr_ref[...])
    lane = lax.broadcasted_iota(I32, aff.shape, 1)
    val = jnp.sum(jnp.where(lane == e, aff, 0.0), axis=1, keepdims=True)
    gate = jnp.dot(xb, wg_ref[...], preferred_element_type=F32)
    up = jnp.dot(xb, wu_ref[...], preferred_element_type=F32)
    hid = gate * _sigmoid(gate) * up
    y = jnp.dot(hid.astype(BF16), wd_ref[...], preferred_element_type=F32) * val
    half = y.shape[1] // 2
    lo_ref[...] = y[:, :half]
    hi_ref[...] = y[:, half:]


def _ffn(xe, wg, wu, wd, wr, cap, tc):
    R, D = xe.shape
    E, _, F = wg.shape
    nb = cap // tc
    half = jax.ShapeDtypeStruct((R, D // 2), F32)
    return pl.pallas_call(
        _ffn_kernel,
        out_shape=(half, half),
        grid=(E, nb),
        in_specs=[pl.BlockSpec((tc, D), lambda e, c: (e * nb + c, 0)),
                  pl.BlockSpec((None, D, F), lambda e, c: (e, 0, 0)),
                  pl.BlockSpec((None, D, F), lambda e, c: (e, 0, 0)),
                  pl.BlockSpec((None, F, D), lambda e, c: (e, 0, 0)),
                  pl.BlockSpec(wr.shape, lambda e, c: (0, 0))],
        out_specs=(pl.BlockSpec((tc, D // 2), lambda e, c: (e * nb + c, 0)),
                   pl.BlockSpec((tc, D // 2), lambda e, c: (e * nb + c, 0))),
        compiler_params=_cparams(("parallel", "arbitrary")),
        name="expert_swiglu",
    )(xe, wg, wu, wd, wr)


def _scatter_add(n_tokens, ye_lo, ye_hi, idx_flat, cap, K):
    R, H = ye_lo.shape
    per_s = cap // SC_SUBCORES
    mesh = plsc.VectorSubcoreMesh(core_axis_name="c", subcore_axis_name="s")
    lanes = SC_LANES

    @functools.partial(
        pl.kernel, mesh=mesh, out_type=(),
        scratch_types=[pltpu.VMEM((K,), I32), pltpu.VMEM((K, H), F32), pltpu.VMEM((K, H), F32),
                       pltpu.SemaphoreType.DMA],
        name="scatter_add_sc",
    )
    def run(ylo_hbm, yhi_hbm, yelo_hbm, yehi_hbm, idx_hbm, idx_v, acc_v, add_v, sem):
        s = lax.axis_index("s")

        def half(y_hbm, ye_hbm):
            @pl.loop(0, N_EXPERTS)
            def _(e):
                @pl.loop(0, per_s // K)
                def _(ch):
                    off = e * cap + s * per_s + ch * K
                    pltpu.sync_copy(idx_hbm.at[pl.ds(off, K)], idx_v)
                    pltpu.sync_copy(ye_hbm.at[pl.ds(off, K)], add_v)
                    pltpu.async_copy(y_hbm.at[idx_v], acc_v, sem).wait()

                    @pl.loop(0, K)
                    def _(r):
                        for j in range(H // lanes):
                            sl = pl.ds(j * lanes, lanes)
                            acc_v[r, sl] = acc_v[r, sl] + add_v[r, sl]

                    pltpu.async_copy(acc_v, y_hbm.at[idx_v], sem).wait()

                plsc.subcore_barrier()

        @pl.when(lax.axis_index("c") == 0)
        def _():
            half(ylo_hbm, yelo_hbm)

        @pl.when(lax.axis_index("c") == 1)
        def _():
            half(yhi_hbm, yehi_hbm)

    y_lo = jax.new_ref(jnp.zeros((n_tokens, H), F32))
    y_hi = jax.new_ref(jnp.zeros((n_tokens, H), F32))
    run(y_lo, y_hi, ye_lo, ye_hi, idx_flat)
    return y_lo[...], y_hi[...]


def _ln2_kernel(x1_ref, ylo_ref, yhi_ref, g_ref, b_ref, x2_ref, xb_ref, *, alpha):
    y = jnp.concatenate([ylo_ref[...], yhi_ref[...]], axis=1)
    x2 = _layer_norm(alpha * x1_ref[...] + y, g_ref[...], b_ref[...])
    x2_ref[...] = x2
    xb_ref[...] = x2.astype(BF16)


def _ln2(x1, y_lo, y_hi, g, b, tm, alpha):
    N, D = x1.shape
    row = lambda i: (i, 0)
    full = lambda i: (0, 0)
    return pl.pallas_call(
        functools.partial(_ln2_kernel, alpha=alpha),
        out_shape=(jax.ShapeDtypeStruct((N, D), F32), jax.ShapeDtypeStruct((N, D), BF16)),
        grid=(N // tm,),
        in_specs=[pl.BlockSpec((tm, D), row), pl.BlockSpec((tm, D // 2), row), pl.BlockSpec((tm, D // 2), row),
                  pl.BlockSpec(g.shape, full), pl.BlockSpec(b.shape, full)],
        out_specs=(pl.BlockSpec((tm, D), row), pl.BlockSpec((tm, D), row)),
        compiler_params=_cparams(("parallel",)),
        name="residual_ln2",
    )(x1, y_lo, y_hi, g, b)


def _tiles(B, S):
    N = B * S
    tm = 1024 if S % 1024 == 0 else S
    return dict(tm_proj=tm, tq_b=min(512, S), tq_c=512, tm_merge=512,
                tc=min(512, CAPACITY_FACTOR * N // N_EXPERTS))


def _prep_layer(l, w_in, pool_mix, pool_scale, sink_logit, w_branch, w_out, ln1_g, ln1_b, w_router,
                w_gate_e, w_up_e, w_down_e, ln2_g, ln2_b):
    D = w_in.shape[1]
    a0, qb0 = 0, A_WIDTH
    kb0 = qb0 + B_Q_HEADS * B_HEAD_DIM
    vb0 = kb0 + B_KV_HEADS * B_HEAD_DIM
    qc0 = vb0 + B_KV_HEADS * B_HEAD_DIM
    kc0 = qc0 + len(C_CONFIGS) * C_GROUP_W
    vc0 = kc0 + len(C_CONFIGS) * C_GROUP_W
    zg0 = vc0 + len(C_CONFIGS) * C_GROUP_W
    spread = np.concatenate([h * C_HEAD_DIM + _spread_head_perm(C_HEAD_DIM) for h in range(C_HEADS)])
    wl = w_in[l]
    wr = jnp.zeros((D, LANES), F32).at[:, :N_EXPERTS].set(w_router[l])
    main = [wl[:, qb0:vb0], jnp.concatenate([wl[:, a0:qb0], wl[:, vb0:qc0]], axis=1), 0.5 * wl[:, zg0:]]
    return dict(
        w_main=jnp.concatenate(main, axis=1).astype(BF16), main_widths=tuple(m.shape[1] for m in main),
        w_c=[jnp.concatenate([wl[:, c0 + g * C_GROUP_W:c0 + (g + 1) * C_GROUP_W][:, cols]
                              for c0, cols in ((qc0, spread), (kc0, spread), (vc0, np.arange(C_GROUP_W)))],
                             axis=1).astype(BF16) for g in range(len(C_CONFIGS))],
        pmix=pool_mix[l].astype(BF16), pscale=pool_scale[l][None, :], sink=sink_logit[l],
        wbr=(0.5 * w_branch[l]).astype(BF16), wout=w_out[l].astype(BF16),
        g1=ln1_g[l][None, :], b1=ln1_b[l][None, :], wr=wr.astype(BF16),
        wg=w_gate_e[l].astype(BF16), wu=w_up_e[l].astype(BF16), wd=w_down_e[l].astype(BF16),
        g2=ln2_g[l][None, :], b2=ln2_b[l][None, :],
    )


def _layer(x, xb, p, B, S, alpha, rot_b, rot_c):
    N, D = x.shape
    t = _tiles(B, S)
    cap = CAPACITY_FACTOR * N // N_EXPERTS
    qkb, avb, zg = _proj_main(xb, p["w_main"], p["main_widths"], S, t["tm_proj"], rot_b)
    o_b = _attn_b(qkb, avb, p["sink"], B, S, t["tq_b"])
    o_cs, lses = [], []
    for g, (_, d) in enumerate(C_CONFIGS):
        qkv = _proj_fold(xb, p["w_c"][g], B, S, d, t["tm_proj"], 2 * C_GROUP_W, rot_c)
        o, lse = _attn_c(qkv, g, B, S, t["tq_c"])
        o_cs.append(o)
        lses.append(lse)
    x1, aff_t = _merge(x, avb, o_b, o_cs, lses, zg, p["pmix"], p["pscale"], p["wbr"], p["wout"], p["g1"], p["b1"],
                       p["wr"], S, t["tm_merge"], alpha)
    idx_flat = _select(aff_t, cap).reshape(-1)
    xe = _gather(x1, idx_flat, SC_GATHER_ROWS)
    ye_lo, ye_hi = _ffn(xe, p["wg"], p["wu"], p["wd"], p["wr"], cap, t["tc"])
    y_lo, y_hi = _scatter_add(N, ye_lo, ye_hi, idx_flat, cap, SC_GATHER_ROWS)
    return _ln2(x1, y_lo, y_hi, p["g2"], p["b2"], t["tm_proj"], alpha)


def _trunk(x, layers, alpha):
    B, S, D = x.shape
    rot_b = _rot_tables(S, B_HEAD_DIM)
    rot_c = _rot_tables_spread(S, C_HEAD_DIM)
    xf = x.reshape(B * S, D)
    xb = xf.astype(BF16)
    for p in layers:
        xf, xb = _layer(xf, xb, p, B, S, alpha, rot_b, rot_c)
    return xf.reshape(B, S, D)


def kernel(x_prompt, x_sample, w_in, pool_mix, pool_scale, sink_logit, w_branch, w_out, ln1_g, ln1_b, w_router,
           w_gate_e, w_up_e, w_down_e, ln2_g, ln2_b):
    depth = w_in.shape[0]
    alpha = (2 * depth) ** 0.25
    layers = [_prep_layer(l, w_in, pool_mix, pool_scale, sink_logit, w_branch, w_out, ln1_g, ln1_b, w_router,
                          w_gate_e, w_up_e, w_down_e, ln2_g, ln2_b) for l in range(depth)]
    return (_trunk(x_prompt, layers, alpha), _trunk(x_sample, layers, alpha))
```

```python
import functools

import numpy as np
import jax
import jax.numpy as jnp
from jax import lax
from jax.experimental import pallas as pl
from jax.experimental.pallas import tpu as pltpu
from jax.experimental.pallas import tpu_sc as plsc

F32 = jnp.float32
BF16 = jnp.bfloat16
I32 = jnp.int32

POOL_WINDOWS = (2, 4, 8, 16)
POOL_CH = 128
A_WIDTH = 512
B_Q_HEADS = 8
B_KV_HEADS = 2
B_HEAD_DIM = 64
B_HALF_WINDOW = 128
C_CONFIGS = ((128, 1), (512, 4), (2048, 16))
C_HEADS = 4
C_HEAD_DIM = 128
C_GROUP_W = C_HEADS * C_HEAD_DIM
FOLD_DILATIONS = tuple(d for _, d in C_CONFIGS if d > 1)
N_EXPERTS = 16
CAPACITY_FACTOR = 2
ROPE_THETA = 500000.0
ROT_DIV = 4
LN_EPS = 1e-5
NEG_INF = -1e30
LOG2E = 1.4426950408889634
LN2 = 0.6931471805599453

LANES = 128
SUBLANES = 8
VMEM_LIMIT = 56 * 1024 * 1024
SC_CORES = 2
SC_SUBCORES = 16
SC_LANES = 16
SC_GATHER_ROWS = 64
MXU_WIDTH = 256
ATTN_STRIP = 32
ATTN_UNROLL = 8
PERM_BLOCK = MXU_WIDTH


def _cparams(sem):
    return pltpu.CompilerParams(dimension_semantics=sem, vmem_limit_bytes=VMEM_LIMIT)


def _col_chunks(width):
    return [(c, min(c + MXU_WIDTH, width)) for c in range(0, width, MXU_WIDTH)]


def _rotary(acc, c, s1, s2, rot_half):
    outs = []
    for g in range(acc.shape[1] // LANES):
        a = acc[:, g * LANES:(g + 1) * LANES]
        if 2 * rot_half == LANES:
            outs.append(a * c + pltpu.roll(a, rot_half, 1) * s1)
        else:
            outs.append(a * c + pltpu.roll(a, LANES - rot_half, 1) * s1 + pltpu.roll(a, rot_half, 1) * s2)
    return outs[0] if len(outs) == 1 else jnp.concatenate(outs, axis=1)


def _proj_main_kernel(x_ref, w_ref, cos_ref, s1_ref, s2_ref, qk_ref, av_ref, zg_ref, *, rot_half):
    x = x_ref[...]
    base = 0
    for o_ref, rotate in ((qk_ref, True), (av_ref, False), (zg_ref, False)):
        for c0, c1 in _col_chunks(o_ref.shape[1]):
            acc = jnp.dot(x, w_ref[:, base + c0:base + c1], preferred_element_type=F32)
            if rotate:
                acc = _rotary(acc, cos_ref[...], s1_ref[...], s2_ref[...], rot_half)
            o_ref[:, c0:c1] = acc.astype(o_ref.dtype)
        base += o_ref.shape[1]


def _proj_main(x, w, widths, S, tm, rot):
    N, K = x.shape
    rot_half, tabs = rot
    spb = S // tm
    row = lambda i: (i, 0)
    tab_spec = pl.BlockSpec((tm, LANES), lambda i: (i % spb, 0))
    dtypes = (BF16, F32, BF16)
    return pl.pallas_call(
        functools.partial(_proj_main_kernel, rot_half=rot_half),
        out_shape=tuple(jax.ShapeDtypeStruct((N, c), dt) for c, dt in zip(widths, dtypes)),
        grid=(N // tm,),
        in_specs=[pl.BlockSpec((tm, K), row), pl.BlockSpec(w.shape, lambda i: (0, 0)), tab_spec, tab_spec, tab_spec],
        out_specs=tuple(pl.BlockSpec((tm, c), row) for c in widths),
        compiler_params=_cparams(("parallel",)),
        name="proj_main",
    )(x, w, *tabs)


def _fold_perm(d, inverse=False):
    i = lax.broadcasted_iota(I32, (PERM_BLOCK, PERM_BLOCK), 1 if inverse else 0)
    j = lax.broadcasted_iota(I32, (PERM_BLOCK, PERM_BLOCK), 0 if inverse else 1)
    per = PERM_BLOCK // d
    return jnp.where(j == (i % per) * d + i // per, 1.0, 0.0).astype(BF16)


def _proj_qkv_kernel(x_ref, w_ref, cos_ref, s1_ref, s2_ref, o_ref, *, rot_half, rot_cols):
    x = x_ref[...]
    for c0, c1 in _col_chunks(w_ref.shape[1]):
        acc = jnp.dot(x, w_ref[:, c0:c1], preferred_element_type=F32)
        if c0 < rot_cols:
            acc = _rotary(acc, cos_ref[...], s1_ref[...], s2_ref[...], rot_half)
        o_ref[:, c0:c1] = acc.astype(BF16)


def _proj_qkv(x, w, S, tm, rot_cols, rot, name):
    N, K = x.shape
    C = w.shape[1]
    rot_half, tabs = rot
    spb = S // tm
    tab_spec = pl.BlockSpec((tm, LANES), lambda i: (i % spb, 0))
    return pl.pallas_call(
        functools.partial(_proj_qkv_kernel, rot_half=rot_half, rot_cols=rot_cols),
        out_shape=jax.ShapeDtypeStruct((N, C), BF16),
        grid=(N // tm,),
        in_specs=[pl.BlockSpec((tm, K), lambda i: (i, 0)), pl.BlockSpec((K, C), lambda i: (0, 0)),
                  tab_spec, tab_spec, tab_spec],
        out_specs=pl.BlockSpec((tm, C), lambda i: (i, 0)),
        compiler_params=_cparams(("parallel",)),
        name=name,
    )(x, w, *tabs)


def _rot_tables(S, head_dim):
    rot = head_dim // ROT_DIV
    half = rot // 2
    inv = 1.0 / (ROPE_THETA ** (jnp.arange(0, rot, 2, dtype=F32) / rot))
    ang = jnp.arange(S, dtype=F32)[:, None] * inv[None, :]
    cos, sin = jnp.cos(ang), jnp.sin(ang)
    d = np.arange(LANES) % head_dim
    first = d < half
    second = (d >= half) & (d < rot)
    src = np.where(first, d, np.where(second, d - half, 0))
    cos_t = jnp.where(first | second, cos[:, src], 1.0)
    s1_t = jnp.where(first, -sin[:, src], 0.0)
    s2_t = jnp.where(second, sin[:, src], 0.0)
    return half, (cos_t, s1_t, s2_t)


def _spread_head_perm(head_dim):
    half = head_dim // ROT_DIV // 2
    mid = LANES // 2
    return np.concatenate([np.arange(0, half), np.arange(2 * half, mid + half), np.arange(half, 2 * half),
                           np.arange(mid + half, head_dim)])


def _rot_tables_spread(S, head_dim):
    half, (cos_t, s1_t, s2_t) = _rot_tables(S, head_dim)
    perm = _spread_head_perm(head_dim)
    return LANES // 2, (cos_t[:, perm], (s1_t + s2_t)[:, perm], s2_t)


def _band_bias(bias_ref, W, q0, L, sub):
    tq, nk = bias_ref.shape
    row = lax.broadcasted_iota(I32, (tq, nk), 0)
    col = lax.broadcasted_iota(I32, (tq, nk), 1)
    first = (row // sub) * sub
    kpos = q0 - W + first + col
    valid = (jnp.abs(col - W - (row - first)) <= W) & (kpos >= 0) & (kpos < L)
    bias_ref[...] = jnp.where(valid, 0.0, NEG_INF)


def _softmax_strips(s_ref, p_ref, den_ref, max_ref, bias_ref, *, unroll, sink_fn=None):
    rows, nk = s_ref.shape
    tq = bias_ref.shape[0]

    def strip(t):
        r0 = pl.multiple_of(t * ATTN_STRIP, ATTN_STRIP)
        rows_sl = pl.ds(r0, ATTN_STRIP)
        s = s_ref[rows_sl, :] + bias_ref[pl.ds(pl.multiple_of(r0 % tq, ATTN_STRIP), ATTN_STRIP), :]
        return r0, rows_sl, s

    def row_max(t, carry):
        r0, rows_sl, s = strip(t)
        m = jnp.max(s, axis=1, keepdims=True)
        if sink_fn is not None:
            m = jnp.maximum(m, sink_fn(r0 // tq) * LOG2E)
        max_ref[rows_sl, :] = jnp.broadcast_to(m, (ATTN_STRIP, LANES))
        return carry

    def numerators(t, carry):
        r0, rows_sl, s = strip(t)
        m = max_ref[rows_sl, :]
        p = jnp.exp2(s - jnp.concatenate([m] * (nk // LANES), axis=1))
        den = jnp.broadcast_to(jnp.sum(p, axis=1, keepdims=True), (ATTN_STRIP, LANES))
        if sink_fn is not None:
            den = den + jnp.exp2(sink_fn(r0 // tq) * LOG2E - m)
        p_ref[rows_sl, :] = p.astype(BF16)
        den_ref[rows_sl, :] = den
        return carry

    lax.fori_loop(0, rows // ATTN_STRIP, row_max, 0, unroll=unroll)
    lax.fori_loop(0, rows // ATTN_STRIP, numerators, 0, unroll=unroll)


def _attn_b_kernel(sink_ref, q_ref, kp_ref, km_ref, kn_ref, vp_ref, vm_ref, vn_ref, o_ref,
                   s_ref, p_ref, den_ref, max_ref, bias_ref, *, S, TQ):
    W = B_HALF_WINDOW
    G = B_Q_HEADS // B_KV_HEADS
    sub, span = W, 3 * W
    nb = TQ // sub
    q0 = pl.program_id(1) * TQ
    _band_bias(bias_ref, W, q0, S, sub)
    k_all = jnp.concatenate([kp_ref[...], km_ref[...], kn_ref[...]], axis=0)
    v_all = jnp.concatenate([vp_ref[...], vm_ref[...], vn_ref[...]], axis=0).astype(BF16)
    q = q_ref[...]
    outs = []
    for j in range(B_KV_HEADS):
        kj = k_all[:, j * B_HEAD_DIM:(j + 1) * B_HEAD_DIM]
        vj = v_all[:, j * B_HEAD_DIM:(j + 1) * B_HEAD_DIM]
        for b in range(nb):
            qs = jnp.concatenate([q[b * sub:(b + 1) * sub, (G * j + g) * B_HEAD_DIM:(G * j + g + 1) * B_HEAD_DIM]
                                  for g in range(G)], axis=0)
            s = lax.dot_general(qs, kj[b * sub:b * sub + span], (((1,), (1,)), ((), ())),
                                preferred_element_type=F32) * (B_HEAD_DIM ** -0.5 * LOG2E)
            for g in range(G):
                s_ref[j, g * TQ + b * sub:g * TQ + (b + 1) * sub, :] = s[g * sub:(g + 1) * sub]
        _softmax_strips(s_ref.at[j], p_ref.at[j], den_ref.at[j], max_ref.at[j], bias_ref, unroll=True,
                        sink_fn=lambda g, j=j: sink_ref[G * j + g])
        pv = []
        for b in range(nb):
            pb = jnp.concatenate([p_ref[j, g * TQ + b * sub:g * TQ + (b + 1) * sub, :] for g in range(G)], axis=0)
            pv.append(jnp.dot(pb, vj[b * sub:b * sub + span], preferred_element_type=F32))
        for g in range(G):
            o = jnp.concatenate([pv[b][g * sub:(g + 1) * sub] for b in range(nb)], axis=0)
            outs.append(o / den_ref[j, g * TQ:(g + 1) * TQ, 0:1])
    o_ref[...] = jnp.concatenate(outs, axis=1).astype(o_ref.dtype)


def _attn_b(qk, av, sink, B, S, TQ):
    N = B * S
    W = B_HALF_WINDOW
    nq = S // TQ
    r = TQ // W
    kcol = (B_Q_HEADS * B_HEAD_DIM) // LANES
    last = N // W - 1
    rows = (B_Q_HEADS // B_KV_HEADS) * TQ

    def prev(b, i):
        return (jnp.maximum((b * nq + i) * r - 1, 0), kcol)

    def main(b, i):
        return (b * nq + i, kcol)

    def nxt(b, i):
        return (jnp.minimum((b * nq + i + 1) * r, last), kcol)

    return pl.pallas_call(
        functools.partial(_attn_b_kernel, S=S, TQ=TQ),
        out_shape=jax.ShapeDtypeStruct((N, B_Q_HEADS * B_HEAD_DIM), BF16),
        grid=(B, nq),
        in_specs=[pl.BlockSpec(memory_space=pltpu.SMEM),
                  pl.BlockSpec((TQ, B_Q_HEADS * B_HEAD_DIM), lambda b, i: (b * nq + i, 0)),
                  pl.BlockSpec((W, LANES), prev), pl.BlockSpec((TQ, LANES), main), pl.BlockSpec((W, LANES), nxt),
                  pl.BlockSpec((W, LANES), prev), pl.BlockSpec((TQ, LANES), main), pl.BlockSpec((W, LANES), nxt)],
        out_specs=pl.BlockSpec((TQ, B_Q_HEADS * B_HEAD_DIM), lambda b, i: (b * nq + i, 0)),
        scratch_shapes=[pltpu.VMEM((B_KV_HEADS, rows, 3 * W), F32),
                        pltpu.VMEM((B_KV_HEADS, rows, 3 * W), BF16),
                        pltpu.VMEM((B_KV_HEADS, rows, LANES), F32), pltpu.VMEM((B_KV_HEADS, rows, LANES), F32),
                        pltpu.VMEM((TQ, 3 * W), F32)],
        compiler_params=_cparams(("parallel", "arbitrary")),
        name="attn_window",
    )(sink, qk, qk, qk, qk, av, av, av)


def _attn_c_kernel(q_ref, kp_ref, km_ref, kn_ref, vp_ref, vm_ref, vn_ref, o_ref, lse_ref,
                   s_ref, p_ref, den_ref, max_ref, bias_ref, *, M, TQ, W):
    sub, span = 2 * W, 4 * W
    q0 = pl.program_id(1) * TQ
    _band_bias(bias_ref, W, q0, M, sub)
    k_all = jnp.concatenate([kp_ref[...], km_ref[...], kn_ref[...]], axis=0)
    v_all = jnp.concatenate([vp_ref[...], vm_ref[...], vn_ref[...]], axis=0)
    q = q_ref[...]
    lane = lax.broadcasted_iota(I32, (TQ, LANES), 1)
    lse_tile = jnp.zeros((TQ, LANES), F32)
    outs = []
    for h in range(C_HEADS):
        sl = slice(h * C_HEAD_DIM, (h + 1) * C_HEAD_DIM)
        for b in range(TQ // sub):
            s_ref[h, b * sub:(b + 1) * sub, :] = lax.dot_general(
                q[b * sub:(b + 1) * sub, sl], k_all[b * sub:b * sub + span, sl], (((1,), (1,)), ((), ())),
                preferred_element_type=F32) * (C_HEAD_DIM ** -0.5 * LOG2E)
        _softmax_strips(s_ref.at[h], p_ref.at[h], den_ref.at[h], max_ref.at[h], bias_ref, unroll=True)
        den = den_ref[h]
        pv = [jnp.dot(p_ref[h, b * sub:(b + 1) * sub, :], v_all[b * sub:b * sub + span, sl],
                      preferred_element_type=F32) for b in range(TQ // sub)]
        outs.append(jnp.concatenate(pv, axis=0) / den[:, 0:1])
        lse_tile = jnp.where(lane == h, max_ref[h] * LN2 + jnp.log(den), lse_tile)
    o_ref[...] = jnp.concatenate(outs, axis=1).astype(o_ref.dtype)
    lse_ref[...] = lse_tile


def _attn_c(qkv, g, B, S, TQ):
    window, d = C_CONFIGS[g]
    W = window // (2 * d)
    M = S // d
    TQ = min(TQ, M)
    r = TQ // W
    GW = C_GROUP_W
    last = M // W - 1

    def mk(col):
        return [pl.BlockSpec((None, W, GW), lambda b, i: (b, jnp.maximum(i * r - 1, 0), col)),
                pl.BlockSpec((None, TQ, GW), lambda b, i: (b, i, col)),
                pl.BlockSpec((None, W, GW), lambda b, i: (b, jnp.minimum((i + 1) * r, last), col))]

    return pl.pallas_call(
        functools.partial(_attn_c_kernel, M=M, TQ=TQ, W=W),
        out_shape=(jax.ShapeDtypeStruct((B * d, M, GW), BF16), jax.ShapeDtypeStruct((B * d, M, LANES), F32)),
        grid=(B * d, M // TQ),
        in_specs=[pl.BlockSpec((None, TQ, GW), lambda b, i: (b, i, 0))] + mk(1) + mk(2),
        out_specs=(pl.BlockSpec((None, TQ, GW), lambda b, i: (b, i, 0)),
                   pl.BlockSpec((None, TQ, LANES), lambda b, i: (b, i, 0))),
        scratch_shapes=[pltpu.VMEM((C_HEADS, TQ, 4 * W), F32), pltpu.VMEM((C_HEADS, TQ, 4 * W), BF16),
                        pltpu.VMEM((C_HEADS, TQ, LANES), F32), pltpu.VMEM((C_HEADS, TQ, LANES), F32),
                        pltpu.VMEM((TQ, 4 * W), F32)],
        compiler_params=_cparams(("parallel", "arbitrary")),
        name=f"attn_dilated_{d}",
    )(qkv, qkv, qkv, qkv, qkv, qkv, qkv)


def _sigmoid(z):
    return 0.5 * jnp.tanh(0.5 * z) + 0.5


def _layer_norm(h, g, b):
    mu = jnp.mean(h, axis=-1, keepdims=True)
    var = jnp.mean(jnp.square(h - mu), axis=-1, keepdims=True)
    return (h - mu) * lax.rsqrt(var + LN_EPS) * g + b


def _router_aff(xb, wr):
    logits = jnp.dot(xb, wr, preferred_element_type=F32)
    lane = lax.broadcasted_iota(I32, logits.shape, 1)
    logits = jnp.where(lane < N_EXPERTS, logits, NEG_INF)
    e = jnp.exp(logits - jnp.max(logits, axis=1, keepdims=True))
    return e / jnp.sum(e, axis=1, keepdims=True)


def _unfold(ref, scr_ref):
    d, rows, _ = ref.shape
    if d == 1:
        return ref[0]
    for rr in range(d):
        scr_ref[pl.ds(rr, rows, stride=d), :] = ref[rr]
    return scr_ref[...]


def _unfold_bf16(ref):
    d, rows, _ = ref.shape
    if d == 1:
        return ref[0].astype(F32)
    per = PERM_BLOCK // d
    perm = _fold_perm(d, inverse=True)
    outs = []
    for blk in range(d * rows // PERM_BLOCK):
        stacked = jnp.concatenate([ref[rr, blk * per:(blk + 1) * per, :] for rr in range(d)], axis=0)
        outs.append(jnp.dot(perm, stacked, preferred_element_type=F32))
    return outs[0] if len(outs) == 1 else jnp.concatenate(outs, axis=0)


def _merge_kernel(x_ref, a_ref, ap_ref, an_ref, ob_ref, oc0_ref, oc1_ref, oc2_ref, l0_ref, l1_ref, l2_ref,
                  zg_ref, pmix_ref, pscale_ref, wbr_ref, wout_ref, g_ref, b_ref, wr_ref,
                  x1_ref, aff_ref, sl1_ref, sl2_ref, *, S, tm, alpha):
    HW = max(POOL_WINDOWS) // 2
    L = tm + 2 * HW
    pos0 = (pl.program_id(0) * tm) % S
    xa = jnp.concatenate([ap_ref[...], a_ref[...], an_ref[...]], axis=0)
    xpos = pos0 - HW + lax.broadcasted_iota(I32, (L, 1), 0)
    xa = jnp.where((xpos >= 0) & (xpos < S), xa, 0.0)
    sums = {}
    t, w = xa, 1
    while w < max(POOL_WINDOWS):
        t = t + pltpu.roll(t, L - w, 0)
        w *= 2
        sums[w] = t
    pos = pos0 + lax.broadcasted_iota(I32, (tm, 1), 0)
    mixed = []
    for g, w in enumerate(POOL_WINDOWS):
        cs = slice(g * POOL_CH, (g + 1) * POOL_CH)
        off = HW - w // 2
        sw = sums[w][:, cs]
        if off:
            sw = pltpu.roll(sw, L - off, 0)
        sw = sw[:tm]
        cnt = (jnp.clip(pos + w // 2, 0, S) - jnp.clip(pos - w // 2, 0, S)).astype(F32)
        pooled = sw / cnt - a_ref[:, cs]
        mixed.append(jnp.dot(pooled.astype(BF16), pmix_ref[g], preferred_element_type=F32) * pscale_ref[:, cs])
    o_a = jnp.concatenate(mixed, axis=1)
    ls = [_unfold(l0_ref, None), _unfold(l1_ref, sl1_ref), _unfold(l2_ref, sl2_ref)]
    mx = jnp.maximum(jnp.maximum(ls[0], ls[1]), ls[2])
    es = [jnp.exp(l - mx) for l in ls]
    tot = es[0] + es[1] + es[2]
    ocs = [_unfold_bf16(oc0_ref), _unfold_bf16(oc1_ref), _unfold_bf16(oc2_ref)]
    pieces = []
    for h in range(C_HEADS):
        sl = slice(h * C_HEAD_DIM, (h + 1) * C_HEAD_DIM)
        acc = None
        for gi in range(3):
            term = (es[gi] / tot)[:, h:h + 1] * ocs[gi][:, sl].astype(F32)
            acc = term if acc is None else acc + term
        pieces.append(acc)
    o_c = jnp.concatenate(pieces, axis=1)
    branches = [o_a.astype(BF16), ob_ref[...], o_c.astype(BF16)]
    D = x_ref.shape[1]
    merged = []
    for c0, c1 in _col_chunks(D):
        acc = None
        for k in range(3):
            gate2 = jnp.tanh(zg_ref[:, k * D + c0:k * D + c1].astype(F32)) + 1.0
            term = gate2 * jnp.dot(branches[k], wbr_ref[k, :, c0:c1], preferred_element_type=F32)
            acc = term if acc is None else acc + term
        merged.append(acc.astype(BF16))
    merged = jnp.concatenate(merged, axis=1)
    pre = [alpha * x_ref[:, c0:c1] + jnp.dot(merged, wout_ref[:, c0:c1], preferred_element_type=F32)
           for c0, c1 in _col_chunks(D)]
    x1 = _layer_norm(jnp.concatenate(pre, axis=1), g_ref[...], b_ref[...])
    x1_ref[...] = x1
    aff = _router_aff(x1.astype(BF16), wr_ref[...])
    aff_ref[...] = aff.T[:N_EXPERTS]


def _merge(x, av, o_b, o_cs, lses, vz, pmix, pscale, wbr, wout, g, b, wr, S, tm, alpha):
    N, D = x.shape
    HW = max(POOL_WINDOWS) // 2
    r = tm // HW
    last = N // HW - 1
    row = lambda i: (i, 0)
    full2 = lambda i: (0, 0)
    full3 = lambda i: (0, 0, 0)
    spb = S // tm
    dils = [d for _, d in C_CONFIGS]

    def folded(width):
        return [pl.BlockSpec((d, tm // d, width), lambda i: (i // spb, i % spb, 0)) for d in dils]

    in_specs = [
        pl.BlockSpec((tm, D), row),
        pl.BlockSpec((tm, A_WIDTH), row),
        pl.BlockSpec((HW, A_WIDTH), lambda i: (jnp.maximum(i * r - 1, 0), 0)),
        pl.BlockSpec((HW, A_WIDTH), lambda i: (jnp.minimum((i + 1) * r, last), 0)),
        pl.BlockSpec((tm, B_Q_HEADS * B_HEAD_DIM), row),
        *folded(C_GROUP_W),
        *folded(LANES),
        pl.BlockSpec((tm, 3 * D), row),
        pl.BlockSpec(pmix.shape, full3),
        pl.BlockSpec(pscale.shape, full2),
        pl.BlockSpec(wbr.shape, full3),
        pl.BlockSpec(wout.shape, full2),
        pl.BlockSpec(g.shape, full2), pl.BlockSpec(b.shape, full2),
        pl.BlockSpec(wr.shape, full2),
    ]
    return pl.pallas_call(
        functools.partial(_merge_kernel, S=S, tm=tm, alpha=alpha),
        out_shape=(jax.ShapeDtypeStruct((N, D), F32), jax.ShapeDtypeStruct((N_EXPERTS, N), F32)),
        grid=(N // tm,),
        in_specs=in_specs,
        out_specs=(pl.BlockSpec((tm, D), row), pl.BlockSpec((N_EXPERTS, tm), lambda i: (0, i))),
        scratch_shapes=[pltpu.VMEM((tm, LANES), F32)] * 2,
        compiler_params=_cparams(("parallel",)),
        name="merge_ln1_router",
    )(x, av, av, av, o_b, *o_cs, *lses, vz, pmix, pscale, wbr, wout, g, b, wr)


def _count(mask):
    return jnp.sum(jnp.sum(jnp.where(mask, 1.0, 0.0), axis=1), axis=1, keepdims=True)


def _thr_kernel(aff_ref, t_ref, need_ref, *, cap):
    bits = pltpu.bitcast(aff_ref[...], I32)

    def body(k, t):
        cand = t | jnp.left_shift(jnp.int32(1), 30 - k)
        return jnp.where(_count(bits >= cand[:, :, None]) >= cap, cand, t)

    t = lax.fori_loop(0, 31, body, jnp.zeros((N_EXPERTS, 1), I32))
    need = cap - _count(bits > t[:, :, None])
    t_ref[...] = jnp.broadcast_to(t, t_ref.shape)
    need_ref[...] = jnp.broadcast_to(need.astype(I32), need_ref.shape)


def _pos_kernel(t_ref, need_ref, aff_ref, clt_ref, ext_ref, lo_ref, hi_ref, *, NC):
    e = pl.program_id(0)
    bits = pltpu.bitcast(aff_ref[...], I32)
    t = t_ref[e]
    need = need_ref[e].astype(F32)
    upper = jnp.where(lax.broadcasted_iota(I32, (LANES, LANES), 0) <= lax.broadcasted_iota(I32, (LANES, LANES), 1),
                      1.0, 0.0).astype(BF16)
    ones = jnp.ones((LANES, LANES), BF16)
    before = jnp.where(lax.broadcasted_iota(I32, (NC, NC), 1) < lax.broadcasted_iota(I32, (NC, NC), 0),
                       1.0, 0.0).astype(BF16)

    def cums(xf):
        xb = xf.astype(BF16)
        local = jnp.dot(xb, upper, preferred_element_type=F32)
        tot = jnp.dot(xb, ones, preferred_element_type=F32)
        offs = jnp.dot(before, tot.astype(BF16), preferred_element_type=F32)
        return local, tot, offs

    eq = jnp.where(bits == t, 1.0, 0.0)
    local, _, offs = cums(eq)
    sel = (bits > t) | ((bits == t) & (local + offs - eq < need))
    local, tot, offs = cums(jnp.where(sel, 1.0, 0.0))
    lo_ref[...] = offs
    hi_ref[...] = offs + tot
    clt_ref[...] = local.T
    chunk = lax.broadcasted_iota(I32, (NC, LANES), 0)
    lane = lax.broadcasted_iota(I32, (NC, LANES), 1)
    offs_i = offs.astype(I32)
    ext = jnp.where(lane == 0, offs_i >> 8,
                    jnp.where(lane == 1, offs_i & 255,
                              jnp.where(lane == 2, chunk >> 8, jnp.where(lane == 3, chunk & 255, 0))))
    ext_ref[...] = ext.astype(F32).T[:SUBLANES]


def _idx_kernel(clt_ref, ext_ref, lo_ref, hi_ref, idx_ref, *, SB):
    s = (pl.program_id(1) * SB + lax.broadcasted_iota(I32, (1, SB), 1)).astype(F32)
    onehot = jnp.where((lo_ref[:, 0:1] <= s) & (s < hi_ref[:, 0:1]), 1.0, 0.0).astype(BF16)
    cg = jnp.dot(clt_ref[...].astype(BF16), onehot, preferred_element_type=F32)
    ex = jnp.dot(ext_ref[...].astype(BF16), onehot, preferred_element_type=F32)
    slot_lo = ex[0:1] * 256.0 + ex[1:2]
    chunk = ex[2:3] * 256.0 + ex[3:4]
    within = jnp.sum(jnp.where(cg <= s - slot_lo, 1.0, 0.0), axis=0, keepdims=True)
    idx_ref[...] = (chunk * LANES + within).astype(I32)


def _select(aff_t, cap):
    E, N = aff_t.shape
    NC = N // LANES
    aff3 = aff_t.reshape(E, NC, LANES)
    t, need = pl.pallas_call(
        functools.partial(_thr_kernel, cap=cap),
        out_shape=(jax.ShapeDtypeStruct((E, LANES), I32), jax.ShapeDtypeStruct((E, LANES), I32)),
        compiler_params=pltpu.CompilerParams(vmem_limit_bytes=VMEM_LIMIT),
        name="select_threshold",
    )(aff3)
    per_e = lambda e: (e, 0, 0)
    smem = pl.BlockSpec(memory_space=pltpu.SMEM)
    clt, ext, lo, hi = pl.pallas_call(
        functools.partial(_pos_kernel, NC=NC),
        out_shape=(jax.ShapeDtypeStruct((E, LANES, NC), F32), jax.ShapeDtypeStruct((E, SUBLANES, NC), F32),
                   jax.ShapeDtypeStruct((E, NC, LANES), F32), jax.ShapeDtypeStruct((E, NC, LANES), F32)),
        grid=(E,),
        in_specs=[smem, smem, pl.BlockSpec((None, NC, LANES), per_e)],
        out_specs=(pl.BlockSpec((None, LANES, NC), per_e), pl.BlockSpec((None, SUBLANES, NC), per_e),
                   pl.BlockSpec((None, NC, LANES), per_e), pl.BlockSpec((None, NC, LANES), per_e)),
        compiler_params=_cparams(("parallel",)),
        name="select_prefix",
    )(t[:, 0], need[:, 0], aff3)
    SB = min(cap, 1024)
    per_e2 = lambda e, s: (e, 0, 0)
    idx = pl.pallas_call(
        functools.partial(_idx_kernel, SB=SB),
        out_shape=jax.ShapeDtypeStruct((E, 1, cap), I32),
        grid=(E, cap // SB),
        in_specs=[pl.BlockSpec((None, LANES, NC), per_e2), pl.BlockSpec((None, SUBLANES, NC), per_e2),
                  pl.BlockSpec((None, NC, LANES), per_e2), pl.BlockSpec((None, NC, LANES), per_e2)],
        out_specs=pl.BlockSpec((None, 1, SB), lambda e, s: (e, 0, s)),
        compiler_params=_cparams(("parallel", "arbitrary")),
        name="select_index",
    )(clt, ext, lo, hi)
    return idx.reshape(E, cap)


def _gather(x, idx_flat, K):
    N, D = x.shape
    R = idx_flat.shape[0]
    workers = SC_CORES * SC_SUBCORES
    per_w = R // workers
    mesh = plsc.VectorSubcoreMesh(core_axis_name="c", subcore_axis_name="s")

    @functools.partial(
        pl.kernel, mesh=mesh,
        out_type=jax.ShapeDtypeStruct((R, D), x.dtype),
        scratch_types=[pltpu.VMEM((K,), I32), pltpu.VMEM((K, D), x.dtype), pltpu.SemaphoreType.DMA],
        name="gather_rows_sc",
    )
    def run(x_hbm, idx_hbm, o_hbm, idx_v, rows_v, sem):
        base = (lax.axis_index("s") * SC_CORES + lax.axis_index("c")) * per_w

        @pl.loop(0, per_w // K)
        def _(c):
            off = base + c * K
            pltpu.sync_copy(idx_hbm.at[pl.ds(off, K)], idx_v)
            pltpu.async_copy(x_hbm.at[idx_v], rows_v, sem).wait()
            pltpu.sync_copy(rows_v, o_hbm.at[pl.ds(off, K)])

    return run(x, idx_flat)


def _ffn_kernel(x_ref, wg_ref, wu_ref, wd_ref, wr_ref, lo_ref, hi_ref):
    e = pl.program_id(0)
    xb = x_ref[...].astype(BF16)
    aff = _router_aff(xb, wr_ref[...])
    lane = lax.broadcasted_iota(I32, aff.shape, 1)
    val = jnp.sum(jnp.where(lane == e, aff, 0.0), axis=1, keepdims=True)
    gate = jnp.dot(xb, wg_ref[...], preferred_element_type=F32)
    up = jnp.dot(xb, wu_ref[...], preferred_element_type=F32)
    hid = gate * _sigmoid(gate) * up
    y = jnp.dot(hid.astype(BF16), wd_ref[...], preferred_element_type=F32) * val
    half = y.shape[1] // 2
    lo_ref[...] = y[:, :half]
    hi_ref[...] = y[:, half:]


def _ffn(xe, wg, wu, wd, wr, cap, tc):
    R, D = xe.shape
    E, _, F = wg.shape
    nb = cap // tc
    half = jax.ShapeDtypeStruct((R, D // 2), F32)
    return pl.pallas_call(
        _ffn_kernel,
        out_shape=(half, half),
        grid=(E, nb),
        in_specs=[pl.BlockSpec((tc, D), lambda e, c: (e * nb + c, 0)),
                  pl.BlockSpec((None, D, F), lambda e, c: (e, 0, 0)),
                  pl.BlockSpec((None, D, F), lambda e, c: (e, 0, 0)),
                  pl.BlockSpec((None, F, D), lambda e, c: (e, 0, 0)),
                  pl.BlockSpec(wr.shape, lambda e, c: (0, 0))],
        out_specs=(pl.BlockSpec((tc, D // 2), lambda e, c: (e * nb + c, 0)),
                   pl.BlockSpec((tc, D // 2), lambda e, c: (e * nb + c, 0))),
        compiler_params=_cparams(("parallel", "arbitrary")),
        name="expert_swiglu",
    )(xe, wg, wu, wd, wr)


def _scatter_add(n_tokens, ye_lo, ye_hi, idx_flat, cap, K):
    R, H = ye_lo.shape
    per_s = cap // SC_SUBCORES
    mesh = plsc.VectorSubcoreMesh(core_axis_name="c", subcore_axis_name="s")
    lanes = SC_LANES

    @functools.partial(
        pl.kernel, mesh=mesh, out_type=(),
        scratch_types=[pltpu.VMEM((K,), I32), pltpu.VMEM((K, H), F32), pltpu.VMEM((K, H), F32),
                       pltpu.SemaphoreType.DMA],
        name="scatter_add_sc",
    )
    def run(ylo_hbm, yhi_hbm, yelo_hbm, yehi_hbm, idx_hbm, idx_v, acc_v, add_v, sem):
        s = lax.axis_index("s")

        def half(y_hbm, ye_hbm):
            @pl.loop(0, N_EXPERTS)
            def _(e):
                @pl.loop(0, per_s // K)
                def _(ch):
                    off = e * cap + s * per_s + ch * K
                    pltpu.sync_copy(idx_hbm.at[pl.ds(off, K)], idx_v)
                    pltpu.sync_copy(ye_hbm.at[pl.ds(off, K)], add_v)
                    pltpu.async_copy(y_hbm.at[idx_v], acc_v, sem).wait()

                    @pl.loop(0, K)
                    def _(r):
                        for j in range(H // lanes):
                            sl = pl.ds(j * lanes, lanes)
                            acc_v[r, sl] = acc_v[r, sl] + add_v[r, sl]

                    pltpu.async_copy(acc_v, y_hbm.at[idx_v], sem).wait()

                plsc.subcore_barrier()

        @pl.when(lax.axis_index("c") == 0)
        def _():
            half(ylo_hbm, yelo_hbm)

        @pl.when(lax.axis_index("c") == 1)
        def _():
            half(yhi_hbm, yehi_hbm)

    y_lo = jax.new_ref(jnp.zeros((n_tokens, H), F32))
    y_hi = jax.new_ref(jnp.zeros((n_tokens, H), F32))
    run(y_lo, y_hi, ye_lo, ye_hi, idx_flat)
    return y_lo[...], y_hi[...]


def _ln2_kernel(x1_ref, ylo_ref, yhi_ref, g_ref, b_ref, x2_ref, xb_ref, *rest, alpha, dils):
    fold_refs, scr_ref = rest[:len(dils)], rest[len(dils)]
    y = jnp.concatenate([ylo_ref[...], yhi_ref[...]], axis=1)
    x2 = _layer_norm(alpha * x1_ref[...] + y, g_ref[...], b_ref[...])
    x2_ref[...] = x2
    xb_ref[...] = x2.astype(BF16)
    tm, D = x2.shape
    for g in range(D // LANES):
        scr_ref[g] = x2[:, g * LANES:(g + 1) * LANES]
    for d, f_ref in zip(dils, fold_refs):
        for g in range(D // LANES):
            for rr in range(d):
                f_ref[rr, :, g * LANES:(g + 1) * LANES] = scr_ref[g, pl.ds(rr, tm // d, stride=d), :].astype(BF16)


def _ln2(x1, y_lo, y_hi, g, b, B, S, tm, alpha, dils):
    N, D = x1.shape
    row = lambda i: (i, 0)
    full = lambda i: (0, 0)
    spb = S // tm
    outs = pl.pallas_call(
        functools.partial(_ln2_kernel, alpha=alpha, dils=dils),
        out_shape=(jax.ShapeDtypeStruct((N, D), F32), jax.ShapeDtypeStruct((N, D), BF16),
                   *[jax.ShapeDtypeStruct((B * d, S // d, D), BF16) for d in dils]),
        grid=(N // tm,),
        in_specs=[pl.BlockSpec((tm, D), row), pl.BlockSpec((tm, D // 2), row), pl.BlockSpec((tm, D // 2), row),
                  pl.BlockSpec(g.shape, full), pl.BlockSpec(b.shape, full)],
        out_specs=(pl.BlockSpec((tm, D), row), pl.BlockSpec((tm, D), row),
                   *[pl.BlockSpec((d, tm // d, D), lambda i: (i // spb, i % spb, 0)) for d in dils]),
        scratch_shapes=[pltpu.VMEM((D // LANES, tm, LANES), F32)],
        compiler_params=_cparams(("parallel",)),
        name="residual_ln2",
    )(x1, y_lo, y_hi, g, b)
    x2, xb = outs[0], outs[1]
    return x2, {1: xb, **{d: f.reshape(N, D) for d, f in zip(dils, outs[2:])}}


def _tiles(B, S):
    N = B * S
    tm = 1024 if S % 1024 == 0 else S
    return dict(tm_proj=tm, tq_b=min(512, S), tq_c=512, tm_merge=512,
                tc=min(512, CAPACITY_FACTOR * N // N_EXPERTS))


def _prep_layer(l, w_in, pool_mix, pool_scale, sink_logit, w_branch, w_out, ln1_g, ln1_b, w_router,
                w_gate_e, w_up_e, w_down_e, ln2_g, ln2_b):
    D = w_in.shape[1]
    a0, qb0 = 0, A_WIDTH
    kb0 = qb0 + B_Q_HEADS * B_HEAD_DIM
    vb0 = kb0 + B_KV_HEADS * B_HEAD_DIM
    qc0 = vb0 + B_KV_HEADS * B_HEAD_DIM
    kc0 = qc0 + len(C_CONFIGS) * C_GROUP_W
    vc0 = kc0 + len(C_CONFIGS) * C_GROUP_W
    zg0 = vc0 + len(C_CONFIGS) * C_GROUP_W
    spread = np.concatenate([h * C_HEAD_DIM + _spread_head_perm(C_HEAD_DIM) for h in range(C_HEADS)])
    wl = w_in[l]
    wr = jnp.zeros((D, LANES), F32).at[:, :N_EXPERTS].set(w_router[l])
    main = [wl[:, qb0:vb0], jnp.concatenate([wl[:, a0:qb0], wl[:, vb0:qc0]], axis=1), 0.5 * wl[:, zg0:]]
    return dict(
        w_main=jnp.concatenate(main, axis=1).astype(BF16), main_widths=tuple(m.shape[1] for m in main),
        w_c=[jnp.concatenate([wl[:, c0 + g * C_GROUP_W:c0 + (g + 1) * C_GROUP_W][:, cols]
                              for c0, cols in ((qc0, spread), (kc0, spread), (vc0, np.arange(C_GROUP_W)))],
                             axis=1).astype(BF16) for g in range(len(C_CONFIGS))],
        pmix=pool_mix[l].astype(BF16), pscale=pool_scale[l][None, :], sink=sink_logit[l],
        wbr=(0.5 * w_branch[l]).astype(BF16), wout=w_out[l].astype(BF16),
        g1=ln1_g[l][None, :], b1=ln1_b[l][None, :], wr=wr.astype(BF16),
        wg=w_gate_e[l].astype(BF16), wu=w_up_e[l].astype(BF16), wd=w_down_e[l].astype(BF16),
        g2=ln2_g[l][None, :], b2=ln2_b[l][None, :],
    )


def _fold_rows(a, B, S, d):
    return a if d == 1 else a.reshape(B, S // d, d, -1).transpose(0, 2, 1, 3).reshape(B * S, -1)


def _layer(x, xbs, p, B, S, alpha, rot_b, rot_c):
    N, D = x.shape
    t = _tiles(B, S)
    cap = CAPACITY_FACTOR * N // N_EXPERTS
    qkb, avb, zg = _proj_main(xbs[1], p["w_main"], p["main_widths"], S, t["tm_proj"], rot_b)
    o_b = _attn_b(qkb, avb, p["sink"], B, S, t["tq_b"])
    o_cs, lses = [], []
    for g, (_, d) in enumerate(C_CONFIGS):
        qkv = _proj_qkv(xbs[d], p["w_c"][g], S, t["tm_proj"], 2 * C_GROUP_W, rot_c[d], f"proj_qkv_{d}")
        qkv = qkv.reshape(B * d, S // d, qkv.shape[1])
        o, lse = _attn_c(qkv, g, B, S, t["tq_c"])
        o_cs.append(o)
        lses.append(lse)
    x1, aff_t = _merge(x, avb, o_b, o_cs, lses, zg, p["pmix"], p["pscale"], p["wbr"], p["wout"], p["g1"], p["b1"],
                       p["wr"], S, t["tm_merge"], alpha)
    idx_flat = _select(aff_t, cap).reshape(-1)
    xe = _gather(x1, idx_flat, SC_GATHER_ROWS)
    ye_lo, ye_hi = _ffn(xe, p["wg"], p["wu"], p["wd"], p["wr"], cap, t["tc"])
    y_lo, y_hi = _scatter_add(N, ye_lo, ye_hi, idx_flat, cap, SC_GATHER_ROWS)
    return _ln2(x1, y_lo, y_hi, p["g2"], p["b2"], B, S, t["tm_proj"], alpha, FOLD_DILATIONS)


def _trunk(x, layers, alpha):
    B, S, D = x.shape
    rot_b = _rot_tables(S, B_HEAD_DIM)
    half_c, tabs_c = _rot_tables_spread(S, C_HEAD_DIM)
    rot_c = {d: (half_c, tuple(_fold_rows(tab, 1, S, d) for tab in tabs_c)) for _, d in C_CONFIGS}
    xf = x.reshape(B * S, D)
    xb = xf.astype(BF16)
    xbs = {d: _fold_rows(xb, B, S, d) for _, d in C_CONFIGS}
    for p in layers:
        xf, xbs = _layer(xf, xbs, p, B, S, alpha, rot_b, rot_c)
    return xf.reshape(B, S, D)


def kernel(x_prompt, x_sample, w_in, pool_mix, pool_scale, sink_logit, w_branch, w_out, ln1_g, ln1_b, w_router,
           w_gate_e, w_up_e, w_down_e, ln2_g, ln2_b):
    depth = w_in.shape[0]
    alpha = (2 * depth) ** 0.25
    layers = [_prep_layer(l, w_in, pool_mix, pool_scale, sink_logit, w_branch, w_out, ln1_g, ln1_b, w_router,
                          w_gate_e, w_up_e, w_down_e, ln2_g, ln2_b) for l in range(depth)]
    return (_trunk(x_prompt, layers, alpha), _trunk(x_sample, layers, alpha))
```

```python
import functools

import numpy as np
import jax
import jax.numpy as jnp
from jax import lax
from jax.experimental import pallas as pl
from jax.experimental.pallas import tpu as pltpu
from jax.experimental.pallas import tpu_sc as plsc

F32 = jnp.float32
BF16 = jnp.bfloat16
I32 = jnp.int32

POOL_WINDOWS = (2, 4, 8, 16)
POOL_CH = 128
A_WIDTH = 512
B_Q_HEADS = 8
B_KV_HEADS = 2
B_HEAD_DIM = 64
B_HALF_WINDOW = 128
C_CONFIGS = ((128, 1), (512, 4), (2048, 16))
C_HEADS = 4
C_HEAD_DIM = 128
C_GROUP_W = C_HEADS * C_HEAD_DIM
FOLD_DILATIONS = tuple(d for _, d in C_CONFIGS if d > 1)
N_EXPERTS = 16
CAPACITY_FACTOR = 2
ROPE_THETA = 500000.0
ROT_DIV = 4
LN_EPS = 1e-5
NEG_INF = -1e30
LOG2E = 1.4426950408889634
LN2 = 0.6931471805599453

LANES = 128
SUBLANES = 8
VMEM_LIMIT = 56 * 1024 * 1024
SC_CORES = 2
SC_SUBCORES = 16
SC_LANES = 16
SC_GATHER_ROWS = 64
MXU_WIDTH = 256
MERGE_SLAB = 256
ATTN_STRIP = 32
ATTN_UNROLL = 8
PERM_BLOCK = MXU_WIDTH


def _cparams(sem):
    return pltpu.CompilerParams(dimension_semantics=sem, vmem_limit_bytes=VMEM_LIMIT)


def _col_chunks(width):
    return [(c, min(c + MXU_WIDTH, width)) for c in range(0, width, MXU_WIDTH)]


def _rotary(acc, c, s1, s2, rot_half):
    outs = []
    for g in range(acc.shape[1] // LANES):
        a = acc[:, g * LANES:(g + 1) * LANES]
        if 2 * rot_half == LANES:
            outs.append(a * c + pltpu.roll(a, rot_half, 1) * s1)
        else:
            outs.append(a * c + pltpu.roll(a, LANES - rot_half, 1) * s1 + pltpu.roll(a, rot_half, 1) * s2)
    return outs[0] if len(outs) == 1 else jnp.concatenate(outs, axis=1)


def _proj_main_kernel(x_ref, w_ref, cos_ref, s1_ref, s2_ref, qk_ref, av_ref, zg_ref, *, rot_half):
    x = x_ref[...]
    base = 0
    for o_ref, rotate in ((qk_ref, True), (av_ref, False), (zg_ref, False)):
        for c0, c1 in _col_chunks(o_ref.shape[1]):
            acc = jnp.dot(x, w_ref[:, base + c0:base + c1], preferred_element_type=F32)
            if rotate:
                acc = _rotary(acc, cos_ref[...], s1_ref[...], s2_ref[...], rot_half)
            o_ref[:, c0:c1] = acc.astype(o_ref.dtype)
        base += o_ref.shape[1]


def _proj_main(x, w, widths, S, tm, rot):
    N, K = x.shape
    rot_half, tabs = rot
    spb = S // tm
    row = lambda i: (i, 0)
    tab_spec = pl.BlockSpec((tm, LANES), lambda i: (i % spb, 0))
    dtypes = (BF16, F32, BF16)
    return pl.pallas_call(
        functools.partial(_proj_main_kernel, rot_half=rot_half),
        out_shape=tuple(jax.ShapeDtypeStruct((N, c), dt) for c, dt in zip(widths, dtypes)),
        grid=(N // tm,),
        in_specs=[pl.BlockSpec((tm, K), row), pl.BlockSpec(w.shape, lambda i: (0, 0)), tab_spec, tab_spec, tab_spec],
        out_specs=tuple(pl.BlockSpec((tm, c), row) for c in widths),
        compiler_params=_cparams(("parallel",)),
        name="proj_main",
    )(x, w, *tabs)


def _fold_perm(d, inverse=False):
    i = lax.broadcasted_iota(I32, (PERM_BLOCK, PERM_BLOCK), 1 if inverse else 0)
    j = lax.broadcasted_iota(I32, (PERM_BLOCK, PERM_BLOCK), 0 if inverse else 1)
    per = PERM_BLOCK // d
    return jnp.where(j == (i % per) * d + i // per, 1.0, 0.0).astype(BF16)


def _proj_qkv_kernel(x_ref, w_ref, cos_ref, s1_ref, s2_ref, o_ref, *, rot_half, rot_cols):
    x = x_ref[...]
    for c0, c1 in _col_chunks(w_ref.shape[1]):
        acc = jnp.dot(x, w_ref[:, c0:c1], preferred_element_type=F32)
        if c0 < rot_cols:
            acc = _rotary(acc, cos_ref[...], s1_ref[...], s2_ref[...], rot_half)
        o_ref[:, c0:c1] = acc.astype(BF16)


def _proj_qkv(x, w, S, tm, rot_cols, rot, name):
    N, K = x.shape
    C = w.shape[1]
    rot_half, tabs = rot
    spb = S // tm
    tab_spec = pl.BlockSpec((tm, LANES), lambda i: (i % spb, 0))
    return pl.pallas_call(
        functools.partial(_proj_qkv_kernel, rot_half=rot_half, rot_cols=rot_cols),
        out_shape=jax.ShapeDtypeStruct((N, C), BF16),
        grid=(N // tm,),
        in_specs=[pl.BlockSpec((tm, K), lambda i: (i, 0)), pl.BlockSpec((K, C), lambda i: (0, 0)),
                  tab_spec, tab_spec, tab_spec],
        out_specs=pl.BlockSpec((tm, C), lambda i: (i, 0)),
        compiler_params=_cparams(("parallel",)),
        name=name,
    )(x, w, *tabs)


def _rot_tables(S, head_dim):
    rot = head_dim // ROT_DIV
    half = rot // 2
    inv = 1.0 / (ROPE_THETA ** (jnp.arange(0, rot, 2, dtype=F32) / rot))
    ang = jnp.arange(S, dtype=F32)[:, None] * inv[None, :]
    cos, sin = jnp.cos(ang), jnp.sin(ang)
    d = np.arange(LANES) % head_dim
    first = d < half
    second = (d >= half) & (d < rot)
    src = np.where(first, d, np.where(second, d - half, 0))
    cos_t = jnp.where(first | second, cos[:, src], 1.0)
    s1_t = jnp.where(first, -sin[:, src], 0.0)
    s2_t = jnp.where(second, sin[:, src], 0.0)
    return half, (cos_t, s1_t, s2_t)


def _spread_head_perm(head_dim):
    half = head_dim // ROT_DIV // 2
    mid = LANES // 2
    return np.concatenate([np.arange(0, half), np.arange(2 * half, mid + half), np.arange(half, 2 * half),
                           np.arange(mid + half, head_dim)])


def _rot_tables_spread(S, head_dim):
    half, (cos_t, s1_t, s2_t) = _rot_tables(S, head_dim)
    perm = _spread_head_perm(head_dim)
    return LANES // 2, (cos_t[:, perm], (s1_t + s2_t)[:, perm], s2_t)


def _band_bias(bias_ref, W, q0, L, sub):
    tq, nk = bias_ref.shape
    row = lax.broadcasted_iota(I32, (tq, nk), 0)
    col = lax.broadcasted_iota(I32, (tq, nk), 1)
    first = (row // sub) * sub
    kpos = q0 - W + first + col
    valid = (jnp.abs(col - W - (row - first)) <= W) & (kpos >= 0) & (kpos < L)
    bias_ref[...] = jnp.where(valid, 0.0, NEG_INF)


def _softmax_strips(s_ref, p_ref, den_ref, max_ref, bias_ref, *, unroll, sink_fn=None):
    rows, nk = s_ref.shape
    tq = bias_ref.shape[0]

    def strip(t):
        r0 = pl.multiple_of(t * ATTN_STRIP, ATTN_STRIP)
        rows_sl = pl.ds(r0, ATTN_STRIP)
        s = s_ref[rows_sl, :] + bias_ref[pl.ds(pl.multiple_of(r0 % tq, ATTN_STRIP), ATTN_STRIP), :]
        return r0, rows_sl, s

    def row_max(t, carry):
        r0, rows_sl, s = strip(t)
        m = jnp.max(s, axis=1, keepdims=True)
        if sink_fn is not None:
            m = jnp.maximum(m, sink_fn(r0 // tq) * LOG2E)
        max_ref[rows_sl, :] = jnp.broadcast_to(m, (ATTN_STRIP, LANES))
        return carry

    def numerators(t, carry):
        r0, rows_sl, s = strip(t)
        m = max_ref[rows_sl, :]
        p = jnp.exp2(s - jnp.concatenate([m] * (nk // LANES), axis=1))
        den = jnp.broadcast_to(jnp.sum(p, axis=1, keepdims=True), (ATTN_STRIP, LANES))
        if sink_fn is not None:
            den = den + jnp.exp2(sink_fn(r0 // tq) * LOG2E - m)
        p_ref[rows_sl, :] = p.astype(BF16)
        den_ref[rows_sl, :] = den
        return carry

    lax.fori_loop(0, rows // ATTN_STRIP, row_max, 0, unroll=unroll)
    lax.fori_loop(0, rows // ATTN_STRIP, numerators, 0, unroll=unroll)


def _attn_b_kernel(sink_ref, q_ref, kp_ref, km_ref, kn_ref, vp_ref, vm_ref, vn_ref, o_ref,
                   s_ref, p_ref, den_ref, max_ref, bias_ref, *, S, TQ):
    W = B_HALF_WINDOW
    G = B_Q_HEADS // B_KV_HEADS
    sub, span = W, 3 * W
    nb = TQ // sub
    q0 = pl.program_id(1) * TQ
    _band_bias(bias_ref, W, q0, S, sub)
    k_all = jnp.concatenate([kp_ref[...], km_ref[...], kn_ref[...]], axis=0)
    v_all = jnp.concatenate([vp_ref[...], vm_ref[...], vn_ref[...]], axis=0).astype(BF16)
    q = q_ref[...]
    outs = []
    for j in range(B_KV_HEADS):
        kj = k_all[:, j * B_HEAD_DIM:(j + 1) * B_HEAD_DIM]
        vj = v_all[:, j * B_HEAD_DIM:(j + 1) * B_HEAD_DIM]
        for b in range(nb):
            qs = jnp.concatenate([q[b * sub:(b + 1) * sub, (G * j + g) * B_HEAD_DIM:(G * j + g + 1) * B_HEAD_DIM]
                                  for g in range(G)], axis=0)
            s = lax.dot_general(qs, kj[b * sub:b * sub + span], (((1,), (1,)), ((), ())),
                                preferred_element_type=F32) * (B_HEAD_DIM ** -0.5 * LOG2E)
            for g in range(G):
                s_ref[j, g * TQ + b * sub:g * TQ + (b + 1) * sub, :] = s[g * sub:(g + 1) * sub]
        _softmax_strips(s_ref.at[j], p_ref.at[j], den_ref.at[j], max_ref.at[j], bias_ref, unroll=True,
                        sink_fn=lambda g, j=j: sink_ref[G * j + g])
        pv = []
        for b in range(nb):
            pb = jnp.concatenate([p_ref[j, g * TQ + b * sub:g * TQ + (b + 1) * sub, :] for g in range(G)], axis=0)
            pv.append(jnp.dot(pb, vj[b * sub:b * sub + span], preferred_element_type=F32))
        for g in range(G):
            o = jnp.concatenate([pv[b][g * sub:(g + 1) * sub] for b in range(nb)], axis=0)
            outs.append(o / den_ref[j, g * TQ:(g + 1) * TQ, 0:1])
    o_ref[...] = jnp.concatenate(outs, axis=1).astype(o_ref.dtype)


def _attn_b(qk, av, sink, B, S, TQ):
    N = B * S
    W = B_HALF_WINDOW
    nq = S // TQ
    r = TQ // W
    kcol = (B_Q_HEADS * B_HEAD_DIM) // LANES
    last = N // W - 1
    rows = (B_Q_HEADS // B_KV_HEADS) * TQ

    def prev(b, i):
        return (jnp.maximum((b * nq + i) * r - 1, 0), kcol)

    def main(b, i):
        return (b * nq + i, kcol)

    def nxt(b, i):
        return (jnp.minimum((b * nq + i + 1) * r, last), kcol)

    return pl.pallas_call(
        functools.partial(_attn_b_kernel, S=S, TQ=TQ),
        out_shape=jax.ShapeDtypeStruct((N, B_Q_HEADS * B_HEAD_DIM), BF16),
        grid=(B, nq),
        in_specs=[pl.BlockSpec(memory_space=pltpu.SMEM),
                  pl.BlockSpec((TQ, B_Q_HEADS * B_HEAD_DIM), lambda b, i: (b * nq + i, 0)),
                  pl.BlockSpec((W, LANES), prev), pl.BlockSpec((TQ, LANES), main), pl.BlockSpec((W, LANES), nxt),
                  pl.BlockSpec((W, LANES), prev), pl.BlockSpec((TQ, LANES), main), pl.BlockSpec((W, LANES), nxt)],
        out_specs=pl.BlockSpec((TQ, B_Q_HEADS * B_HEAD_DIM), lambda b, i: (b * nq + i, 0)),
        scratch_shapes=[pltpu.VMEM((B_KV_HEADS, rows, 3 * W), F32),
                        pltpu.VMEM((B_KV_HEADS, rows, 3 * W), BF16),
                        pltpu.VMEM((B_KV_HEADS, rows, LANES), F32), pltpu.VMEM((B_KV_HEADS, rows, LANES), F32),
                        pltpu.VMEM((TQ, 3 * W), F32)],
        compiler_params=_cparams(("parallel", "arbitrary")),
        name="attn_window",
    )(sink, qk, qk, qk, qk, av, av, av)


def _attn_c_kernel(q_ref, kp_ref, km_ref, kn_ref, vp_ref, vm_ref, vn_ref, o_ref, lse_ref,
                   s_ref, p_ref, den_ref, max_ref, bias_ref, *, M, TQ, W):
    sub, span = 2 * W, 4 * W
    q0 = pl.program_id(1) * TQ
    _band_bias(bias_ref, W, q0, M, sub)
    k_all = jnp.concatenate([kp_ref[...], km_ref[...], kn_ref[...]], axis=0)
    v_all = jnp.concatenate([vp_ref[...], vm_ref[...], vn_ref[...]], axis=0)
    q = q_ref[...]
    lane = lax.broadcasted_iota(I32, (TQ, LANES), 1)
    lse_tile = jnp.zeros((TQ, LANES), F32)
    outs = []
    for h in range(C_HEADS):
        sl = slice(h * C_HEAD_DIM, (h + 1) * C_HEAD_DIM)
        for b in range(TQ // sub):
            s_ref[h, b * sub:(b + 1) * sub, :] = lax.dot_general(
                q[b * sub:(b + 1) * sub, sl], k_all[b * sub:b * sub + span, sl], (((1,), (1,)), ((), ())),
                preferred_element_type=F32) * (C_HEAD_DIM ** -0.5 * LOG2E)
        _softmax_strips(s_ref.at[h], p_ref.at[h], den_ref.at[h], max_ref.at[h], bias_ref, unroll=True)
        den = den_ref[h]
        pv = [jnp.dot(p_ref[h, b * sub:(b + 1) * sub, :], v_all[b * sub:b * sub + span, sl],
                      preferred_element_type=F32) for b in range(TQ // sub)]
        outs.append(jnp.concatenate(pv, axis=0) / den[:, 0:1])
        lse_tile = jnp.where(lane == h, max_ref[h] * LN2 + jnp.log(den), lse_tile)
    o_ref[...] = jnp.concatenate(outs, axis=1).astype(o_ref.dtype)
    lse_ref[...] = lse_tile


def _attn_c(qkv, g, B, S, TQ):
    window, d = C_CONFIGS[g]
    W = window // (2 * d)
    M = S // d
    TQ = min(TQ, M)
    r = TQ // W
    GW = C_GROUP_W
    last = M // W - 1

    def mk(col):
        return [pl.BlockSpec((None, W, GW), lambda b, i: (b, jnp.maximum(i * r - 1, 0), col)),
                pl.BlockSpec((None, TQ, GW), lambda b, i: (b, i, col)),
                pl.BlockSpec((None, W, GW), lambda b, i: (b, jnp.minimum((i + 1) * r, last), col))]

    return pl.pallas_call(
        functools.partial(_attn_c_kernel, M=M, TQ=TQ, W=W),
        out_shape=(jax.ShapeDtypeStruct((B * d, M, GW), BF16), jax.ShapeDtypeStruct((B * d, M, LANES), F32)),
        grid=(B * d, M // TQ),
        in_specs=[pl.BlockSpec((None, TQ, GW), lambda b, i: (b, i, 0))] + mk(1) + mk(2),
        out_specs=(pl.BlockSpec((None, TQ, GW), lambda b, i: (b, i, 0)),
                   pl.BlockSpec((None, TQ, LANES), lambda b, i: (b, i, 0))),
        scratch_shapes=[pltpu.VMEM((C_HEADS, TQ, 4 * W), F32), pltpu.VMEM((C_HEADS, TQ, 4 * W), BF16),
                        pltpu.VMEM((C_HEADS, TQ, LANES), F32), pltpu.VMEM((C_HEADS, TQ, LANES), F32),
                        pltpu.VMEM((TQ, 4 * W), F32)],
        compiler_params=_cparams(("parallel", "arbitrary")),
        name=f"attn_dilated_{d}",
    )(qkv, qkv, qkv, qkv, qkv, qkv, qkv)


def _sigmoid(z):
    return 0.5 * jnp.tanh(0.5 * z) + 0.5


def _layer_norm(h, g, b):
    mu = jnp.mean(h, axis=-1, keepdims=True)
    var = jnp.mean(jnp.square(h - mu), axis=-1, keepdims=True)
    return (h - mu) * lax.rsqrt(var + LN_EPS) * g + b


def _router_aff(xb, wr):
    logits = jnp.dot(xb, wr, preferred_element_type=F32)
    lane = lax.broadcasted_iota(I32, logits.shape, 1)
    logits = jnp.where(lane < N_EXPERTS, logits, NEG_INF)
    e = jnp.exp(logits - jnp.max(logits, axis=1, keepdims=True))
    return e / jnp.sum(e, axis=1, keepdims=True)


def _unfold(ref, scr_ref):
    d, rows, _ = ref.shape
    if d == 1:
        return ref[0]
    for rr in range(d):
        scr_ref[pl.ds(rr, rows, stride=d), :] = ref[rr]
    return scr_ref[...]


def _unfold_bf16(ref):
    d, rows, _ = ref.shape
    if d == 1:
        return ref[0].astype(F32)
    per = PERM_BLOCK // d
    perm = _fold_perm(d, inverse=True)
    outs = []
    for blk in range(d * rows // PERM_BLOCK):
        stacked = jnp.concatenate([ref[rr, blk * per:(blk + 1) * per, :] for rr in range(d)], axis=0)
        outs.append(jnp.dot(perm, stacked, preferred_element_type=F32))
    return outs[0] if len(outs) == 1 else jnp.concatenate(outs, axis=0)


def _merge_kernel(x_ref, a_ref, ap_ref, an_ref, ob_ref, oc0_ref, oc1_ref, oc2_ref, l0_ref, l1_ref, l2_ref,
                  zg_ref, pmix_ref, pscale_ref, wbr_ref, wout_ref, g_ref, b_ref, wr_ref,
                  x1_ref, aff_ref, sl1_ref, sl2_ref, *, S, tm, alpha):
    HW = max(POOL_WINDOWS) // 2
    L = tm + 2 * HW
    pos0 = (pl.program_id(0) * tm) % S
    xa = jnp.concatenate([ap_ref[...], a_ref[...], an_ref[...]], axis=0)
    xpos = pos0 - HW + lax.broadcasted_iota(I32, (L, 1), 0)
    xa = jnp.where((xpos >= 0) & (xpos < S), xa, 0.0)
    sums = {}
    t, w = xa, 1
    while w < max(POOL_WINDOWS):
        t = t + pltpu.roll(t, L - w, 0)
        w *= 2
        sums[w] = t
    pos = pos0 + lax.broadcasted_iota(I32, (tm, 1), 0)
    mixed = []
    for g, w in enumerate(POOL_WINDOWS):
        cs = slice(g * POOL_CH, (g + 1) * POOL_CH)
        off = HW - w // 2
        sw = sums[w][:, cs]
        if off:
            sw = pltpu.roll(sw, L - off, 0)
        sw = sw[:tm]
        cnt = (jnp.clip(pos + w // 2, 0, S) - jnp.clip(pos - w // 2, 0, S)).astype(F32)
        pooled = sw / cnt - a_ref[:, cs]
        mixed.append(jnp.dot(pooled.astype(BF16), pmix_ref[g], preferred_element_type=F32) * pscale_ref[:, cs])
    o_a = jnp.concatenate(mixed, axis=1)
    ls = [_unfold(l0_ref, None), _unfold(l1_ref, sl1_ref), _unfold(l2_ref, sl2_ref)]
    mx = jnp.maximum(jnp.maximum(ls[0], ls[1]), ls[2])
    es = [jnp.exp(l - mx) for l in ls]
    tot = es[0] + es[1] + es[2]
    ocs = [_unfold_bf16(oc0_ref), _unfold_bf16(oc1_ref), _unfold_bf16(oc2_ref)]
    pieces = []
    for h in range(C_HEADS):
        sl = slice(h * C_HEAD_DIM, (h + 1) * C_HEAD_DIM)
        acc = None
        for gi in range(3):
            term = (es[gi] / tot)[:, h:h + 1] * ocs[gi][:, sl].astype(F32)
            acc = term if acc is None else acc + term
        pieces.append(acc)
    o_c = jnp.concatenate(pieces, axis=1)
    o_a, o_c = o_a.astype(BF16), o_c.astype(BF16)
    D = x_ref.shape[1]
    for r0 in range(0, tm, MERGE_SLAB):
        rs = slice(r0, r0 + MERGE_SLAB)
        branches = [o_a[rs], ob_ref[rs, :], o_c[rs]]
        merged = []
        for c0, c1 in _col_chunks(D):
            acc = None
            for k in range(3):
                gate2 = jnp.tanh(zg_ref[rs, k * D + c0:k * D + c1].astype(F32)) + 1.0
                term = gate2 * jnp.dot(branches[k], wbr_ref[k, :, c0:c1], preferred_element_type=F32)
                acc = term if acc is None else acc + term
            merged.append(acc.astype(BF16))
        merged = jnp.concatenate(merged, axis=1)
        pre = [alpha * x_ref[rs, c0:c1] + jnp.dot(merged, wout_ref[:, c0:c1], preferred_element_type=F32)
               for c0, c1 in _col_chunks(D)]
        x1 = _layer_norm(jnp.concatenate(pre, axis=1), g_ref[...], b_ref[...])
        x1_ref[rs, :] = x1
        aff = _router_aff(x1.astype(BF16), wr_ref[...])
        aff_ref[:, rs] = aff.T[:N_EXPERTS]


def _merge(x, av, o_b, o_cs, lses, vz, pmix, pscale, wbr, wout, g, b, wr, S, tm, alpha):
    N, D = x.shape
    HW = max(POOL_WINDOWS) // 2
    r = tm // HW
    last = N // HW - 1
    row = lambda i: (i, 0)
    full2 = lambda i: (0, 0)
    full3 = lambda i: (0, 0, 0)
    spb = S // tm
    dils = [d for _, d in C_CONFIGS]

    def folded(width):
        return [pl.BlockSpec((d, tm // d, width), lambda i: (i // spb, i % spb, 0)) for d in dils]

    in_specs = [
        pl.BlockSpec((tm, D), row),
        pl.BlockSpec((tm, A_WIDTH), row),
        pl.BlockSpec((HW, A_WIDTH), lambda i: (jnp.maximum(i * r - 1, 0), 0)),
        pl.BlockSpec((HW, A_WIDTH), lambda i: (jnp.minimum((i + 1) * r, last), 0)),
        pl.BlockSpec((tm, B_Q_HEADS * B_HEAD_DIM), row),
        *folded(C_GROUP_W),
        *folded(LANES),
        pl.BlockSpec((tm, 3 * D), row),
        pl.BlockSpec(pmix.shape, full3),
        pl.BlockSpec(pscale.shape, full2),
        pl.BlockSpec(wbr.shape, full3),
        pl.BlockSpec(wout.shape, full2),
        pl.BlockSpec(g.shape, full2), pl.BlockSpec(b.shape, full2),
        pl.BlockSpec(wr.shape, full2),
    ]
    return pl.pallas_call(
        functools.partial(_merge_kernel, S=S, tm=tm, alpha=alpha),
        out_shape=(jax.ShapeDtypeStruct((N, D), F32), jax.ShapeDtypeStruct((N_EXPERTS, N), F32)),
        grid=(N // tm,),
        in_specs=in_specs,
        out_specs=(pl.BlockSpec((tm, D), row), pl.BlockSpec((N_EXPERTS, tm), lambda i: (0, i))),
        scratch_shapes=[pltpu.VMEM((tm, LANES), F32)] * 2,
        compiler_params=_cparams(("parallel",)),
        name="merge_ln1_router",
    )(x, av, av, av, o_b, *o_cs, *lses, vz, pmix, pscale, wbr, wout, g, b, wr)


def _count(mask):
    return jnp.sum(jnp.sum(jnp.where(mask, 1.0, 0.0), axis=1), axis=1, keepdims=True)


def _thr_kernel(aff_ref, t_ref, need_ref, *, cap):
    bits = pltpu.bitcast(aff_ref[...], I32)

    def body(k, t):
        cand = t | jnp.left_shift(jnp.int32(1), 30 - k)
        return jnp.where(_count(bits >= cand[:, :, None]) >= cap, cand, t)

    t = lax.fori_loop(0, 31, body, jnp.zeros((N_EXPERTS, 1), I32))
    need = cap - _count(bits > t[:, :, None])
    t_ref[...] = jnp.broadcast_to(t, t_ref.shape)
    need_ref[...] = jnp.broadcast_to(need.astype(I32), need_ref.shape)


def _pos_kernel(t_ref, need_ref, aff_ref, clt_ref, ext_ref, lo_ref, hi_ref, *, NC):
    e = pl.program_id(0)
    bits = pltpu.bitcast(aff_ref[...], I32)
    t = t_ref[e]
    need = need_ref[e].astype(F32)
    upper = jnp.where(lax.broadcasted_iota(I32, (LANES, LANES), 0) <= lax.broadcasted_iota(I32, (LANES, LANES), 1),
                      1.0, 0.0).astype(BF16)
    ones = jnp.ones((LANES, LANES), BF16)
    before = jnp.where(lax.broadcasted_iota(I32, (NC, NC), 1) < lax.broadcasted_iota(I32, (NC, NC), 0),
                       1.0, 0.0).astype(BF16)

    def cums(xf):
        xb = xf.astype(BF16)
        local = jnp.dot(xb, upper, preferred_element_type=F32)
        tot = jnp.dot(xb, ones, preferred_element_type=F32)
        offs = jnp.dot(before, tot.astype(BF16), preferred_element_type=F32)
        return local, tot, offs

    eq = jnp.where(bits == t, 1.0, 0.0)
    local, _, offs = cums(eq)
    sel = (bits > t) | ((bits == t) & (local + offs - eq < need))
    local, tot, offs = cums(jnp.where(sel, 1.0, 0.0))
    lo_ref[...] = offs
    hi_ref[...] = offs + tot
    clt_ref[...] = local.T
    chunk = lax.broadcasted_iota(I32, (NC, LANES), 0)
    lane = lax.broadcasted_iota(I32, (NC, LANES), 1)
    offs_i = offs.astype(I32)
    ext = jnp.where(lane == 0, offs_i >> 8,
                    jnp.where(lane == 1, offs_i & 255,
                              jnp.where(lane == 2, chunk >> 8, jnp.where(lane == 3, chunk & 255, 0))))
    ext_ref[...] = ext.astype(F32).T[:SUBLANES]


def _idx_kernel(clt_ref, ext_ref, lo_ref, hi_ref, idx_ref, *, SB):
    s = (pl.program_id(1) * SB + lax.broadcasted_iota(I32, (1, SB), 1)).astype(F32)
    onehot = jnp.where((lo_ref[:, 0:1] <= s) & (s < hi_ref[:, 0:1]), 1.0, 0.0).astype(BF16)
    cg = jnp.dot(clt_ref[...].astype(BF16), onehot, preferred_element_type=F32)
    ex = jnp.dot(ext_ref[...].astype(BF16), onehot, preferred_element_type=F32)
    slot_lo = ex[0:1] * 256.0 + ex[1:2]
    chunk = ex[2:3] * 256.0 + ex[3:4]
    within = jnp.sum(jnp.where(cg <= s - slot_lo, 1.0, 0.0), axis=0, keepdims=True)
    idx_ref[...] = (chunk * LANES + within).astype(I32)


def _select(aff_t, cap):
    E, N = aff_t.shape
    NC = N // LANES
    aff3 = aff_t.reshape(E, NC, LANES)
    t, need = pl.pallas_call(
        functools.partial(_thr_kernel, cap=cap),
        out_shape=(jax.ShapeDtypeStruct((E, LANES), I32), jax.ShapeDtypeStruct((E, LANES), I32)),
        compiler_params=pltpu.CompilerParams(vmem_limit_bytes=VMEM_LIMIT),
        name="select_threshold",
    )(aff3)
    per_e = lambda e: (e, 0, 0)
    smem = pl.BlockSpec(memory_space=pltpu.SMEM)
    clt, ext, lo, hi = pl.pallas_call(
        functools.partial(_pos_kernel, NC=NC),
        out_shape=(jax.ShapeDtypeStruct((E, LANES, NC), F32), jax.ShapeDtypeStruct((E, SUBLANES, NC), F32),
                   jax.ShapeDtypeStruct((E, NC, LANES), F32), jax.ShapeDtypeStruct((E, NC, LANES), F32)),
        grid=(E,),
        in_specs=[smem, smem, pl.BlockSpec((None, NC, LANES), per_e)],
        out_specs=(pl.BlockSpec((None, LANES, NC), per_e), pl.BlockSpec((None, SUBLANES, NC), per_e),
                   pl.BlockSpec((None, NC, LANES), per_e), pl.BlockSpec((None, NC, LANES), per_e)),
        compiler_params=_cparams(("parallel",)),
        name="select_prefix",
    )(t[:, 0], need[:, 0], aff3)
    SB = min(cap, 1024)
    per_e2 = lambda e, s: (e, 0, 0)
    idx = pl.pallas_call(
        functools.partial(_idx_kernel, SB=SB),
        out_shape=jax.ShapeDtypeStruct((E, 1, cap), I32),
        grid=(E, cap // SB),
        in_specs=[pl.BlockSpec((None, LANES, NC), per_e2), pl.BlockSpec((None, SUBLANES, NC), per_e2),
                  pl.BlockSpec((None, NC, LANES), per_e2), pl.BlockSpec((None, NC, LANES), per_e2)],
        out_specs=pl.BlockSpec((None, 1, SB), lambda e, s: (e, 0, s)),
        compiler_params=_cparams(("parallel", "arbitrary")),
        name="select_index",
    )(clt, ext, lo, hi)
    return idx.reshape(E, cap)


def _gather(x, idx_flat, K):
    N, D = x.shape
    R = idx_flat.shape[0]
    workers = SC_CORES * SC_SUBCORES
    per_w = R // workers
    mesh = plsc.VectorSubcoreMesh(core_axis_name="c", subcore_axis_name="s")

    @functools.partial(
        pl.kernel, mesh=mesh,
        out_type=jax.ShapeDtypeStruct((R, D), x.dtype),
        scratch_types=[pltpu.VMEM((K,), I32), pltpu.VMEM((K, D), x.dtype), pltpu.SemaphoreType.DMA],
        name="gather_rows_sc",
    )
    def run(x_hbm, idx_hbm, o_hbm, idx_v, rows_v, sem):
        base = (lax.axis_index("s") * SC_CORES + lax.axis_index("c")) * per_w

        @pl.loop(0, per_w // K)
        def _(c):
            off = base + c * K
            pltpu.sync_copy(idx_hbm.at[pl.ds(off, K)], idx_v)
            pltpu.async_copy(x_hbm.at[idx_v], rows_v, sem).wait()
            pltpu.sync_copy(rows_v, o_hbm.at[pl.ds(off, K)])

    return run(x, idx_flat)


def _ffn_kernel(x_ref, wg_ref, wu_ref, wd_ref, wr_ref, lo_ref, hi_ref):
    e = pl.program_id(0)
    xb = x_ref[...].astype(BF16)
    aff = _router_aff(xb, wr_ref[...])
    lane = lax.broadcasted_iota(I32, aff.shape, 1)
    val = jnp.sum(jnp.where(lane == e, aff, 0.0), axis=1, keepdims=True)
    gate = jnp.dot(xb, wg_ref[...], preferred_element_type=F32)
    up = jnp.dot(xb, wu_ref[...], preferred_element_type=F32)
    hid = gate * _sigmoid(gate) * up
    y = jnp.dot(hid.astype(BF16), wd_ref[...], preferred_element_type=F32) * val
    half = y.shape[1] // 2
    lo_ref[...] = y[:, :half]
    hi_ref[...] = y[:, half:]


def _ffn(xe, wg, wu, wd, wr, cap, tc):
    R, D = xe.shape
    E, _, F = wg.shape
    nb = cap // tc
    half = jax.ShapeDtypeStruct((R, D // 2), F32)
    return pl.pallas_call(
        _ffn_kernel,
        out_shape=(half, half),
        grid=(E, nb),
        in_specs=[pl.BlockSpec((tc, D), lambda e, c: (e * nb + c, 0)),
                  pl.BlockSpec((None, D, F), lambda e, c: (e, 0, 0)),
                  pl.BlockSpec((None, D, F), lambda e, c: (e, 0, 0)),
                  pl.BlockSpec((None, F, D), lambda e, c: (e, 0, 0)),
                  pl.BlockSpec(wr.shape, lambda e, c: (0, 0))],
        out_specs=(pl.BlockSpec((tc, D // 2), lambda e, c: (e * nb + c, 0)),
                   pl.BlockSpec((tc, D // 2), lambda e, c: (e * nb + c, 0))),
        compiler_params=_cparams(("parallel", "arbitrary")),
        name="expert_swiglu",
    )(xe, wg, wu, wd, wr)


def _scatter_add(n_tokens, ye_lo, ye_hi, idx_flat, cap, K):
    R, H = ye_lo.shape
    per_s = cap // SC_SUBCORES
    mesh = plsc.VectorSubcoreMesh(core_axis_name="c", subcore_axis_name="s")
    lanes = SC_LANES

    @functools.partial(
        pl.kernel, mesh=mesh, out_type=(),
        scratch_types=[pltpu.VMEM((K,), I32), pltpu.VMEM((K, H), F32), pltpu.VMEM((K, H), F32),
                       pltpu.SemaphoreType.DMA],
        name="scatter_add_sc",
    )
    def run(ylo_hbm, yhi_hbm, yelo_hbm, yehi_hbm, idx_hbm, idx_v, acc_v, add_v, sem):
        s = lax.axis_index("s")

        def half(y_hbm, ye_hbm):
            @pl.loop(0, N_EXPERTS)
            def _(e):
                @pl.loop(0, per_s // K)
                def _(ch):
                    off = e * cap + s * per_s + ch * K
                    pltpu.sync_copy(idx_hbm.at[pl.ds(off, K)], idx_v)
                    pltpu.sync_copy(ye_hbm.at[pl.ds(off, K)], add_v)
                    pltpu.async_copy(y_hbm.at[idx_v], acc_v, sem).wait()

                    @pl.loop(0, K)
                    def _(r):
                        for j in range(H // lanes):
                            sl = pl.ds(j * lanes, lanes)
                            acc_v[r, sl] = acc_v[r, sl] + add_v[r, sl]

                    pltpu.async_copy(acc_v, y_hbm.at[idx_v], sem).wait()

                plsc.subcore_barrier()

        @pl.when(lax.axis_index("c") == 0)
        def _():
            half(ylo_hbm, yelo_hbm)

        @pl.when(lax.axis_index("c") == 1)
        def _():
            half(yhi_hbm, yehi_hbm)

    y_lo = jax.new_ref(jnp.zeros((n_tokens, H), F32))
    y_hi = jax.new_ref(jnp.zeros((n_tokens, H), F32))
    run(y_lo, y_hi, ye_lo, ye_hi, idx_flat)
    return y_lo[...], y_hi[...]


def _ln2_kernel(x1_ref, ylo_ref, yhi_ref, g_ref, b_ref, x2_ref, xb_ref, *rest, alpha, dils):
    fold_refs, scr_ref = rest[:len(dils)], rest[len(dils)]
    y = jnp.concatenate([ylo_ref[...], yhi_ref[...]], axis=1)
    x2 = _layer_norm(alpha * x1_ref[...] + y, g_ref[...], b_ref[...])
    x2_ref[...] = x2
    xb_ref[...] = x2.astype(BF16)
    tm, D = x2.shape
    if not dils:
        return
    for g in range(D // LANES):
        scr_ref[g] = x2[:, g * LANES:(g + 1) * LANES]
    for d, f_ref in zip(dils, fold_refs):
        for g in range(D // LANES):
            for rr in range(d):
                f_ref[rr, :, g * LANES:(g + 1) * LANES] = scr_ref[g, pl.ds(rr, tm // d, stride=d), :].astype(BF16)


def _ln2(x1, y_lo, y_hi, g, b, B, S, tm, alpha, dils):
    N, D = x1.shape
    row = lambda i: (i, 0)
    full = lambda i: (0, 0)
    spb = S // tm
    outs = pl.pallas_call(
        functools.partial(_ln2_kernel, alpha=alpha, dils=dils),
        out_shape=(jax.ShapeDtypeStruct((N, D), F32), jax.ShapeDtypeStruct((N, D), BF16),
                   *[jax.ShapeDtypeStruct((B * d, S // d, D), BF16) for d in dils]),
        grid=(N // tm,),
        in_specs=[pl.BlockSpec((tm, D), row), pl.BlockSpec((tm, D // 2), row), pl.BlockSpec((tm, D // 2), row),
                  pl.BlockSpec(g.shape, full), pl.BlockSpec(b.shape, full)],
        out_specs=(pl.BlockSpec((tm, D), row), pl.BlockSpec((tm, D), row),
                   *[pl.BlockSpec((d, tm // d, D), lambda i: (i // spb, i % spb, 0)) for d in dils]),
        scratch_shapes=[pltpu.VMEM((D // LANES, tm, LANES), F32)],
        compiler_params=_cparams(("parallel",)),
        name="residual_ln2",
    )(x1, y_lo, y_hi, g, b)
    x2, xb = outs[0], outs[1]
    return x2, {1: xb, **{d: f.reshape(N, D) for d, f in zip(dils, outs[2:])}}


def _tiles(B, S):
    N = B * S
    tm = 1024 if S % 1024 == 0 else S
    return dict(tm_proj=tm, tq_b=min(512, S), tq_c=512, tm_merge=512,
                tc=min(512, CAPACITY_FACTOR * N // N_EXPERTS))


def _prep_layer(l, w_in, pool_mix, pool_scale, sink_logit, w_branch, w_out, ln1_g, ln1_b, w_router,
                w_gate_e, w_up_e, w_down_e, ln2_g, ln2_b):
    D = w_in.shape[1]
    a0, qb0 = 0, A_WIDTH
    kb0 = qb0 + B_Q_HEADS * B_HEAD_DIM
    vb0 = kb0 + B_KV_HEADS * B_HEAD_DIM
    qc0 = vb0 + B_KV_HEADS * B_HEAD_DIM
    kc0 = qc0 + len(C_CONFIGS) * C_GROUP_W
    vc0 = kc0 + len(C_CONFIGS) * C_GROUP_W
    zg0 = vc0 + len(C_CONFIGS) * C_GROUP_W
    spread = np.concatenate([h * C_HEAD_DIM + _spread_head_perm(C_HEAD_DIM) for h in range(C_HEADS)])
    wl = w_in[l]
    wr = jnp.zeros((D, LANES), F32).at[:, :N_EXPERTS].set(w_router[l])
    main = [wl[:, qb0:vb0], jnp.concatenate([wl[:, a0:qb0], wl[:, vb0:qc0]], axis=1), 0.5 * wl[:, zg0:]]
    return dict(
        w_main=jnp.concatenate(main, axis=1).astype(BF16), main_widths=tuple(m.shape[1] for m in main),
        w_c=[jnp.concatenate([wl[:, c0 + g * C_GROUP_W:c0 + (g + 1) * C_GROUP_W][:, cols]
                              for c0, cols in ((qc0, spread), (kc0, spread), (vc0, np.arange(C_GROUP_W)))],
                             axis=1).astype(BF16) for g in range(len(C_CONFIGS))],
        pmix=pool_mix[l].astype(BF16), pscale=pool_scale[l][None, :], sink=sink_logit[l],
        wbr=(0.5 * w_branch[l]).astype(BF16), wout=w_out[l].astype(BF16),
        g1=ln1_g[l][None, :], b1=ln1_b[l][None, :], wr=wr.astype(BF16),
        wg=w_gate_e[l].astype(BF16), wu=w_up_e[l].astype(BF16), wd=w_down_e[l].astype(BF16),
        g2=ln2_g[l][None, :], b2=ln2_b[l][None, :],
    )


def _fold_rows(a, B, S, d):
    return a if d == 1 else a.reshape(B, S // d, d, -1).transpose(0, 2, 1, 3).reshape(B * S, -1)


def _layer(x, xbs, p, B, S, alpha, rot_b, rot_c, feeds_next):
    N, D = x.shape
    t = _tiles(B, S)
    cap = CAPACITY_FACTOR * N // N_EXPERTS
    qkb, avb, zg = _proj_main(xbs[1], p["w_main"], p["main_widths"], S, t["tm_proj"], rot_b)
    o_b = _attn_b(qkb, avb, p["sink"], B, S, t["tq_b"])
    o_cs, lses = [], []
    for g, (_, d) in enumerate(C_CONFIGS):
        qkv = _proj_qkv(xbs[d], p["w_c"][g], S, t["tm_proj"], 2 * C_GROUP_W, rot_c[d], f"proj_qkv_{d}")
        qkv = qkv.reshape(B * d, S // d, qkv.shape[1])
        o, lse = _attn_c(qkv, g, B, S, t["tq_c"])
        o_cs.append(o)
        lses.append(lse)
    x1, aff_t = _merge(x, avb, o_b, o_cs, lses, zg, p["pmix"], p["pscale"], p["wbr"], p["wout"], p["g1"], p["b1"],
                       p["wr"], S, t["tm_merge"], alpha)
    idx_flat = _select(aff_t, cap).reshape(-1)
    xe = _gather(x1, idx_flat, SC_GATHER_ROWS)
    ye_lo, ye_hi = _ffn(xe, p["wg"], p["wu"], p["wd"], p["wr"], cap, t["tc"])
    y_lo, y_hi = _scatter_add(N, ye_lo, ye_hi, idx_flat, cap, SC_GATHER_ROWS)
    return _ln2(x1, y_lo, y_hi, p["g2"], p["b2"], B, S, t["tm_proj"], alpha, FOLD_DILATIONS if feeds_next else ())


def _trunk(x, layers, alpha):
    B, S, D = x.shape
    rot_b = _rot_tables(S, B_HEAD_DIM)
    half_c, tabs_c = _rot_tables_spread(S, C_HEAD_DIM)
    rot_c = {d: (half_c, tuple(_fold_rows(tab, 1, S, d) for tab in tabs_c)) for _, d in C_CONFIGS}
    xf = x.reshape(B * S, D)
    xb = xf.astype(BF16)
    xbs = {d: _fold_rows(xb, B, S, d) for _, d in C_CONFIGS}
    for l, p in enumerate(layers):
        xf, xbs = _layer(xf, xbs, p, B, S, alpha, rot_b, rot_c, feeds_next=l + 1 < len(layers))
    return xf.reshape(B, S, D)


def kernel(x_prompt, x_sample, w_in, pool_mix, pool_scale, sink_logit, w_branch, w_out, ln1_g, ln1_b, w_router,
           w_gate_e, w_up_e, w_down_e, ln2_g, ln2_b):
    depth = w_in.shape[0]
    alpha = (2 * depth) ** 0.25
    layers = [_prep_layer(l, w_in, pool_mix, pool_scale, sink_logit, w_branch, w_out, ln1_g, ln1_b, w_router,
                          w_gate_e, w_up_e, w_down_e, ln2_g, ln2_b) for l in range(depth)]
    y_sample = _trunk(x_sample, layers, alpha)
    y_prompt = _trunk(x_prompt, layers, alpha)
    return (y_prompt, y_sample)
```

```python
import functools

import numpy as np
import jax
import jax.numpy as jnp
from jax import lax
from jax.experimental import pallas as pl
from jax.experimental.pallas import tpu as pltpu
from jax.experimental.pallas import tpu_sc as plsc

F32 = jnp.float32
BF16 = jnp.bfloat16
I32 = jnp.int32

POOL_WINDOWS = (2, 4, 8, 16)
POOL_CH = 128
A_WIDTH = 512
B_Q_HEADS = 8
B_KV_HEADS = 2
B_HEAD_DIM = 64
B_HALF_WINDOW = 128
C_CONFIGS = ((128, 1), (512, 4), (2048, 16))
C_HEADS = 4
C_HEAD_DIM = 128
C_GROUP_W = C_HEADS * C_HEAD_DIM
FOLD_DILATIONS = tuple(d for _, d in C_CONFIGS if d > 1)
N_EXPERTS = 16
CAPACITY_FACTOR = 2
ROPE_THETA = 500000.0
ROT_DIV = 4
LN_EPS = 1e-5
NEG_INF = -1e30
LOG2E = 1.4426950408889634
LN2 = 0.6931471805599453

LANES = 128
SUBLANES = 8
VMEM_LIMIT = 56 * 1024 * 1024
SC_CORES = 2
SC_SUBCORES = 16
SC_LANES = 16
SC_GATHER_ROWS = 64
MXU_WIDTH = 256
MERGE_SLAB = 256
ATTN_STRIP = 32
ATTN_UNROLL = 8
PERM_BLOCK = MXU_WIDTH


def _cparams(sem):
    return pltpu.CompilerParams(dimension_semantics=sem, vmem_limit_bytes=VMEM_LIMIT)


def _col_chunks(width):
    return [(c, min(c + MXU_WIDTH, width)) for c in range(0, width, MXU_WIDTH)]


def _rotary(acc, c, s1, s2, rot_half):
    outs = []
    for g in range(acc.shape[1] // LANES):
        a = acc[:, g * LANES:(g + 1) * LANES]
        if 2 * rot_half == LANES:
            outs.append(a * c + pltpu.roll(a, rot_half, 1) * s1)
        else:
            outs.append(a * c + pltpu.roll(a, LANES - rot_half, 1) * s1 + pltpu.roll(a, rot_half, 1) * s2)
    return outs[0] if len(outs) == 1 else jnp.concatenate(outs, axis=1)


def _proj_main_kernel(x_ref, w_ref, cos_ref, s1_ref, s2_ref, qk_ref, av_ref, zg_ref, *, rot_half):
    x = x_ref[...]
    base = 0
    for o_ref, rotate in ((qk_ref, True), (av_ref, False), (zg_ref, False)):
        for c0, c1 in _col_chunks(o_ref.shape[1]):
            acc = jnp.dot(x, w_ref[:, base + c0:base + c1], preferred_element_type=F32)
            if rotate:
                acc = _rotary(acc, cos_ref[...], s1_ref[...], s2_ref[...], rot_half)
            o_ref[:, c0:c1] = acc.astype(o_ref.dtype)
        base += o_ref.shape[1]


def _proj_main(x, w, widths, S, tm, rot):
    N, K = x.shape
    rot_half, tabs = rot
    spb = S // tm
    row = lambda i: (i, 0)
    tab_spec = pl.BlockSpec((tm, LANES), lambda i: (i % spb, 0))
    dtypes = (BF16, F32, BF16)
    return pl.pallas_call(
        functools.partial(_proj_main_kernel, rot_half=rot_half),
        out_shape=tuple(jax.ShapeDtypeStruct((N, c), dt) for c, dt in zip(widths, dtypes)),
        grid=(N // tm,),
        in_specs=[pl.BlockSpec((tm, K), row), pl.BlockSpec(w.shape, lambda i: (0, 0)), tab_spec, tab_spec, tab_spec],
        out_specs=tuple(pl.BlockSpec((tm, c), row) for c in widths),
        compiler_params=_cparams(("parallel",)),
        name="proj_main",
    )(x, w, *tabs)


def _fold_perm(d, inverse=False):
    i = lax.broadcasted_iota(I32, (PERM_BLOCK, PERM_BLOCK), 1 if inverse else 0)
    j = lax.broadcasted_iota(I32, (PERM_BLOCK, PERM_BLOCK), 0 if inverse else 1)
    per = PERM_BLOCK // d
    return jnp.where(j == (i % per) * d + i // per, 1.0, 0.0).astype(BF16)


def _proj_qkv_kernel(x_ref, w_ref, cos_ref, s1_ref, s2_ref, o_ref, *, rot_half, rot_cols):
    x = x_ref[...]
    for c0, c1 in _col_chunks(w_ref.shape[1]):
        acc = jnp.dot(x, w_ref[:, c0:c1], preferred_element_type=F32)
        if c0 < rot_cols:
            acc = _rotary(acc, cos_ref[...], s1_ref[...], s2_ref[...], rot_half)
        o_ref[:, c0:c1] = acc.astype(BF16)


def _proj_qkv(x, w, S, tm, rot_cols, rot, name):
    N, K = x.shape
    C = w.shape[1]
    rot_half, tabs = rot
    spb = S // tm
    tab_spec = pl.BlockSpec((tm, LANES), lambda i: (i % spb, 0))
    return pl.pallas_call(
        functools.partial(_proj_qkv_kernel, rot_half=rot_half, rot_cols=rot_cols),
        out_shape=jax.ShapeDtypeStruct((N, C), BF16),
        grid=(N // tm,),
        in_specs=[pl.BlockSpec((tm, K), lambda i: (i, 0)), pl.BlockSpec((K, C), lambda i: (0, 0)),
                  tab_spec, tab_spec, tab_spec],
        out_specs=pl.BlockSpec((tm, C), lambda i: (i, 0)),
        compiler_params=_cparams(("parallel",)),
        name=name,
    )(x, w, *tabs)


def _rot_tables(S, head_dim):
    rot = head_dim // ROT_DIV
    half = rot // 2
    inv = 1.0 / (ROPE_THETA ** (jnp.arange(0, rot, 2, dtype=F32) / rot))
    ang = jnp.arange(S, dtype=F32)[:, None] * inv[None, :]
    cos, sin = jnp.cos(ang), jnp.sin(ang)
    d = np.arange(LANES) % head_dim
    first = d < half
    second = (d >= half) & (d < rot)
    src = np.where(first, d, np.where(second, d - half, 0))
    cos_t = jnp.where(first | second, cos[:, src], 1.0)
    s1_t = jnp.where(first, -sin[:, src], 0.0)
    s2_t = jnp.where(second, sin[:, src], 0.0)
    return half, (cos_t, s1_t, s2_t)


def _spread_head_perm(head_dim):
    half = head_dim // ROT_DIV // 2
    mid = LANES // 2
    return np.concatenate([np.arange(0, half), np.arange(2 * half, mid + half), np.arange(half, 2 * half),
                           np.arange(mid + half, head_dim)])


def _rot_tables_spread(S, head_dim):
    half, (cos_t, s1_t, s2_t) = _rot_tables(S, head_dim)
    perm = _spread_head_perm(head_dim)
    return LANES // 2, (cos_t[:, perm], (s1_t + s2_t)[:, perm], s2_t)


def _band_bias(bias_ref, W, q0, L, sub):
    tq, nk = bias_ref.shape
    row = lax.broadcasted_iota(I32, (tq, nk), 0)
    col = lax.broadcasted_iota(I32, (tq, nk), 1)
    first = (row // sub) * sub
    kpos = q0 - W + first + col
    valid = (jnp.abs(col - W - (row - first)) <= W) & (kpos >= 0) & (kpos < L)
    bias_ref[...] = jnp.where(valid, 0.0, NEG_INF)


def _softmax_strips(s_ref, p_ref, den_ref, max_ref, bias_ref, *, unroll, sink_fn=None):
    rows, nk = s_ref.shape
    tq = bias_ref.shape[0]

    def strip(t):
        r0 = pl.multiple_of(t * ATTN_STRIP, ATTN_STRIP)
        rows_sl = pl.ds(r0, ATTN_STRIP)
        s = s_ref[rows_sl, :] + bias_ref[pl.ds(pl.multiple_of(r0 % tq, ATTN_STRIP), ATTN_STRIP), :]
        return r0, rows_sl, s

    def row_max(t, carry):
        r0, rows_sl, s = strip(t)
        m = jnp.max(s, axis=1, keepdims=True)
        if sink_fn is not None:
            m = jnp.maximum(m, sink_fn(r0 // tq) * LOG2E)
        max_ref[rows_sl, :] = jnp.broadcast_to(m, (ATTN_STRIP, LANES))
        return carry

    def numerators(t, carry):
        r0, rows_sl, s = strip(t)
        m = max_ref[rows_sl, :]
        p = jnp.exp2(s - jnp.concatenate([m] * (nk // LANES), axis=1))
        den = jnp.broadcast_to(jnp.sum(p, axis=1, keepdims=True), (ATTN_STRIP, LANES))
        if sink_fn is not None:
            den = den + jnp.exp2(sink_fn(r0 // tq) * LOG2E - m)
        p_ref[rows_sl, :] = p.astype(BF16)
        den_ref[rows_sl, :] = den
        return carry

    lax.fori_loop(0, rows // ATTN_STRIP, row_max, 0, unroll=unroll)
    lax.fori_loop(0, rows // ATTN_STRIP, numerators, 0, unroll=unroll)


def _attn_b_kernel(sink_ref, q_ref, kp_ref, km_ref, kn_ref, vp_ref, vm_ref, vn_ref, o_ref,
                   s_ref, p_ref, den_ref, max_ref, bias_ref, *, S, TQ):
    W = B_HALF_WINDOW
    G = B_Q_HEADS // B_KV_HEADS
    sub, span = W, 3 * W
    nb = TQ // sub
    q0 = pl.program_id(1) * TQ
    _band_bias(bias_ref, W, q0, S, sub)
    k_all = jnp.concatenate([kp_ref[...], km_ref[...], kn_ref[...]], axis=0)
    v_all = jnp.concatenate([vp_ref[...], vm_ref[...], vn_ref[...]], axis=0).astype(BF16)
    q = q_ref[...]
    outs = []
    for j in range(B_KV_HEADS):
        kj = k_all[:, j * B_HEAD_DIM:(j + 1) * B_HEAD_DIM]
        vj = v_all[:, j * B_HEAD_DIM:(j + 1) * B_HEAD_DIM]
        for b in range(nb):
            qs = jnp.concatenate([q[b * sub:(b + 1) * sub, (G * j + g) * B_HEAD_DIM:(G * j + g + 1) * B_HEAD_DIM]
                                  for g in range(G)], axis=0)
            s = lax.dot_general(qs, kj[b * sub:b * sub + span], (((1,), (1,)), ((), ())),
                                preferred_element_type=F32) * (B_HEAD_DIM ** -0.5 * LOG2E)
            for g in range(G):
                s_ref[j, g * TQ + b * sub:g * TQ + (b + 1) * sub, :] = s[g * sub:(g + 1) * sub]
        _softmax_strips(s_ref.at[j], p_ref.at[j], den_ref.at[j], max_ref.at[j], bias_ref, unroll=True,
                        sink_fn=lambda g, j=j: sink_ref[G * j + g])
        pv = []
        for b in range(nb):
            pb = jnp.concatenate([p_ref[j, g * TQ + b * sub:g * TQ + (b + 1) * sub, :] for g in range(G)], axis=0)
            pv.append(jnp.dot(pb, vj[b * sub:b * sub + span], preferred_element_type=F32))
        for g in range(G):
            o = jnp.concatenate([pv[b][g * sub:(g + 1) * sub] for b in range(nb)], axis=0)
            outs.append(o / den_ref[j, g * TQ:(g + 1) * TQ, 0:1])
    o_ref[...] = jnp.concatenate(outs, axis=1).astype(o_ref.dtype)


def _attn_b(qk, av, sink, B, S, TQ):
    N = B * S
    W = B_HALF_WINDOW
    nq = S // TQ
    r = TQ // W
    kcol = (B_Q_HEADS * B_HEAD_DIM) // LANES
    last = N // W - 1
    rows = (B_Q_HEADS // B_KV_HEADS) * TQ

    def prev(b, i):
        return (jnp.maximum((b * nq + i) * r - 1, 0), kcol)

    def main(b, i):
        return (b * nq + i, kcol)

    def nxt(b, i):
        return (jnp.minimum((b * nq + i + 1) * r, last), kcol)

    return pl.pallas_call(
        functools.partial(_attn_b_kernel, S=S, TQ=TQ),
        out_shape=jax.ShapeDtypeStruct((N, B_Q_HEADS * B_HEAD_DIM), BF16),
        grid=(B, nq),
        in_specs=[pl.BlockSpec(memory_space=pltpu.SMEM),
                  pl.BlockSpec((TQ, B_Q_HEADS * B_HEAD_DIM), lambda b, i: (b * nq + i, 0)),
                  pl.BlockSpec((W, LANES), prev), pl.BlockSpec((TQ, LANES), main), pl.BlockSpec((W, LANES), nxt),
                  pl.BlockSpec((W, LANES), prev), pl.BlockSpec((TQ, LANES), main), pl.BlockSpec((W, LANES), nxt)],
        out_specs=pl.BlockSpec((TQ, B_Q_HEADS * B_HEAD_DIM), lambda b, i: (b * nq + i, 0)),
        scratch_shapes=[pltpu.VMEM((B_KV_HEADS, rows, 3 * W), F32),
                        pltpu.VMEM((B_KV_HEADS, rows, 3 * W), BF16),
                        pltpu.VMEM((B_KV_HEADS, rows, LANES), F32), pltpu.VMEM((B_KV_HEADS, rows, LANES), F32),
                        pltpu.VMEM((TQ, 3 * W), F32)],
        compiler_params=_cparams(("parallel", "arbitrary")),
        name="attn_window",
    )(sink, qk, qk, qk, qk, av, av, av)


def _attn_c_kernel(q_ref, kp_ref, km_ref, kn_ref, vp_ref, vm_ref, vn_ref, o_ref, lse_ref,
                   s_ref, p_ref, den_ref, max_ref, bias_ref, *, M, TQ, W):
    sub, span = 2 * W, 4 * W
    q0 = pl.program_id(1) * TQ
    _band_bias(bias_ref, W, q0, M, sub)
    k_all = jnp.concatenate([kp_ref[...], km_ref[...], kn_ref[...]], axis=0)
    v_all = jnp.concatenate([vp_ref[...], vm_ref[...], vn_ref[...]], axis=0)
    q = q_ref[...]
    lane = lax.broadcasted_iota(I32, (TQ, LANES), 1)
    lse_tile = jnp.zeros((TQ, LANES), F32)
    outs = []
    for h in range(C_HEADS):
        sl = slice(h * C_HEAD_DIM, (h + 1) * C_HEAD_DIM)
        for b in range(TQ // sub):
            s_ref[h, b * sub:(b + 1) * sub, :] = lax.dot_general(
                q[b * sub:(b + 1) * sub, sl], k_all[b * sub:b * sub + span, sl], (((1,), (1,)), ((), ())),
                preferred_element_type=F32) * (C_HEAD_DIM ** -0.5 * LOG2E)
        _softmax_strips(s_ref.at[h], p_ref.at[h], den_ref.at[h], max_ref.at[h], bias_ref, unroll=True)
        den = den_ref[h]
        pv = [jnp.dot(p_ref[h, b * sub:(b + 1) * sub, :], v_all[b * sub:b * sub + span, sl],
                      preferred_element_type=F32) for b in range(TQ // sub)]
        outs.append(jnp.concatenate(pv, axis=0) / den[:, 0:1])
        lse_tile = jnp.where(lane == h, max_ref[h] * LN2 + jnp.log(den), lse_tile)
    o_ref[...] = jnp.concatenate(outs, axis=1).astype(o_ref.dtype)
    lse_ref[...] = lse_tile


def _attn_c(qkv, g, B, S, TQ):
    window, d = C_CONFIGS[g]
    W = window // (2 * d)
    M = S // d
    TQ = min(TQ, M)
    r = TQ // W
    GW = C_GROUP_W
    last = M // W - 1

    def mk(col):
        return [pl.BlockSpec((None, W, GW), lambda b, i: (b, jnp.maximum(i * r - 1, 0), col)),
                pl.BlockSpec((None, TQ, GW), lambda b, i: (b, i, col)),
                pl.BlockSpec((None, W, GW), lambda b, i: (b, jnp.minimum((i + 1) * r, last), col))]

    return pl.pallas_call(
        functools.partial(_attn_c_kernel, M=M, TQ=TQ, W=W),
        out_shape=(jax.ShapeDtypeStruct((B * d, M, GW), BF16), jax.ShapeDtypeStruct((B * d, M, LANES), F32)),
        grid=(B * d, M // TQ),
        in_specs=[pl.BlockSpec((None, TQ, GW), lambda b, i: (b, i, 0))] + mk(1) + mk(2),
        out_specs=(pl.BlockSpec((None, TQ, GW), lambda b, i: (b, i, 0)),
                   pl.BlockSpec((None, TQ, LANES), lambda b, i: (b, i, 0))),
        scratch_shapes=[pltpu.VMEM((C_HEADS, TQ, 4 * W), F32), pltpu.VMEM((C_HEADS, TQ, 4 * W), BF16),
                        pltpu.VMEM((C_HEADS, TQ, LANES), F32), pltpu.VMEM((C_HEADS, TQ, LANES), F32),
                        pltpu.VMEM((TQ, 4 * W), F32)],
        compiler_params=_cparams(("parallel", "arbitrary")),
        name=f"attn_dilated_{d}",
    )(qkv, qkv, qkv, qkv, qkv, qkv, qkv)


def _sigmoid(z):
    return 0.5 * jnp.tanh(0.5 * z) + 0.5


def _layer_norm(h, g, b):
    mu = jnp.mean(h, axis=-1, keepdims=True)
    var = jnp.mean(jnp.square(h - mu), axis=-1, keepdims=True)
    return (h - mu) * lax.rsqrt(var + LN_EPS) * g + b


def _router_aff(xb, wr):
    logits = jnp.dot(xb, wr, preferred_element_type=F32)
    lane = lax.broadcasted_iota(I32, logits.shape, 1)
    logits = jnp.where(lane < N_EXPERTS, logits, NEG_INF)
    e = jnp.exp(logits - jnp.max(logits, axis=1, keepdims=True))
    return e / jnp.sum(e, axis=1, keepdims=True)


def _unfold(ref, scr_ref):
    d, rows, _ = ref.shape
    if d == 1:
        return ref[0]
    for rr in range(d):
        scr_ref[pl.ds(rr, rows, stride=d), :] = ref[rr]
    return scr_ref[...]


def _unfold_bf16(ref):
    d, rows, _ = ref.shape
    if d == 1:
        return ref[0].astype(F32)
    per = PERM_BLOCK // d
    perm = _fold_perm(d, inverse=True)
    outs = []
    for blk in range(d * rows // PERM_BLOCK):
        stacked = jnp.concatenate([ref[rr, blk * per:(blk + 1) * per, :] for rr in range(d)], axis=0)
        outs.append(jnp.dot(perm, stacked, preferred_element_type=F32))
    return outs[0] if len(outs) == 1 else jnp.concatenate(outs, axis=0)


def _merge_kernel(x_ref, a_ref, ap_ref, an_ref, ob_ref, oc0_ref, oc1_ref, oc2_ref, l0_ref, l1_ref, l2_ref,
                  zg_ref, pmix_ref, pscale_ref, wbr_ref, wout_ref, g_ref, b_ref, wr_ref,
                  x1_ref, aff_ref, sl1_ref, sl2_ref, *, S, tm, alpha):
    HW = max(POOL_WINDOWS) // 2
    L = tm + 2 * HW
    pos0 = (pl.program_id(0) * tm) % S
    xa = jnp.concatenate([ap_ref[...], a_ref[...], an_ref[...]], axis=0)
    xpos = pos0 - HW + lax.broadcasted_iota(I32, (L, 1), 0)
    xa = jnp.where((xpos >= 0) & (xpos < S), xa, 0.0)
    sums = {}
    t, w = xa, 1
    while w < max(POOL_WINDOWS):
        t = t + pltpu.roll(t, L - w, 0)
        w *= 2
        sums[w] = t
    pos = pos0 + lax.broadcasted_iota(I32, (tm, 1), 0)
    mixed = []
    for g, w in enumerate(POOL_WINDOWS):
        cs = slice(g * POOL_CH, (g + 1) * POOL_CH)
        off = HW - w // 2
        sw = sums[w][:, cs]
        if off:
            sw = pltpu.roll(sw, L - off, 0)
        sw = sw[:tm]
        cnt = (jnp.clip(pos + w // 2, 0, S) - jnp.clip(pos - w // 2, 0, S)).astype(F32)
        pooled = sw / cnt - a_ref[:, cs]
        mixed.append(jnp.dot(pooled.astype(BF16), pmix_ref[g], preferred_element_type=F32) * pscale_ref[:, cs])
    o_a = jnp.concatenate(mixed, axis=1)
    ls = [_unfold(l0_ref, None), _unfold(l1_ref, sl1_ref), _unfold(l2_ref, sl2_ref)]
    mx = jnp.maximum(jnp.maximum(ls[0], ls[1]), ls[2])
    es = [jnp.exp(l - mx) for l in ls]
    tot = es[0] + es[1] + es[2]
    ocs = [_unfold_bf16(oc0_ref), _unfold_bf16(oc1_ref), _unfold_bf16(oc2_ref)]
    pieces = []
    for h in range(C_HEADS):
        sl = slice(h * C_HEAD_DIM, (h + 1) * C_HEAD_DIM)
        acc = None
        for gi in range(3):
            term = (es[gi] / tot)[:, h:h + 1] * ocs[gi][:, sl].astype(F32)
            acc = term if acc is None else acc + term
        pieces.append(acc)
    o_c = jnp.concatenate(pieces, axis=1)
    o_a, o_c = o_a.astype(BF16), o_c.astype(BF16)
    D = x_ref.shape[1]
    for r0 in range(0, tm, MERGE_SLAB):
        rs = slice(r0, r0 + MERGE_SLAB)
        branches = [o_a[rs], ob_ref[rs, :], o_c[rs]]
        merged = []
        for c0, c1 in _col_chunks(D):
            acc = None
            for k in range(3):
                gate2 = jnp.tanh(zg_ref[rs, k * D + c0:k * D + c1].astype(F32)) + 1.0
                term = gate2 * jnp.dot(branches[k], wbr_ref[k, :, c0:c1], preferred_element_type=F32)
                acc = term if acc is None else acc + term
            merged.append(acc.astype(BF16))
        merged = jnp.concatenate(merged, axis=1)
        pre = [alpha * x_ref[rs, c0:c1] + jnp.dot(merged, wout_ref[:, c0:c1], preferred_element_type=F32)
               for c0, c1 in _col_chunks(D)]
        x1 = _layer_norm(jnp.concatenate(pre, axis=1), g_ref[...], b_ref[...])
        x1_ref[rs, :] = x1
        aff = _router_aff(x1.astype(BF16), wr_ref[...])
        aff_ref[:, rs] = aff.T[:N_EXPERTS]


def _merge(x, av, o_b, o_cs, lses, vz, pmix, pscale, wbr, wout, g, b, wr, S, tm, alpha):
    N, D = x.shape
    HW = max(POOL_WINDOWS) // 2
    r = tm // HW
    last = N // HW - 1
    row = lambda i: (i, 0)
    full2 = lambda i: (0, 0)
    full3 = lambda i: (0, 0, 0)
    spb = S // tm
    dils = [d for _, d in C_CONFIGS]

    def folded(width):
        return [pl.BlockSpec((d, tm // d, width), lambda i: (i // spb, i % spb, 0)) for d in dils]

    in_specs = [
        pl.BlockSpec((tm, D), row),
        pl.BlockSpec((tm, A_WIDTH), row),
        pl.BlockSpec((HW, A_WIDTH), lambda i: (jnp.maximum(i * r - 1, 0), 0)),
        pl.BlockSpec((HW, A_WIDTH), lambda i: (jnp.minimum((i + 1) * r, last), 0)),
        pl.BlockSpec((tm, B_Q_HEADS * B_HEAD_DIM), row),
        *folded(C_GROUP_W),
        *folded(LANES),
        pl.BlockSpec((tm, 3 * D), row),
        pl.BlockSpec(pmix.shape, full3),
        pl.BlockSpec(pscale.shape, full2),
        pl.BlockSpec(wbr.shape, full3),
        pl.BlockSpec(wout.shape, full2),
        pl.BlockSpec(g.shape, full2), pl.BlockSpec(b.shape, full2),
        pl.BlockSpec(wr.shape, full2),
    ]
    return pl.pallas_call(
        functools.partial(_merge_kernel, S=S, tm=tm, alpha=alpha),
        out_shape=(jax.ShapeDtypeStruct((N, D), F32), jax.ShapeDtypeStruct((N_EXPERTS, N), F32)),
        grid=(N // tm,),
        in_specs=in_specs,
        out_specs=(pl.BlockSpec((tm, D), row), pl.BlockSpec((N_EXPERTS, tm), lambda i: (0, i))),
        scratch_shapes=[pltpu.VMEM((tm, LANES), F32)] * 2,
        compiler_params=_cparams(("parallel",)),
        name="merge_ln1_router",
    )(x, av, av, av, o_b, *o_cs, *lses, vz, pmix, pscale, wbr, wout, g, b, wr)


def _count(mask):
    return jnp.sum(jnp.sum(jnp.where(mask, 1.0, 0.0), axis=1), axis=1, keepdims=True)


def _thr_kernel(aff_ref, t_ref, need_ref, *, cap):
    bits = pltpu.bitcast(aff_ref[...], I32)

    def body(k, t):
        cand = t | jnp.left_shift(jnp.int32(1), 30 - k)
        return jnp.where(_count(bits >= cand[:, :, None]) >= cap, cand, t)

    t = lax.fori_loop(0, 31, body, jnp.zeros((N_EXPERTS, 1), I32))
    need = cap - _count(bits > t[:, :, None])
    t_ref[...] = jnp.broadcast_to(t, t_ref.shape)
    need_ref[...] = jnp.broadcast_to(need.astype(I32), need_ref.shape)


def _pos_kernel(t_ref, need_ref, aff_ref, clt_ref, ext_ref, lo_ref, hi_ref, *, NC):
    e = pl.program_id(0)
    bits = pltpu.bitcast(aff_ref[...], I32)
    t = t_ref[e]
    need = need_ref[e].astype(F32)
    upper = jnp.where(lax.broadcasted_iota(I32, (LANES, LANES), 0) <= lax.broadcasted_iota(I32, (LANES, LANES), 1),
                      1.0, 0.0).astype(BF16)
    ones = jnp.ones((LANES, LANES), BF16)
    before = jnp.where(lax.broadcasted_iota(I32, (NC, NC), 1) < lax.broadcasted_iota(I32, (NC, NC), 0),
                       1.0, 0.0).astype(BF16)

    def cums(xf):
        xb = xf.astype(BF16)
        local = jnp.dot(xb, upper, preferred_element_type=F32)
        tot = jnp.dot(xb, ones, preferred_element_type=F32)
        offs = jnp.dot(before, tot.astype(BF16), preferred_element_type=F32)
        return local, tot, offs

    eq = jnp.where(bits == t, 1.0, 0.0)
    local, _, offs = cums(eq)
    sel = (bits > t) | ((bits == t) & (local + offs - eq < need))
    local, tot, offs = cums(jnp.where(sel, 1.0, 0.0))
    lo_ref[...] = offs
    hi_ref[...] = offs + tot
    clt_ref[...] = local.T
    chunk = lax.broadcasted_iota(I32, (NC, LANES), 0)
    lane = lax.broadcasted_iota(I32, (NC, LANES), 1)
    offs_i = offs.astype(I32)
    ext = jnp.where(lane == 0, offs_i >> 8,
                    jnp.where(lane == 1, offs_i & 255,
                              jnp.where(lane == 2, chunk >> 8, jnp.where(lane == 3, chunk & 255, 0))))
    ext_ref[...] = ext.astype(F32).T[:SUBLANES]


def _idx_kernel(clt_ref, ext_ref, lo_ref, hi_ref, idx_ref, *, SB):
    s = (pl.program_id(1) * SB + lax.broadcasted_iota(I32, (1, SB), 1)).astype(F32)
    onehot = jnp.where((lo_ref[:, 0:1] <= s) & (s < hi_ref[:, 0:1]), 1.0, 0.0).astype(BF16)
    cg = jnp.dot(clt_ref[...].astype(BF16), onehot, preferred_element_type=F32)
    ex = jnp.dot(ext_ref[...].astype(BF16), onehot, preferred_element_type=F32)
    slot_lo = ex[0:1] * 256.0 + ex[1:2]
    chunk = ex[2:3] * 256.0 + ex[3:4]
    within = jnp.sum(jnp.where(cg <= s - slot_lo, 1.0, 0.0), axis=0, keepdims=True)
    idx_ref[...] = (chunk * LANES + within).astype(I32)


def _select(aff_t, cap):
    E, N = aff_t.shape
    NC = N // LANES
    aff3 = aff_t.reshape(E, NC, LANES)
    t, need = pl.pallas_call(
        functools.partial(_thr_kernel, cap=cap),
        out_shape=(jax.ShapeDtypeStruct((E, LANES), I32), jax.ShapeDtypeStruct((E, LANES), I32)),
        compiler_params=pltpu.CompilerParams(vmem_limit_bytes=VMEM_LIMIT),
        name="select_threshold",
    )(aff3)
    per_e = lambda e: (e, 0, 0)
    smem = pl.BlockSpec(memory_space=pltpu.SMEM)
    clt, ext, lo, hi = pl.pallas_call(
        functools.partial(_pos_kernel, NC=NC),
        out_shape=(jax.ShapeDtypeStruct((E, LANES, NC), F32), jax.ShapeDtypeStruct((E, SUBLANES, NC), F32),
                   jax.ShapeDtypeStruct((E, NC, LANES), F32), jax.ShapeDtypeStruct((E, NC, LANES), F32)),
        grid=(E,),
        in_specs=[smem, smem, pl.BlockSpec((None, NC, LANES), per_e)],
        out_specs=(pl.BlockSpec((None, LANES, NC), per_e), pl.BlockSpec((None, SUBLANES, NC), per_e),
                   pl.BlockSpec((None, NC, LANES), per_e), pl.BlockSpec((None, NC, LANES), per_e)),
        compiler_params=_cparams(("parallel",)),
        name="select_prefix",
    )(t[:, 0], need[:, 0], aff3)
    SB = min(cap, 1024)
    per_e2 = lambda e, s: (e, 0, 0)
    idx = pl.pallas_call(
        functools.partial(_idx_kernel, SB=SB),
        out_shape=jax.ShapeDtypeStruct((E, 1, cap), I32),
        grid=(E, cap // SB),
        in_specs=[pl.BlockSpec((None, LANES, NC), per_e2), pl.BlockSpec((None, SUBLANES, NC), per_e2),
                  pl.BlockSpec((None, NC, LANES), per_e2), pl.BlockSpec((None, NC, LANES), per_e2)],
        out_specs=pl.BlockSpec((None, 1, SB), lambda e, s: (e, 0, s)),
        compiler_params=_cparams(("parallel", "arbitrary")),
        name="select_index",
    )(clt, ext, lo, hi)
    return idx.reshape(E, cap)


def _gather(x, idx_flat, K):
    N, D = x.shape
    R = idx_flat.shape[0]
    workers = SC_CORES * SC_SUBCORES
    per_w = R // workers
    mesh = plsc.VectorSubcoreMesh(core_axis_name="c", subcore_axis_name="s")

    @functools.partial(
        pl.kernel, mesh=mesh,
        out_type=jax.ShapeDtypeStruct((R, D), x.dtype),
        scratch_types=[pltpu.VMEM((K,), I32), pltpu.VMEM((K, D), x.dtype), pltpu.SemaphoreType.DMA],
        name="gather_rows_sc",
    )
    def run(x_hbm, idx_hbm, o_hbm, idx_v, rows_v, sem):
        base = (lax.axis_index("s") * SC_CORES + lax.axis_index("c")) * per_w

        @pl.loop(0, per_w // K)
        def _(c):
            off = base + c * K
            pltpu.sync_copy(idx_hbm.at[pl.ds(off, K)], idx_v)
            pltpu.async_copy(x_hbm.at[idx_v], rows_v, sem).wait()
            pltpu.sync_copy(rows_v, o_hbm.at[pl.ds(off, K)])

    return run(x, idx_flat)


def _ffn_kernel(x_ref, wg_ref, wu_ref, wd_ref, wr_ref, lo_ref, hi_ref, *, e0):
    e = pl.program_id(0) + e0
    xb = x_ref[...].astype(BF16)
    aff = _router_aff(xb, wr_ref[...])
    lane = lax.broadcasted_iota(I32, aff.shape, 1)
    val = jnp.sum(jnp.where(lane == e, aff, 0.0), axis=1, keepdims=True)
    gate = jnp.dot(xb, wg_ref[...], preferred_element_type=F32)
    up = jnp.dot(xb, wu_ref[...], preferred_element_type=F32)
    hid = gate * _sigmoid(gate) * up
    y = jnp.dot(hid.astype(BF16), wd_ref[...], preferred_element_type=F32) * val
    half = y.shape[1] // 2
    lo_ref[...] = y[:, :half]
    hi_ref[...] = y[:, half:]


def _ffn(xe, wg, wu, wd, wr, cap, tc, e0, ne):
    D = xe.shape[1]
    F = wg.shape[2]
    nb = cap // tc
    half = jax.ShapeDtypeStruct((ne * cap, D // 2), F32)
    return pl.pallas_call(
        functools.partial(_ffn_kernel, e0=e0),
        out_shape=(half, half),
        grid=(ne, nb),
        in_specs=[pl.BlockSpec((tc, D), lambda e, c: ((e0 + e) * nb + c, 0)),
                  pl.BlockSpec((None, D, F), lambda e, c: (e0 + e, 0, 0)),
                  pl.BlockSpec((None, D, F), lambda e, c: (e0 + e, 0, 0)),
                  pl.BlockSpec((None, F, D), lambda e, c: (e0 + e, 0, 0)),
                  pl.BlockSpec(wr.shape, lambda e, c: (0, 0))],
        out_specs=(pl.BlockSpec((tc, D // 2), lambda e, c: (e * nb + c, 0)),
                   pl.BlockSpec((tc, D // 2), lambda e, c: (e * nb + c, 0))),
        compiler_params=_cparams(("parallel", "arbitrary")),
        name="expert_swiglu",
    )(xe, wg, wu, wd, wr)


def _scatter_add(n_tokens, parts, cap, K):
    R, H = parts[0][0].shape
    n_exp = R // cap
    per_s = cap // SC_SUBCORES
    mesh = plsc.VectorSubcoreMesh(core_axis_name="c", subcore_axis_name="s")
    lanes = SC_LANES

    @functools.partial(
        pl.kernel, mesh=mesh, out_type=(),
        scratch_types=[pltpu.VMEM((K,), I32), pltpu.VMEM((K, H), F32), pltpu.VMEM((K, H), F32),
                       pltpu.SemaphoreType.DMA],
        name="scatter_add_sc",
    )
    def run(ylo_hbm, yhi_hbm, yelo_hbm, yehi_hbm, idx_hbm, idx_v, acc_v, add_v, sem):
        s = lax.axis_index("s")

        def half(y_hbm, ye_hbm):
            @pl.loop(0, n_exp)
            def _(e):
                @pl.loop(0, per_s // K)
                def _(ch):
                    off = e * cap + s * per_s + ch * K
                    pltpu.sync_copy(idx_hbm.at[pl.ds(off, K)], idx_v)
                    pltpu.sync_copy(ye_hbm.at[pl.ds(off, K)], add_v)
                    pltpu.async_copy(y_hbm.at[idx_v], acc_v, sem).wait()

                    @pl.loop(0, K)
                    def _(r):
                        for j in range(H // lanes):
                            sl = pl.ds(j * lanes, lanes)
                            acc_v[r, sl] = acc_v[r, sl] + add_v[r, sl]

                    pltpu.async_copy(acc_v, y_hbm.at[idx_v], sem).wait()

                plsc.subcore_barrier()

        @pl.when(lax.axis_index("c") == 0)
        def _():
            half(ylo_hbm, yelo_hbm)

        @pl.when(lax.axis_index("c") == 1)
        def _():
            half(yhi_hbm, yehi_hbm)

    y_lo = jax.new_ref(jnp.zeros((n_tokens, H), F32))
    y_hi = jax.new_ref(jnp.zeros((n_tokens, H), F32))
    for ye_lo, ye_hi, idx in parts:
        run(y_lo, y_hi, ye_lo, ye_hi, idx)
    return y_lo[...], y_hi[...]


def _ln2_kernel(x1_ref, ylo_ref, yhi_ref, g_ref, b_ref, x2_ref, xb_ref, *rest, alpha, dils):
    fold_refs, scr_ref = rest[:len(dils)], rest[len(dils)]
    y = jnp.concatenate([ylo_ref[...], yhi_ref[...]], axis=1)
    x2 = _layer_norm(alpha * x1_ref[...] + y, g_ref[...], b_ref[...])
    x2_ref[...] = x2
    xb_ref[...] = x2.astype(BF16)
    tm, D = x2.shape
    if not dils:
        return
    for g in range(D // LANES):
        scr_ref[g] = x2[:, g * LANES:(g + 1) * LANES]
    for d, f_ref in zip(dils, fold_refs):
        for g in range(D // LANES):
            for rr in range(d):
                f_ref[rr, :, g * LANES:(g + 1) * LANES] = scr_ref[g, pl.ds(rr, tm // d, stride=d), :].astype(BF16)


def _ln2(x1, y_lo, y_hi, g, b, B, S, tm, alpha, dils):
    N, D = x1.shape
    row = lambda i: (i, 0)
    full = lambda i: (0, 0)
    spb = S // tm
    outs = pl.pallas_call(
        functools.partial(_ln2_kernel, alpha=alpha, dils=dils),
        out_shape=(jax.ShapeDtypeStruct((N, D), F32), jax.ShapeDtypeStruct((N, D), BF16),
                   *[jax.ShapeDtypeStruct((B * d, S // d, D), BF16) for d in dils]),
        grid=(N // tm,),
        in_specs=[pl.BlockSpec((tm, D), row), pl.BlockSpec((tm, D // 2), row), pl.BlockSpec((tm, D // 2), row),
                  pl.BlockSpec(g.shape, full), pl.BlockSpec(b.shape, full)],
        out_specs=(pl.BlockSpec((tm, D), row), pl.BlockSpec((tm, D), row),
                   *[pl.BlockSpec((d, tm // d, D), lambda i: (i // spb, i % spb, 0)) for d in dils]),
        scratch_shapes=[pltpu.VMEM((D // LANES, tm, LANES), F32)],
        compiler_params=_cparams(("parallel",)),
        name="residual_ln2",
    )(x1, y_lo, y_hi, g, b)
    x2, xb = outs[0], outs[1]
    return x2, {1: xb, **{d: f.reshape(N, D) for d, f in zip(dils, outs[2:])}}


def _tiles(B, S):
    N = B * S
    tm = 1024 if S % 1024 == 0 else S
    return dict(tm_proj=tm, tq_b=min(512, S), tq_c=512, tm_merge=512,
                tc=min(512, CAPACITY_FACTOR * N // N_EXPERTS))


def _prep_layer(l, w_in, pool_mix, pool_scale, sink_logit, w_branch, w_out, ln1_g, ln1_b, w_router,
                w_gate_e, w_up_e, w_down_e, ln2_g, ln2_b):
    D = w_in.shape[1]
    a0, qb0 = 0, A_WIDTH
    kb0 = qb0 + B_Q_HEADS * B_HEAD_DIM
    vb0 = kb0 + B_KV_HEADS * B_HEAD_DIM
    qc0 = vb0 + B_KV_HEADS * B_HEAD_DIM
    kc0 = qc0 + len(C_CONFIGS) * C_GROUP_W
    vc0 = kc0 + len(C_CONFIGS) * C_GROUP_W
    zg0 = vc0 + len(C_CONFIGS) * C_GROUP_W
    spread = np.concatenate([h * C_HEAD_DIM + _spread_head_perm(C_HEAD_DIM) for h in range(C_HEADS)])
    wl = w_in[l]
    wr = jnp.zeros((D, LANES), F32).at[:, :N_EXPERTS].set(w_router[l])
    main = [wl[:, qb0:vb0], jnp.concatenate([wl[:, a0:qb0], wl[:, vb0:qc0]], axis=1), 0.5 * wl[:, zg0:]]
    return dict(
        w_main=jnp.concatenate(main, axis=1).astype(BF16), main_widths=tuple(m.shape[1] for m in main),
        w_c=[jnp.concatenate([wl[:, c0 + g * C_GROUP_W:c0 + (g + 1) * C_GROUP_W][:, cols]
                              for c0, cols in ((qc0, spread), (kc0, spread), (vc0, np.arange(C_GROUP_W)))],
                             axis=1).astype(BF16) for g in range(len(C_CONFIGS))],
        pmix=pool_mix[l].astype(BF16), pscale=pool_scale[l][None, :], sink=sink_logit[l],
        wbr=(0.5 * w_branch[l]).astype(BF16), wout=w_out[l].astype(BF16),
        g1=ln1_g[l][None, :], b1=ln1_b[l][None, :], wr=wr.astype(BF16),
        wg=w_gate_e[l].astype(BF16), wu=w_up_e[l].astype(BF16), wd=w_down_e[l].astype(BF16),
        g2=ln2_g[l][None, :], b2=ln2_b[l][None, :],
    )


def _fold_rows(a, B, S, d):
    return a if d == 1 else a.reshape(B, S // d, d, -1).transpose(0, 2, 1, 3).reshape(B * S, -1)


def _layer(x, xbs, p, B, S, alpha, rot_b, rot_c, feeds_next):
    N, D = x.shape
    t = _tiles(B, S)
    cap = CAPACITY_FACTOR * N // N_EXPERTS
    qkb, avb, zg = _proj_main(xbs[1], p["w_main"], p["main_widths"], S, t["tm_proj"], rot_b)
    o_b = _attn_b(qkb, avb, p["sink"], B, S, t["tq_b"])
    o_cs, lses = [], []
    for g, (_, d) in enumerate(C_CONFIGS):
        qkv = _proj_qkv(xbs[d], p["w_c"][g], S, t["tm_proj"], 2 * C_GROUP_W, rot_c[d], f"proj_qkv_{d}")
        qkv = qkv.reshape(B * d, S // d, qkv.shape[1])
        o, lse = _attn_c(qkv, g, B, S, t["tq_c"])
        o_cs.append(o)
        lses.append(lse)
    x1, aff_t = _merge(x, avb, o_b, o_cs, lses, zg, p["pmix"], p["pscale"], p["wbr"], p["wout"], p["g1"], p["b1"],
                       p["wr"], S, t["tm_merge"], alpha)
    idx_flat = _select(aff_t, cap).reshape(-1)
    xe = _gather(x1, idx_flat, SC_GATHER_ROWS)
    groups = 1 if feeds_next else 2
    ne = N_EXPERTS // groups
    parts = []
    for e0 in range(0, N_EXPERTS, ne):
        ye_lo, ye_hi = _ffn(xe, p["wg"], p["wu"], p["wd"], p["wr"], cap, t["tc"], e0, ne)
        parts.append((ye_lo, ye_hi, idx_flat[e0 * cap:(e0 + ne) * cap]))
    y_lo, y_hi = _scatter_add(N, parts, cap, SC_GATHER_ROWS)
    return _ln2(x1, y_lo, y_hi, p["g2"], p["b2"], B, S, t["tm_proj"], alpha, FOLD_DILATIONS if feeds_next else ())


def _trunk(x, layers, alpha):
    B, S, D = x.shape
    rot_b = _rot_tables(S, B_HEAD_DIM)
    half_c, tabs_c = _rot_tables_spread(S, C_HEAD_DIM)
    rot_c = {d: (half_c, tuple(_fold_rows(tab, 1, S, d) for tab in tabs_c)) for _, d in C_CONFIGS}
    xf = x.reshape(B * S, D)
    xb = xf.astype(BF16)
    xbs = {d: _fold_rows(xb, B, S, d) for _, d in C_CONFIGS}
    for l, p in enumerate(layers):
        xf, xbs = _layer(xf, xbs, p, B, S, alpha, rot_b, rot_c, feeds_next=l + 1 < len(layers))
    return xf.reshape(B, S, D)


def kernel(x_prompt, x_sample, w_in, pool_mix, pool_scale, sink_logit, w_branch, w_out, ln1_g, ln1_b, w_router,
           w_gate_e, w_up_e, w_down_e, ln2_g, ln2_b):
    depth = w_in.shape[0]
    alpha = (2 * depth) ** 0.25
    layers = [_prep_layer(l, w_in, pool_mix, pool_scale, sink_logit, w_branch, w_out, ln1_g, ln1_b, w_router,
                          w_gate_e, w_up_e, w_down_e, ln2_g, ln2_b) for l in range(depth)]
    y_sample = _trunk(x_sample, layers, alpha)
    y_prompt = _trunk(x_prompt, layers, alpha)
    return (y_prompt, y_sample)
```
